```python
import jax, jax.numpy as jnp
from jax import lax
import numpy as np

D_MODEL = 1024
BATCH = 1
SEQ = 16384
DEPTH = 1
DEC_BATCH = 16
DEC_SEQ = 32
PAST_LEN = 2048

CHUNK = 64
N_META = 16
GLA_HEADS = 4
GLA_DK = D_MODEL // 2
GLA_DV = D_MODEL
DK_HEAD = GLA_DK // GLA_HEADS
DV_HEAD = GLA_DV // GLA_HEADS
GATE_RANK = 16
GATE_TAU = 16.0
CONV_DIM = D_MODEL
CONV_WIDTH = 3
N_EXPERTS = 256
TOP_K = 8
N_GROUPS = 8
TOPK_GROUPS = 4
EXPERT_HIDDEN = D_MODEL // 4
SHARED_HIDDEN = D_MODEL // 4
ROUTED_SCALE = 2.5
EXPERT_BLOCK = 64
LN_EPS = 1e-5
RMS_EPS = 1e-6
DEEPNORM_ALPHA = (2.0 * DEPTH) ** 0.25
DEEPNORM_BETA = (8.0 * DEPTH) ** -0.25
PROJ_SPLITS = (GLA_DK, GLA_DK, GLA_DV, GLA_DV, GATE_RANK, CONV_DIM, CONV_DIM, CONV_DIM, D_MODEL, D_MODEL)
PROJ_WIDTH = sum(PROJ_SPLITS)
PROJ_OFFSETS = [int(o) for o in np.cumsum(PROJ_SPLITS)[:-1]]

kernel_name = 'hybrid_gla_shortconv_moe_stream_step'


def layer_norm(x, g, b):
    xf = x.astype(jnp.float32)
    mu = xf.mean(-1, keepdims=True)
    var = jnp.square(xf - mu).mean(-1, keepdims=True)
    return ((xf - mu) * lax.rsqrt(var + LN_EPS) * g.astype(jnp.float32) + b.astype(jnp.float32)).astype(x.dtype)


def gla_block(q, k, v, gk, s0):
    length = q.shape[1]
    qf, kf, vf = q.astype(jnp.float32), k.astype(jnp.float32), v.astype(jnp.float32)
    s0f = s0.astype(jnp.float32)
    b = jnp.cumsum(gk.astype(jnp.float32), axis=1)
    o_inter = jnp.einsum('blhk,bhkv->blhv', qf * jnp.exp(b), s0f)
    causal = jnp.tril(jnp.ones((length, length), dtype=bool))[None, :, :, None, None]
    diff = b[:, :, None] - b[:, None, :]
    decay = jnp.exp(jnp.where(causal, diff, -jnp.inf))
    scores = jnp.einsum('bthk,bshk,btshk->bhts', qf, kf, decay)
    o_intra = jnp.einsum('bhts,bshv->bthv', scores, vf)
    b_last = b[:, -1]
    s_new = jnp.exp(b_last)[..., None] * s0f + jnp.einsum(
        'bshk,bshv->bhkv', kf * jnp.exp(b_last[:, None] - b), vf)
    return o_inter + o_intra, s_new


def gla_prompt(q, k, v, gk):
    bsz, total = q.shape[:2]
    n_frames = total - N_META
    n_chunks = n_frames // CHUNK
    s0 = jnp.zeros((bsz, GLA_HEADS, DK_HEAD, DV_HEAD), jnp.float32)
    o_meta, s_meta = gla_block(q[:, :N_META], k[:, :N_META], v[:, :N_META], gk[:, :N_META], s0)

    def to_chunks(a):
        a = a[:, N_META:]
        return jnp.moveaxis(a.reshape(bsz, n_chunks, CHUNK, *a.shape[2:]), 1, 0)

    def step(s, blk):
        o, s_next = gla_block(*blk, s)
        return s_next, o

    s_final, o_chunks = lax.scan(step, s_meta, (to_chunks(q), to_chunks(k), to_chunks(v), to_chunks(gk)))
    o_frames = jnp.moveaxis(o_chunks, 0, 1).reshape(bsz, n_frames, GLA_HEADS, DV_HEAD)
    return jnp.concatenate([o_meta, o_frames], axis=1), s_final


def short_conv(z_pad, conv_w):
    length = z_pad.shape[1] - (CONV_WIDTH - 1)
    out = conv_w[0] * z_pad[:, 0:length]
    for j in range(1, CONV_WIDTH):
        out = out + conv_w[j] * z_pad[:, j:j + length]
    return out


def swiglu(x, w_gate, w_up, w_down):
    return (jax.nn.silu(x @ w_gate) * (x @ w_up)) @ w_down


def route(xt, w_router, router_bias):
    n_tok = xt.shape[0]
    scores = jax.nn.sigmoid(jnp.einsum('nd,de->ne', xt, w_router).astype(jnp.float32))
    biased = scores + router_bias.astype(jnp.float32)
    grouped = biased.reshape(n_tok, N_GROUPS, N_EXPERTS // N_GROUPS)
    group_score = lax.top_k(grouped, 2)[0].sum(-1)
    _, top_groups = lax.top_k(group_score, TOPK_GROUPS)
    group_keep = (top_groups[..., None] == jnp.arange(N_GROUPS)).any(axis=1)
    expert_keep = jnp.repeat(group_keep, N_EXPERTS // N_GROUPS, axis=-1)
    _, idx = lax.top_k(jnp.where(expert_keep, biased, -jnp.inf), TOP_K)
    w = jnp.take_along_axis(scores, idx, axis=-1)
    w = w / w.sum(-1, keepdims=True) * ROUTED_SCALE
    return idx, w


def routed_experts(xt, idx, wts, w_gate, w_up, w_down):
    n_tok, d = xt.shape
    n_assign = n_tok * TOP_K
    n_blocks = -(-(n_assign + N_EXPERTS * (EXPERT_BLOCK - 1)) // EXPERT_BLOCK)
    n_rows = n_blocks * EXPERT_BLOCK
    flat_e = idx.reshape(-1)
    flat_t = jnp.arange(n_assign, dtype=jnp.int32) // TOP_K
    flat_w = wts.reshape(-1)
    order = jnp.argsort(flat_e)
    e_sorted = flat_e[order]
    counts = jnp.bincount(flat_e, length=N_EXPERTS)
    padded = (counts + EXPERT_BLOCK - 1) // EXPERT_BLOCK * EXPERT_BLOCK
    pad_end = jnp.cumsum(padded)
    pad_start = pad_end - padded
    start = jnp.cumsum(counts) - counts
    dest = pad_start[e_sorted] + jnp.arange(n_assign) - start[e_sorted]
    row_tok = jnp.full((n_rows,), n_tok, jnp.int32).at[dest].set(flat_t[order])
    row_w = jnp.zeros((n_rows,), flat_w.dtype).at[dest].set(flat_w[order])
    block_exp = jnp.minimum(
        jnp.searchsorted(pad_end, jnp.arange(n_blocks) * EXPERT_BLOCK, side='right'), N_EXPERTS - 1)
    x_pad = jnp.concatenate([xt, jnp.zeros((1, d), xt.dtype)], axis=0)

    def step(acc, blk):
        tok, w, e = blk
        xb = x_pad[tok]
        yb = swiglu(xb, w_gate[e], w_up[e], w_down[e])
        return acc.at[tok].add(yb * w[:, None].astype(yb.dtype)), None

    acc, _ = lax.scan(step, jnp.zeros((n_tok + 1, d), xt.dtype),
                      (row_tok.reshape(n_blocks, EXPERT_BLOCK), row_w.reshape(n_blocks, EXPERT_BLOCK), block_exp))
    return acc[:n_tok]


def moe_ffn(h, lp):
    bsz, length, d = h.shape
    xt = h.reshape(bsz * length, d)
    idx, wts = route(xt, lp['w_router'], lp['router_bias'])
    routed = routed_experts(xt, idx, wts, lp['w_exp_gate'], lp['w_exp_up'], lp['w_exp_down'])
    shared = swiglu(xt, lp['w_sh_gate'], lp['w_sh_up'], lp['w_sh_down'])
    return (routed + shared).reshape(bsz, length, d)


def layer_forward(h, gla_state, conv_state, lp):
    bsz, length, _ = h.shape
    p = jnp.einsum('bld,de->ble', h, lp['w_in'])
    q, k, v, g, a, conv_b, conv_c, conv_h, gate_a, gate_b = jnp.split(p, PROJ_OFFSETS, axis=-1)
    gk = jax.nn.log_sigmoid(jnp.einsum('blr,rk->blk', a, lp['w_gate_a2']).astype(jnp.float32)
                            + lp['b_gate'].astype(jnp.float32)) / GATE_TAU
    q = (q * DK_HEAD ** -0.5).reshape(bsz, length, GLA_HEADS, DK_HEAD)
    k = k.reshape(bsz, length, GLA_HEADS, DK_HEAD)
    v = v.reshape(bsz, length, GLA_HEADS, DV_HEAD)
    gk = gk.reshape(bsz, length, GLA_HEADS, DK_HEAD)
    z = conv_c * conv_h
    if gla_state is None:
        o, s_new = gla_prompt(q, k, v, gk)
        z_pad = jnp.pad(z, ((0, 0), (CONV_WIDTH - 1, 0), (0, 0)))
    else:
        o, s_new = gla_block(q, k, v, gk, gla_state)
        z_pad = jnp.concatenate([conv_state.astype(z.dtype), z], axis=1)
    y_conv = conv_b * short_conv(z_pad, lp['conv_w'])
    o = o * lax.rsqrt(jnp.square(o).mean(-1, keepdims=True) + RMS_EPS) * lp['gla_norm_g'].astype(jnp.float32)
    o = o.reshape(bsz, length, GLA_DV).astype(h.dtype) * jax.nn.silu(g)
    branch_a = o @ lp['w_gla_out']
    branch_b = y_conv @ lp['w_conv_out']
    mixed = (jax.nn.sigmoid(gate_a) * branch_a + jax.nn.sigmoid(gate_b) * branch_b) @ lp['w_o']
    h = layer_norm(DEEPNORM_ALPHA * h + mixed, lp['ln1_g'], lp['ln1_b'])
    h = layer_norm(DEEPNORM_ALPHA * h + moe_ffn(h, lp), lp['ln2_g'], lp['ln2_b'])
    return h, s_new, z_pad[:, -(CONV_WIDTH - 1):]


def setup_inputs(seed: int = 0) -> dict:
    key = jax.random.key(seed)
    ks = jax.random.split(key, 32)

    def nrm(k, shape, scale):
        return jax.random.normal(k, shape, jnp.float32) * scale

    L = DEPTH
    return {
        'x_prompt': nrm(ks[0], (BATCH, SEQ, D_MODEL), 1.0),
        'x_sample': nrm(ks[1], (DEC_BATCH, DEC_SEQ, D_MODEL), 1.0),
        'state_gla': nrm(ks[2], (L, DEC_BATCH, GLA_HEADS, DK_HEAD, DV_HEAD), 0.1),
        'state_conv': nrm(ks[3], (L, DEC_BATCH, CONV_WIDTH - 1, CONV_DIM), 1.0),
        'meta_tokens': nrm(ks[4], (N_META, D_MODEL), 1.0),
        'ln_emb_g': 1.0 + nrm(ks[5], (D_MODEL,), 0.05),
        'ln_emb_b': nrm(ks[6], (D_MODEL,), 0.02),
        'w_in': nrm(ks[7], (L, D_MODEL, PROJ_WIDTH), D_MODEL ** -0.5),
        'w_gate_a2': nrm(ks[8], (L, GATE_RANK, GLA_DK), GATE_RANK ** -0.5),
        'b_gate': nrm(ks[9], (L, GLA_DK), 0.1),
        'gla_norm_g': 1.0 + nrm(ks[10], (L, DV_HEAD), 0.05),
        'w_gla_out': nrm(ks[11], (L, GLA_DV, D_MODEL), GLA_DV ** -0.5),
        'conv_w': nrm(ks[12], (L, CONV_WIDTH, CONV_DIM), CONV_WIDTH ** -0.5),
        'w_conv_out': nrm(ks[13], (L, CONV_DIM, D_MODEL), CONV_DIM ** -0.5),
        'w_o': nrm(ks[14], (L, D_MODEL, D_MODEL), D_MODEL ** -0.5 * DEEPNORM_BETA),
        'ln1_g': 1.0 + nrm(ks[15], (L, D_MODEL), 0.05),
        'ln1_b': nrm(ks[16], (L, D_MODEL), 0.02),
        'w_router': nrm(ks[17], (L, D_MODEL, N_EXPERTS), D_MODEL ** -0.5),
        'router_bias': nrm(ks[18], (L, N_EXPERTS), 0.01),
        'w_exp_gate': nrm(ks[19], (L, N_EXPERTS, D_MODEL, EXPERT_HIDDEN), D_MODEL ** -0.5),
        'w_exp_up': nrm(ks[20], (L, N_EXPERTS, D_MODEL, EXPERT_HIDDEN), D_MODEL ** -0.5),
        'w_exp_down': nrm(ks[21], (L, N_EXPERTS, EXPERT_HIDDEN, D_MODEL), EXPERT_HIDDEN ** -0.5 * DEEPNORM_BETA),
        'w_sh_gate': nrm(ks[22], (L, D_MODEL, SHARED_HIDDEN), D_MODEL ** -0.5),
        'w_sh_up': nrm(ks[23], (L, D_MODEL, SHARED_HIDDEN), D_MODEL ** -0.5),
        'w_sh_down': nrm(ks[24], (L, SHARED_HIDDEN, D_MODEL), SHARED_HIDDEN ** -0.5 * DEEPNORM_BETA),
        'ln2_g': 1.0 + nrm(ks[25], (L, D_MODEL), 0.05),
        'ln2_b': nrm(ks[26], (L, D_MODEL), 0.02),
    }


def reference(x_prompt, x_sample, state_gla, state_conv, meta_tokens, ln_emb_g, ln_emb_b,
              w_in, w_gate_a2, b_gate, gla_norm_g, w_gla_out, conv_w, w_conv_out, w_o,
              ln1_g, ln1_b, w_router, router_bias, w_exp_gate, w_exp_up, w_exp_down,
              w_sh_gate, w_sh_up, w_sh_down, ln2_g, ln2_b):
    meta = jnp.broadcast_to(meta_tokens.astype(x_prompt.dtype)[None], (x_prompt.shape[0], N_META, D_MODEL))
    h_p = layer_norm(jnp.concatenate([meta, x_prompt], axis=1), ln_emb_g, ln_emb_b)
    h_s = layer_norm(x_sample, ln_emb_g, ln_emb_b)
    gla_p, conv_p, gla_s, conv_s = [], [], [], []
    for l in range(DEPTH):
        lp = {
            'w_in': w_in[l], 'w_gate_a2': w_gate_a2[l], 'b_gate': b_gate[l], 'gla_norm_g': gla_norm_g[l],
            'w_gla_out': w_gla_out[l], 'conv_w': conv_w[l], 'w_conv_out': w_conv_out[l], 'w_o': w_o[l],
            'ln1_g': ln1_g[l], 'ln1_b': ln1_b[l], 'w_router': w_router[l], 'router_bias': router_bias[l],
            'w_exp_gate': w_exp_gate[l], 'w_exp_up': w_exp_up[l], 'w_exp_down': w_exp_down[l],
            'w_sh_gate': w_sh_gate[l], 'w_sh_up': w_sh_up[l], 'w_sh_down': w_sh_down[l],
            'ln2_g': ln2_g[l], 'ln2_b': ln2_b[l],
        }
        h_p, sg_p, sc_p = layer_forward(h_p, None, None, lp)
        h_s, sg_s, sc_s = layer_forward(h_s, state_gla[l], state_conv[l], lp)
        gla_p.append(sg_p)
        conv_p.append(sc_p)
        gla_s.append(sg_s)
        conv_s.append(sc_s)
    y_prompt = h_p[:, N_META:]
    y_sample = h_s
    state_gla_prompt = jnp.stack(gla_p).astype(state_gla.dtype)
    state_conv_prompt = jnp.stack(conv_p).astype(state_conv.dtype)
    state_gla_sample = jnp.stack(gla_s).astype(state_gla.dtype)
    state_conv_sample = jnp.stack(conv_s).astype(state_conv.dtype)
    return (y_prompt, y_sample, state_gla_prompt, state_conv_prompt, state_gla_sample, state_conv_sample)
```

```python
import functools

import jax
import jax.numpy as jnp
from jax import lax
from jax.experimental import pallas as pl
from jax.experimental.pallas import tpu as pltpu

F32 = jnp.float32
BF16 = jnp.bfloat16
I32 = jnp.int32

D_MODEL = 1024
N_META = 16
HEADS = 4
DK = 128
DV = 256
GLA_DK = HEADS * DK
GLA_DV = HEADS * DV
GATE_RANK = 16
GATE_TAU = 16.0
N_EXPERTS = 256
TOP_K = 8
N_GROUPS = 8
GROUP_SIZE = N_EXPERTS // N_GROUPS
TOPK_GROUPS = 4
EXPERT_HIDDEN = 256
SHARED_HIDDEN = 256
ROUTED_SCALE = 2.5
LN_EPS = 1e-5
RMS_EPS = 1e-6
ALPHA = 2.0 ** 0.25
QSCALE = DK ** -0.5

Q0, K0, V0, G0, CB0, CC0, CH0, GA0, GB0 = 0, 512, 1024, 2048, 3072, 4096, 5120, 6144, 7168
W_MAIN_COLS = 8192
RANK_PAD = 128

SUB = 16
PROMPT_CHUNK = 64
TM = 256
TN_ROUTER = 512
BM = 256
TN_COMBINE = 256
VMEM_LIMIT = 60 * 1024 * 1024

_DN_TB = (((1,), (1,)), ((), ()))
_DN_TA = (((0,), (0,)), ((), ()))


def _dot(a, b):
    return jnp.dot(a, b, preferred_element_type=F32)


def _layer_norm(x, g, b):
    mu = jnp.mean(x, axis=-1, keepdims=True)
    xc = x - mu
    var = jnp.mean(xc * xc, axis=-1, keepdims=True)
    return xc * lax.rsqrt(var + LN_EPS) * g + b


def _sigmoid(x):
    return 1.0 / (1.0 + jnp.exp(-x))


def _log_sigmoid(x):
    return jnp.minimum(x, 0.0) - jnp.log1p(jnp.exp(-jnp.abs(x)))


def _chunk_cumsum(g, chunk):
    t = g.shape[0]
    shift = chunk.bit_length() - 1
    r = lax.broadcasted_iota(I32, (t, t), 0)
    c = lax.broadcasted_iota(I32, (t, t), 1)
    tri = jnp.where(((r >> shift) == (c >> shift)) & (c <= r), 1.0, 0.0).astype(BF16)
    g1 = g.astype(BF16)
    r1 = g - g1.astype(F32)
    g2 = r1.astype(BF16)
    g3 = (r1 - g2.astype(F32)).astype(BF16)
    return _dot(tri, g1) + _dot(tri, g2) + _dot(tri, g3)


def _gate_log_decay(hb, wa_ref, wa2_ref, bg_ref):
    a = _dot(hb, wa_ref[...])
    return _log_sigmoid(_dot(a.astype(BF16), wa2_ref[...]) + bg_ref[...]) * (1.0 / GATE_TAU)


def _gla_state_update(k, v, b, st):
    n = k.shape[0]
    b_last = b[n - 1:n, :]
    khat = (k * jnp.exp(b_last - b)).astype(BF16)
    return st * jnp.exp(b_last) + lax.dot_general(v.astype(BF16), khat, _DN_TA, preferred_element_type=F32)


def _gla_chunk(q, k, v, b, st):
    n = q.shape[0]
    vb = v.astype(BF16)
    q0 = (q * (jnp.exp(b) * QSCALE)).astype(BF16)
    o = lax.dot_general(q0, st.astype(BF16), _DN_TB, preferred_element_type=F32)
    col = lax.broadcasted_iota(I32, (SUB, n), 1)
    row = lax.broadcasted_iota(I32, (SUB, n), 0)
    blocks = []
    for i in range(n // SUB):
        lo, hi = SUB * i, SUB * (i + 1)
        bref = b[lo:lo + 1, :]
        qi = (q[lo:hi] * (jnp.exp(b[lo:hi] - bref) * QSCALE)).astype(BF16)
        kk = (k[:hi] * jnp.exp(bref - b[:hi])).astype(BF16)
        if hi < n:
            kk = jnp.concatenate([kk, jnp.zeros((n - hi, DK), BF16)], axis=0)
        a = lax.dot_general(qi, kk, _DN_TB, preferred_element_type=F32)
        blocks.append(jnp.where(col <= row + lo, a, 0.0))
    scores = blocks[0] if len(blocks) == 1 else jnp.concatenate(blocks, axis=0)
    o = o + _dot(scores.astype(BF16), vb)
    return o, _gla_state_update(k, v, b, st)


def _mixer_back(h, hb, o_ref, conv, wm_ref, ng_ref, wgo_ref, wco_ref, wo_ref, l1g_ref, l1b_ref):
    g = _dot(hb, wm_ref[:, G0:G0 + GLA_DV])
    parts = []
    for hd in range(HEADS):
        oh = o_ref[:, hd * DV:(hd + 1) * DV]
        ms = jnp.mean(oh * oh, axis=-1, keepdims=True)
        parts.append(oh * lax.rsqrt(ms + RMS_EPS) * ng_ref[...])
    on = jnp.concatenate(parts, axis=1) * (g * _sigmoid(g))
    branch_a = _dot(on.astype(BF16), wgo_ref[...])
    mix = _sigmoid(_dot(hb, wm_ref[:, GA0:GA0 + D_MODEL])) * branch_a
    yc = _dot(hb, wm_ref[:, CB0:CB0 + D_MODEL]) * conv
    branch_b = _dot(yc.astype(BF16), wco_ref[...])
    mix = mix + _sigmoid(_dot(hb, wm_ref[:, GB0:GB0 + D_MODEL])) * branch_b
    mixed = _dot(mix.astype(BF16), wo_ref[...])
    return _layer_norm(ALPHA * h + mixed, l1g_ref[...], l1b_ref[...])


def _conv_input(hb, wm_ref):
    return _dot(hb, wm_ref[:, CC0:CC0 + D_MODEL]) * _dot(hb, wm_ref[:, CH0:CH0 + D_MODEL])


def _gla_rows(qkv_ref, b_ref, rows, hd, st):
    kc = slice(hd * DK, (hd + 1) * DK)
    return _gla_chunk(qkv_ref[rows, Q0 + hd * DK:Q0 + (hd + 1) * DK],
                      qkv_ref[rows, K0 + hd * DK:K0 + (hd + 1) * DK],
                      qkv_ref[rows, V0 + hd * DV:V0 + (hd + 1) * DV],
                      b_ref[rows, kc], st)


def _mixer_kernel(meta_ref, xp_ref, xs_ref, sgin_ref, scin_ref,
                  lng_ref, lnb_ref, wm_ref, wa_ref, wa2_ref, bg_ref,
                  ng_ref, wgo_ref, cw_ref, wco_ref, wo_ref, l1g_ref, l1b_ref,
                  h1_ref, sgp_ref, scp_ref, sgs_ref, scs_ref,
                  st_ref, zbuf_ref, qkv_ref, b_ref, o_ref, *, n_prompt_tiles, seq, streams):
    i = pl.program_id(0)
    back = functools.partial(_mixer_back, wm_ref=wm_ref, ng_ref=ng_ref, wgo_ref=wgo_ref, wco_ref=wco_ref,
                             wo_ref=wo_ref, l1g_ref=l1g_ref, l1b_ref=l1b_ref)

    def front(x):
        h = _layer_norm(x, lng_ref[...], lnb_ref[...])
        hb = h.astype(BF16)
        return h, hb, _gate_log_decay(hb, wa_ref, wa2_ref, bg_ref)

    @pl.when(i == 0)
    def _meta():
        _, hb, gk = front(meta_ref[...])
        k = _dot(hb, wm_ref[:, K0:K0 + GLA_DK])
        v = _dot(hb, wm_ref[:, V0:V0 + GLA_DV])
        b = _chunk_cumsum(gk, N_META)
        zero_state = jnp.zeros((DV, DK), F32)
        for hd in range(HEADS):
            kc = slice(hd * DK, (hd + 1) * DK)
            st_ref[hd] = _gla_state_update(k[:, kc], v[:, hd * DV:(hd + 1) * DV], b[:, kc], zero_state)
        zbuf_ref[0:8, :] = _conv_input(hb, wm_ref)[N_META - 8:N_META, :]

    @pl.when((i >= 1) & (i <= n_prompt_tiles))
    def _prompt_tile():
        h, hb, gk = front(xp_ref[...])
        qkv_ref[...] = _dot(hb, wm_ref[:, Q0:G0])
        b_ref[...] = _chunk_cumsum(gk, PROMPT_CHUNK)
        for c in range(TM // PROMPT_CHUNK):
            rows = slice(c * PROMPT_CHUNK, (c + 1) * PROMPT_CHUNK)
            for hd in range(HEADS):
                o, st_new = _gla_rows(qkv_ref, b_ref, rows, hd, st_ref[hd])
                st_ref[hd] = st_new
                o_ref[rows, hd * DV:(hd + 1) * DV] = o
        z = _conv_input(hb, wm_ref)
        zbuf_ref[8:8 + TM, :] = z
        cw = cw_ref[...]
        conv = cw[0:1, :] * zbuf_ref[6:6 + TM, :] + cw[1:2, :] * zbuf_ref[7:7 + TM, :] + cw[2:3, :] * z
        zbuf_ref[0:8, :] = z[TM - 8:TM, :]
        h1_ref[...] = back(h, hb, o_ref, conv)

        @pl.when(i == n_prompt_tiles)
        def _prompt_states():
            for hd in range(HEADS):
                sgp_ref[hd] = st_ref[hd].T
            scp_ref[...] = z[TM - 2:TM, :]

    @pl.when(i > n_prompt_tiles)
    def _sample_tile():
        h, hb, gk = front(xs_ref[...])
        qkv_ref[...] = _dot(hb, wm_ref[:, Q0:G0])
        b_ref[...] = _chunk_cumsum(gk, seq)
        z = _conv_input(hb, wm_ref)
        pitch = seq + 8
        cw = cw_ref[...]
        convs = []
        for s in range(streams):
            rows = slice(s * seq, (s + 1) * seq)
            for hd in range(HEADS):
                o, st_new = _gla_rows(qkv_ref, b_ref, rows, hd, sgin_ref[s, hd].T)
                sgs_ref[s, hd] = st_new.T
                o_ref[rows, hd * DV:(hd + 1) * DV] = o
            zs = z[rows, :]
            base = s * pitch
            zbuf_ref[base + 6:base + 8, :] = scin_ref[s]
            zbuf_ref[base + 8:base + 8 + seq, :] = zs
            convs.append(cw[0:1, :] * zbuf_ref[base + 6:base + 6 + seq, :]
                         + cw[1:2, :] * zbuf_ref[base + 7:base + 7 + seq, :] + cw[2:3, :] * zs)
            scs_ref[s] = zs[seq - 2:seq, :]
        h1_ref[...] = back(h, hb, o_ref, jnp.concatenate(convs, axis=0))


def _const_spec(shape):
    nd = len(shape)
    return pl.BlockSpec(shape, lambda i, _nd=nd: (0,) * _nd, pipeline_mode=pl.Buffered(1))


def _mixer_weight_specs():
    return [
        _const_spec((1, D_MODEL)), _const_spec((1, D_MODEL)),
        _const_spec((D_MODEL, W_MAIN_COLS)),
        _const_spec((D_MODEL, RANK_PAD)), _const_spec((RANK_PAD, GLA_DK)), _const_spec((1, GLA_DK)),
        _const_spec((1, DV)), _const_spec((GLA_DV, D_MODEL)),
        _const_spec((3, D_MODEL)), _const_spec((D_MODEL, D_MODEL)),
        _const_spec((D_MODEL, D_MODEL)),
        _const_spec((1, D_MODEL)), _const_spec((1, D_MODEL)),
    ]


def _mixer(meta, x_prompt, x_sample, state_gla, state_conv, weights):
    n_streams = state_conv.shape[0]
    seq = x_sample.shape[0] // n_streams
    streams = TM // seq
    npt = x_prompt.shape[0] // TM
    nst = n_streams // streams
    ptile = lambda i: (jnp.clip(i - 1, 0, npt - 1), 0)
    stile = lambda i: jnp.clip(i - 1 - npt, 0, nst - 1)
    kern = functools.partial(_mixer_kernel, n_prompt_tiles=npt, seq=seq, streams=streams)
    return pl.pallas_call(
        kern,
        grid=(1 + npt + nst,),
        in_specs=[_const_spec((N_META, D_MODEL)),
                  pl.BlockSpec((TM, D_MODEL), ptile),
                  pl.BlockSpec((TM, D_MODEL), lambda i: (stile(i), 0), pipeline_mode=pl.Buffered(1)),
                  pl.BlockSpec((streams, HEADS, DK, DV), lambda i: (stile(i), 0, 0, 0),
                               pipeline_mode=pl.Buffered(1)),
                  pl.BlockSpec((streams, 2, D_MODEL), lambda i: (stile(i), 0, 0))] + _mixer_weight_specs(),
        out_specs=[pl.BlockSpec((TM, D_MODEL), lambda i: (jnp.maximum(i - 1, 0), 0)),
                   pl.BlockSpec((HEADS, DK, DV), lambda i: (0, 0, 0)),
                   pl.BlockSpec((2, D_MODEL), lambda i: (0, 0)),
                   pl.BlockSpec((streams, HEADS, DK, DV), lambda i: (stile(i), 0, 0, 0)),
                   pl.BlockSpec((streams, 2, D_MODEL), lambda i: (stile(i), 0, 0))],
        out_shape=[jax.ShapeDtypeStruct((x_prompt.shape[0] + x_sample.shape[0], D_MODEL), F32),
                   jax.ShapeDtypeStruct((HEADS, DK, DV), F32),
                   jax.ShapeDtypeStruct((2, D_MODEL), F32),
                   jax.ShapeDtypeStruct(state_gla.shape, F32),
                   jax.ShapeDtypeStruct(state_conv.shape, F32)],
        scratch_shapes=[pltpu.VMEM((HEADS, DV, DK), F32),
                        pltpu.VMEM((max(TM + 8, streams * (seq + 8)), D_MODEL), F32),
                        pltpu.VMEM((TM, G0), F32),
                        pltpu.VMEM((TM, GLA_DK), F32),
                        pltpu.VMEM((TM, GLA_DV), F32)],
        compiler_params=pltpu.CompilerParams(dimension_semantics=("arbitrary",), vmem_limit_bytes=VMEM_LIMIT),
        name="mixer",
    )(meta, x_prompt, x_sample, state_gla, state_conv, *weights)


def _first_index_of_max(vals, ids, sentinel):
    m = jnp.max(vals, axis=0, keepdims=True)
    first = jnp.min(jnp.where(vals == m, ids, sentinel), axis=0, keepdims=True)
    return m, first


def _router_kernel(h_ref, wr_ref, bias_ref, idx_ref, w_ref):
    h = h_ref[...]
    hh = h.astype(BF16)
    hl = (h - hh.astype(F32)).astype(BF16)
    wh, wl = wr_ref[0], wr_ref[1]
    dg = functools.partial(lax.dot_general, dimension_numbers=_DN_TB, preferred_element_type=F32)
    logits = dg(wh, hh) + dg(wl, hh) + dg(wh, hl)
    scores = _sigmoid(logits)
    biased = scores + bias_ref[...]
    n_tok = biased.shape[1]
    neg_inf = jnp.float32(-jnp.inf)

    eid = lax.broadcasted_iota(I32, (N_EXPERTS, n_tok), 0).astype(F32)
    lid = lax.broadcasted_iota(I32, (GROUP_SIZE, n_tok), 0).astype(F32)
    group_scores = []
    for g in range(N_GROUPS):
        blk = biased[g * GROUP_SIZE:(g + 1) * GROUP_SIZE]
        m1, first = _first_index_of_max(blk, lid, float(GROUP_SIZE))
        m2 = jnp.max(jnp.where(lid == first, neg_inf, blk), axis=0, keepdims=True)
        group_scores.append(m1 + m2)
    gsc = jnp.concatenate(group_scores, axis=0)

    gid = lax.broadcasted_iota(I32, (N_GROUPS, n_tok), 0).astype(F32)
    keep = jnp.zeros((N_GROUPS, n_tok), F32)
    cur = gsc
    for _ in range(TOPK_GROUPS):
        _, first = _first_index_of_max(cur, gid, float(N_GROUPS))
        sel = gid == first
        keep = jnp.where(sel, 1.0, keep)
        cur = jnp.where(sel, neg_inf, cur)

    masked = jnp.concatenate(
        [jnp.where(keep[g:g + 1] > 0.5, biased[g * GROUP_SIZE:(g + 1) * GROUP_SIZE], neg_inf)
         for g in range(N_GROUPS)], axis=0)
    idxs, wts = [], []
    cur = masked
    for _ in range(TOP_K):
        _, first = _first_index_of_max(cur, eid, float(N_EXPERTS))
        sel = eid == first
        idxs.append(first)
        wts.append(jnp.sum(jnp.where(sel, scores, 0.0), axis=0, keepdims=True))
        cur = jnp.where(sel, neg_inf, cur)
    w = jnp.concatenate(wts, axis=0)
    idx_ref[...] = jnp.concatenate(idxs, axis=0).astype(I32)
    w_ref[...] = w / jnp.sum(w, axis=0, keepdims=True) * ROUTED_SCALE


def _router(h1, wr_split, bias_col):
    n = h1.shape[0]
    return pl.pallas_call(
        _router_kernel,
        grid=(n // TN_ROUTER,),
        in_specs=[pl.BlockSpec((TN_ROUTER, D_MODEL), lambda i: (i, 0)),
                  pl.BlockSpec((2, N_EXPERTS, D_MODEL), lambda i: (0, 0, 0)),
                  pl.BlockSpec((N_EXPERTS, 1), lambda i: (0, 0))],
        out_specs=[pl.BlockSpec((TOP_K, TN_ROUTER), lambda i: (0, i)),
                   pl.BlockSpec((TOP_K, TN_ROUTER), lambda i: (0, i))],
        out_shape=[jax.ShapeDtypeStruct((TOP_K, n), I32), jax.ShapeDtypeStruct((TOP_K, n), F32)],
        compiler_params=pltpu.CompilerParams(dimension_semantics=("arbitrary",), vmem_limit_bytes=VMEM_LIMIT),
        name="router",
    )(h1, wr_split, bias_col)


def _row_gather_start(idx_ref, n_rows, src_hbm, dst, sem):
    def body(r, carry):
        row = idx_ref[0, 0, r]
        pltpu.make_async_copy(src_hbm.at[pl.ds(row, 1)], dst.at[pl.ds(r, 1)], sem).start()
        return carry
    lax.fori_loop(0, n_rows, body, 0, unroll=8)


def _row_gather_wait(src_hbm, dst, sem):
    pltpu.make_async_copy(src_hbm.at[pl.ds(0, dst.shape[0])], dst, sem).wait()


def _expert_kernel(bexp_ref, nused_ref, tok_cur_ref, tok_nxt_ref, h_hbm, roww_ref, wg_ref, wu_ref, wd_ref,
                   y_ref, xbuf, wgb, wub, wdb, sem):
    b = pl.program_id(0)
    n_used = nused_ref[0]
    slot = b % 2

    @pl.when((b == 0) & (n_used > 0))
    def _first():
        _row_gather_start(tok_cur_ref, BM, h_hbm, xbuf.at[0], sem.at[0])

    @pl.when(b + 1 < n_used)
    def _prefetch():
        _row_gather_start(tok_nxt_ref, BM, h_hbm, xbuf.at[1 - slot], sem.at[1 - slot])

    @pl.when(b < n_used)
    def _compute():
        e = bexp_ref[b]
        e_prev = bexp_ref[jnp.maximum(b - 1, 0)]

        @pl.when((b == 0) | (e != e_prev))
        def _cast_weights():
            wgb[...] = wg_ref[0].astype(BF16)
            wub[...] = wu_ref[0].astype(BF16)
            wdb[...] = wd_ref[0].astype(BF16)

        _row_gather_wait(h_hbm, xbuf.at[slot], sem.at[slot])
        xb = xbuf[slot].astype(BF16)
        g = _dot(xb, wgb[...])
        u = _dot(xb, wub[...])
        act = (g * _sigmoid(g)) * u
        y_ref[...] = _dot(act.astype(BF16), wdb[...]) * roww_ref[...]

    @pl.when(b >= n_used)
    def _empty():
        y_ref[...] = jnp.zeros_like(y_ref)


def _experts(h1, block_exp, n_used, row_tok3, row_w, w_gate, w_up, w_down):
    n_blocks = row_tok3.shape[0]
    wspec = lambda shape: pl.BlockSpec(shape, lambda b, bexp, nused: (bexp[b], 0, 0))
    grid_spec = pltpu.PrefetchScalarGridSpec(
        num_scalar_prefetch=2,
        grid=(n_blocks,),
        in_specs=[pl.BlockSpec((1, 1, BM), lambda b, bexp, nused: (b, 0, 0), memory_space=pltpu.SMEM),
                  pl.BlockSpec((1, 1, BM), lambda b, bexp, nused: (jnp.minimum(b + 1, n_blocks - 1), 0, 0),
                               memory_space=pltpu.SMEM),
                  pl.BlockSpec(memory_space=pl.ANY),
                  pl.BlockSpec((BM, 1), lambda b, bexp, nused: (b, 0)),
                  wspec((1, D_MODEL, EXPERT_HIDDEN)), wspec((1, D_MODEL, EXPERT_HIDDEN)),
                  wspec((1, EXPERT_HIDDEN, D_MODEL))],
        out_specs=pl.BlockSpec((BM, D_MODEL), lambda b, bexp, nused: (b, 0)),
        scratch_shapes=[pltpu.VMEM((2, BM, D_MODEL), F32),
                        pltpu.VMEM((D_MODEL, EXPERT_HIDDEN), BF16),
                        pltpu.VMEM((D_MODEL, EXPERT_HIDDEN), BF16),
                        pltpu.VMEM((EXPERT_HIDDEN, D_MODEL), BF16),
                        pltpu.SemaphoreType.DMA((2,))],
    )
    return pl.pallas_call(
        _expert_kernel,
        grid_spec=grid_spec,
        out_shape=jax.ShapeDtypeStruct((n_blocks * BM, D_MODEL), F32),
        compiler_params=pltpu.CompilerParams(dimension_semantics=("arbitrary",), vmem_limit_bytes=VMEM_LIMIT),
        name="experts",
    )(block_exp, n_used, row_tok3, row_tok3, h1, row_w, w_gate, w_up, w_down)


def _combine_kernel(pos_cur_ref, pos_nxt_ref, y_hbm, h_ref, wsg_ref, wsu_ref, wsd_ref, l2g_ref, l2b_ref,
                    out_ref, ybuf, sem):
    t = pl.program_id(0)
    n_t = pl.num_programs(0)
    slot = t % 2
    n_rows = TOP_K * TN_COMBINE

    @pl.when(t == 0)
    def _first():
        _row_gather_start(pos_cur_ref, n_rows, y_hbm, ybuf.at[0], sem.at[0])

    @pl.when(t + 1 < n_t)
    def _prefetch():
        _row_gather_start(pos_nxt_ref, n_rows, y_hbm, ybuf.at[1 - slot], sem.at[1 - slot])

    h = h_ref[...]
    hb = h.astype(BF16)
    g = _dot(hb, wsg_ref[...])
    u = _dot(hb, wsu_ref[...])
    shared = _dot(((g * _sigmoid(g)) * u).astype(BF16), wsd_ref[...])

    _row_gather_wait(y_hbm, ybuf.at[slot], sem.at[slot])
    routed = ybuf[slot, 0:TN_COMBINE, :]
    for k in range(1, TOP_K):
        routed = routed + ybuf[slot, k * TN_COMBINE:(k + 1) * TN_COMBINE, :]
    out_ref[...] = _layer_norm(ALPHA * h + (routed + shared), l2g_ref[...], l2b_ref[...])


def _combine(y_sorted, pos3, h1, wsg, wsu, wsd, l2g, l2b):
    n = h1.shape[0]
    n_t = n // TN_COMBINE
    n_rows = TOP_K * TN_COMBINE
    cspec = lambda shape: pl.BlockSpec(shape, lambda t: (0,) * len(shape))
    return pl.pallas_call(
        _combine_kernel,
        grid=(n_t,),
        in_specs=[pl.BlockSpec((1, 1, n_rows), lambda t: (t, 0, 0), memory_space=pltpu.SMEM),
                  pl.BlockSpec((1, 1, n_rows), lambda t: (jnp.minimum(t + 1, n_t - 1), 0, 0),
                               memory_space=pltpu.SMEM),
                  pl.BlockSpec(memory_space=pl.ANY),
                  pl.BlockSpec((TN_COMBINE, D_MODEL), lambda t: (t, 0)),
                  cspec((D_MODEL, SHARED_HIDDEN)), cspec((D_MODEL, SHARED_HIDDEN)), cspec((SHARED_HIDDEN, D_MODEL)),
                  cspec((1, D_MODEL)), cspec((1, D_MODEL))],
        out_specs=pl.BlockSpec((TN_COMBINE, D_MODEL), lambda t: (t, 0)),
        out_shape=jax.ShapeDtypeStruct((n, D_MODEL), F32),
        scratch_shapes=[pltpu.VMEM((2, n_rows, D_MODEL), F32), pltpu.SemaphoreType.DMA((2,))],
        compiler_params=pltpu.CompilerParams(dimension_semantics=("arbitrary",), vmem_limit_bytes=VMEM_LIMIT),
        name="combine",
    )(pos3, pos3, y_sorted, h1, wsg, wsu, wsd, l2g, l2b)


def _dispatch_plan(idx_t, w_t):
    n = idx_t.shape[1]
    n_assign = n * TOP_K
    n_blocks = -(-(n_assign + N_EXPERTS * (BM - 1)) // BM)
    flat_e = idx_t.T.reshape(-1)
    flat_w = w_t.T.reshape(-1)
    order = jnp.argsort(flat_e, stable=True).astype(I32)
    e_sorted = flat_e[order]
    counts = jnp.bincount(flat_e, length=N_EXPERTS).astype(I32)
    start = (jnp.cumsum(counts) - counts).astype(I32)
    padded = (counts + BM - 1) // BM * BM
    pad_end = jnp.cumsum(padded).astype(I32)
    pad_start = pad_end - padded
    dest = pad_start[e_sorted] + jnp.arange(n_assign, dtype=I32) - start[e_sorted]
    row_tok = jnp.zeros((n_blocks * BM,), I32).at[dest].set(order // TOP_K)
    row_w = jnp.zeros((n_blocks * BM,), F32).at[dest].set(flat_w[order])
    pos = jnp.zeros((n_assign,), I32).at[order].set(dest)
    n_used = pad_end[-1] // BM
    block_row0 = jnp.arange(n_blocks, dtype=I32) * BM
    block_exp = jnp.sum((pad_end[None, :] <= block_row0[:, None]).astype(I32), axis=1)
    last_exp = jnp.minimum(block_exp[jnp.maximum(n_used - 1, 0)], N_EXPERTS - 1)
    block_exp = jnp.where(jnp.arange(n_blocks) < n_used, jnp.minimum(block_exp, N_EXPERTS - 1), last_exp)
    pos3 = pos.reshape(n // TN_COMBINE, TN_COMBINE, TOP_K).transpose(0, 2, 1).reshape(n // TN_COMBINE, 1, -1)
    return (block_exp, n_used.reshape(1).astype(I32), row_tok.reshape(n_blocks, 1, BM),
            row_w.reshape(-1, 1), pos3)


def _pack_mixer_weights(ln_emb_g, ln_emb_b, w_in, w_gate_a2, b_gate, gla_norm_g, w_gla_out, conv_w,
                        w_conv_out, w_o, ln1_g, ln1_b):
    q, k, v, g, a, cb, cc, ch, ga, gb = jnp.split(
        w_in, [512, 1024, 2048, 3072, 3088, 4112, 5136, 6160, 7184], axis=-1)
    w_main = jnp.concatenate([q, k, v, g, cb, cc, ch, ga, gb], axis=-1).astype(BF16)
    w_a = jnp.pad(a, ((0, 0), (0, RANK_PAD - GATE_RANK))).astype(BF16)
    w_a2 = jnp.pad(w_gate_a2, ((0, RANK_PAD - GATE_RANK), (0, 0))).astype(BF16)
    row = lambda x: x.reshape(1, -1).astype(F32)
    return [row(ln_emb_g), row(ln_emb_b), w_main, w_a, w_a2, row(b_gate), row(gla_norm_g),
            w_gla_out.astype(BF16), conv_w.astype(F32), w_conv_out.astype(BF16), w_o.astype(BF16),
            row(ln1_g), row(ln1_b)]


def kernel(x_prompt, x_sample, state_gla, state_conv, meta_tokens, ln_emb_g, ln_emb_b, w_in, w_gate_a2, b_gate, gla_norm_g, w_gla_out, conv_w, w_conv_out, w_o, ln1_g, ln1_b, w_router, router_bias, w_exp_gate, w_exp_up, w_exp_down, w_sh_gate, w_sh_up, w_sh_down, ln2_g, ln2_b):
    batch, seq, _ = x_prompt.shape
    dec_batch, dec_seq, _ = x_sample.shape
    depth = w_in.shape[0]
    assert batch == 1 and depth == 1 and seq % TM == 0 and TM % dec_seq == 0 and dec_seq % SUB == 0
    n_prompt, n_sample = batch * seq, dec_batch * dec_seq
    n_total = n_prompt + n_sample
    assert n_sample % TM == 0 and n_total % TN_ROUTER == 0 and n_total % TN_COMBINE == 0

    weights = _pack_mixer_weights(ln_emb_g, ln_emb_b, w_in[0], w_gate_a2[0], b_gate[0], gla_norm_g[0],
                                  w_gla_out[0], conv_w[0], w_conv_out[0], w_o[0], ln1_g[0], ln1_b[0])
    h1, sg_p, sc_p, sg_s, sc_s = _mixer(meta_tokens.astype(F32), x_prompt.reshape(n_prompt, D_MODEL),
                                        x_sample.reshape(n_sample, D_MODEL), state_gla[0], state_conv[0], weights)

    wr_t = w_router[0].T
    wr_hi = wr_t.astype(BF16)
    wr_split = jnp.stack([wr_hi, (wr_t - wr_hi.astype(F32)).astype(BF16)])
    idx_t, w_t = _router(h1, wr_split, router_bias[0].reshape(N_EXPERTS, 1).astype(F32))

    block_exp, n_used, row_tok3, row_w, pos3 = _dispatch_plan(idx_t, w_t)
    y_sorted = _experts(h1, block_exp, n_used, row_tok3, row_w, w_exp_gate[0], w_exp_up[0], w_exp_down[0])
    y = _combine(y_sorted, pos3, h1, w_sh_gate[0].astype(BF16), w_sh_up[0].astype(BF16), w_sh_down[0].astype(BF16),
                 ln2_g[0].reshape(1, -1), ln2_b[0].reshape(1, -1))

    y_prompt = y[:n_prompt].reshape(batch, seq, D_MODEL)
    y_sample = y[n_prompt:].reshape(dec_batch, dec_seq, D_MODEL)
    return (y_prompt, y_sample,
            sg_p.reshape(depth, batch, HEADS, DK, DV), sc_p.reshape(depth, batch, 2, D_MODEL),
            sg_s.reshape(depth, dec_batch, HEADS, DK, DV), sc_s.reshape(depth, dec_batch, 2, D_MODEL))
```

```python
import functools

import jax
import jax.numpy as jnp
from jax import lax
from jax.experimental import pallas as pl
from jax.experimental.pallas import tpu as pltpu

F32 = jnp.float32
BF16 = jnp.bfloat16
I32 = jnp.int32

D_MODEL = 1024
N_META = 16
HEADS = 4
DK = 128
DV = 256
GLA_DK = HEADS * DK
GLA_DV = HEADS * DV
GATE_RANK = 16
GATE_TAU = 16.0
N_EXPERTS = 256
TOP_K = 8
N_GROUPS = 8
GROUP_SIZE = N_EXPERTS // N_GROUPS
TOPK_GROUPS = 4
EXPERT_HIDDEN = 256
SHARED_HIDDEN = 256
ROUTED_SCALE = 2.5
LN_EPS = 1e-5
RMS_EPS = 1e-6
ALPHA = 2.0 ** 0.25
QSCALE = DK ** -0.5

Q0, K0, V0, G0, CB0, CC0, CH0, GA0, GB0 = 0, 512, 1024, 2048, 3072, 4096, 5120, 6144, 7168
W_MAIN_COLS = 8192
RANK_PAD = 128

SUB = 16
PROMPT_CHUNK = 64
TM = 256
TN_ROUTER = 512
TN_DISPATCH = 256
BM = 256
TN_COMBINE = 256
VMEM_LIMIT = 60 * 1024 * 1024
SUBLANES = 8
LANES = 128
assert D_MODEL == SUBLANES * LANES

_DN_TB = (((1,), (1,)), ((), ()))
_DN_TA = (((0,), (0,)), ((), ()))


def _dot(a, b):
    return jnp.dot(a, b, preferred_element_type=F32)


def _tiles_to_rows(ref, n, base=0):
    return jnp.concatenate(
        [ref[pl.ds(SUBLANES * base + s, n, stride=SUBLANES), :] for s in range(SUBLANES)], axis=1)


def _rows_to_tiles(ref, val):
    n = val.shape[0]
    for s in range(SUBLANES):
        ref[pl.ds(s, n, stride=SUBLANES), :] = val[:, s * LANES:(s + 1) * LANES]


def _layer_norm(x, g, b):
    mu = jnp.mean(x, axis=-1, keepdims=True)
    xc = x - mu
    var = jnp.mean(xc * xc, axis=-1, keepdims=True)
    return xc * lax.rsqrt(var + LN_EPS) * g + b


def _sigmoid(x):
    return 1.0 / (1.0 + jnp.exp(-x))


def _log_sigmoid(x):
    return jnp.minimum(x, 0.0) - jnp.log1p(jnp.exp(-jnp.abs(x)))


def _chunk_cumsum(g, chunk):
    t = g.shape[0]
    shift = chunk.bit_length() - 1
    r = lax.broadcasted_iota(I32, (t, t), 0)
    c = lax.broadcasted_iota(I32, (t, t), 1)
    tri = jnp.where(((r >> shift) == (c >> shift)) & (c <= r), 1.0, 0.0).astype(BF16)
    g1 = g.astype(BF16)
    r1 = g - g1.astype(F32)
    g2 = r1.astype(BF16)
    g3 = (r1 - g2.astype(F32)).astype(BF16)
    return _dot(tri, g1) + _dot(tri, g2) + _dot(tri, g3)


def _gate_log_decay(hb, wa_ref, wa2_ref, bg_ref):
    a = _dot(hb, wa_ref[...])
    return _log_sigmoid(_dot(a.astype(BF16), wa2_ref[...]) + bg_ref[...]) * (1.0 / GATE_TAU)


def _gla_state_update(k, v, b, st):
    n = k.shape[0]
    b_last = b[n - 1:n, :]
    khat = (k * jnp.exp(b_last - b)).astype(BF16)
    return st * jnp.exp(b_last) + lax.dot_general(v.astype(BF16), khat, _DN_TA, preferred_element_type=F32)


def _gla_chunk(q, k, v, b, st):
    n = q.shape[0]
    vb = v.astype(BF16)
    q0 = (q * (jnp.exp(b) * QSCALE)).astype(BF16)
    o = lax.dot_general(q0, st.astype(BF16), _DN_TB, preferred_element_type=F32)
    col = lax.broadcasted_iota(I32, (SUB, n), 1)
    row = lax.broadcasted_iota(I32, (SUB, n), 0)
    blocks = []
    for i in range(n // SUB):
        lo, hi = SUB * i, SUB * (i + 1)
        bref = b[lo:lo + 1, :]
        qi = (q[lo:hi] * (jnp.exp(b[lo:hi] - bref) * QSCALE)).astype(BF16)
        kk = (k[:hi] * jnp.exp(bref - b[:hi])).astype(BF16)
        if hi < n:
            kk = jnp.concatenate([kk, jnp.zeros((n - hi, DK), BF16)], axis=0)
        a = lax.dot_general(qi, kk, _DN_TB, preferred_element_type=F32)
        blocks.append(jnp.where(col <= row + lo, a, 0.0))
    scores = blocks[0] if len(blocks) == 1 else jnp.concatenate(blocks, axis=0)
    o = o + _dot(scores.astype(BF16), vb)
    return o, _gla_state_update(k, v, b, st)


def _mixer_back(h, hb, o_ref, conv, wm_ref, ng_ref, wgo_ref, wco_ref, wo_ref, l1g_ref, l1b_ref):
    g = _dot(hb, wm_ref[:, G0:G0 + GLA_DV])
    parts = []
    for hd in range(HEADS):
        oh = o_ref[:, hd * DV:(hd + 1) * DV]
        ms = jnp.mean(oh * oh, axis=-1, keepdims=True)
        parts.append(oh * lax.rsqrt(ms + RMS_EPS) * ng_ref[...])
    on = jnp.concatenate(parts, axis=1) * (g * _sigmoid(g))
    branch_a = _dot(on.astype(BF16), wgo_ref[...])
    mix = _sigmoid(_dot(hb, wm_ref[:, GA0:GA0 + D_MODEL])) * branch_a
    yc = _dot(hb, wm_ref[:, CB0:CB0 + D_MODEL]) * conv
    branch_b = _dot(yc.astype(BF16), wco_ref[...])
    mix = mix + _sigmoid(_dot(hb, wm_ref[:, GB0:GB0 + D_MODEL])) * branch_b
    mixed = _dot(mix.astype(BF16), wo_ref[...])
    return _layer_norm(ALPHA * h + mixed, l1g_ref[...], l1b_ref[...])


def _conv_input(hb, wm_ref):
    return _dot(hb, wm_ref[:, CC0:CC0 + D_MODEL]) * _dot(hb, wm_ref[:, CH0:CH0 + D_MODEL])


def _gla_rows(qkv_ref, b_ref, rows, hd, st):
    kc = slice(hd * DK, (hd + 1) * DK)
    return _gla_chunk(qkv_ref[rows, Q0 + hd * DK:Q0 + (hd + 1) * DK],
                      qkv_ref[rows, K0 + hd * DK:K0 + (hd + 1) * DK],
                      qkv_ref[rows, V0 + hd * DV:V0 + (hd + 1) * DV],
                      b_ref[rows, kc], st)


def _mixer_kernel(meta_ref, xp_ref, xs_ref, sgin_ref, scin_ref,
                  lng_ref, lnb_ref, wm_ref, wa_ref, wa2_ref, bg_ref,
                  ng_ref, wgo_ref, cw_ref, wco_ref, wo_ref, l1g_ref, l1b_ref,
                  h1_ref, sgp_ref, scp_ref, sgs_ref, scs_ref,
                  st_ref, zbuf_ref, qkv_ref, b_ref, o_ref, *, n_prompt_tiles, seq, streams):
    i = pl.program_id(0)
    back = functools.partial(_mixer_back, wm_ref=wm_ref, ng_ref=ng_ref, wgo_ref=wgo_ref, wco_ref=wco_ref,
                             wo_ref=wo_ref, l1g_ref=l1g_ref, l1b_ref=l1b_ref)

    def front(x):
        h = _layer_norm(x, lng_ref[...], lnb_ref[...])
        hb = h.astype(BF16)
        return h, hb, _gate_log_decay(hb, wa_ref, wa2_ref, bg_ref)

    @pl.when(i == 0)
    def _meta():
        _, hb, gk = front(meta_ref[...])
        k = _dot(hb, wm_ref[:, K0:K0 + GLA_DK])
        v = _dot(hb, wm_ref[:, V0:V0 + GLA_DV])
        b = _chunk_cumsum(gk, N_META)
        zero_state = jnp.zeros((DV, DK), F32)
        for hd in range(HEADS):
            kc = slice(hd * DK, (hd + 1) * DK)
            st_ref[hd] = _gla_state_update(k[:, kc], v[:, hd * DV:(hd + 1) * DV], b[:, kc], zero_state)
        zbuf_ref[0:8, :] = _conv_input(hb, wm_ref)[N_META - 8:N_META, :]

    @pl.when((i >= 1) & (i <= n_prompt_tiles))
    def _prompt_tile():
        h, hb, gk = front(xp_ref[...])
        qkv_ref[...] = _dot(hb, wm_ref[:, Q0:G0])
        b_ref[...] = _chunk_cumsum(gk, PROMPT_CHUNK)
        for c in range(TM // PROMPT_CHUNK):
            rows = slice(c * PROMPT_CHUNK, (c + 1) * PROMPT_CHUNK)
            for hd in range(HEADS):
                o, st_new = _gla_rows(qkv_ref, b_ref, rows, hd, st_ref[hd])
                st_ref[hd] = st_new
                o_ref[rows, hd * DV:(hd + 1) * DV] = o
        z = _conv_input(hb, wm_ref)
        zbuf_ref[8:8 + TM, :] = z
        cw = cw_ref[...]
        conv = cw[0:1, :] * zbuf_ref[6:6 + TM, :] + cw[1:2, :] * zbuf_ref[7:7 + TM, :] + cw[2:3, :] * z
        zbuf_ref[0:8, :] = z[TM - 8:TM, :]
        _rows_to_tiles(h1_ref, back(h, hb, o_ref, conv))

        @pl.when(i == n_prompt_tiles)
        def _prompt_states():
            for hd in range(HEADS):
                sgp_ref[hd] = st_ref[hd].T
            scp_ref[...] = z[TM - 2:TM, :]

    @pl.when(i > n_prompt_tiles)
    def _sample_tile():
        h, hb, gk = front(xs_ref[...])
        qkv_ref[...] = _dot(hb, wm_ref[:, Q0:G0])
        b_ref[...] = _chunk_cumsum(gk, seq)
        z = _conv_input(hb, wm_ref)
        pitch = seq + 8
        cw = cw_ref[...]
        convs = []
        for s in range(streams):
            rows = slice(s * seq, (s + 1) * seq)
            for hd in range(HEADS):
                o, st_new = _gla_rows(qkv_ref, b_ref, rows, hd, sgin_ref[s, hd].T)
                sgs_ref[s, hd] = st_new.T
                o_ref[rows, hd * DV:(hd + 1) * DV] = o
            zs = z[rows, :]
            base = s * pitch
            zbuf_ref[base + 6:base + 8, :] = scin_ref[s]
            zbuf_ref[base + 8:base + 8 + seq, :] = zs
            convs.append(cw[0:1, :] * zbuf_ref[base + 6:base + 6 + seq, :]
                         + cw[1:2, :] * zbuf_ref[base + 7:base + 7 + seq, :] + cw[2:3, :] * zs)
            scs_ref[s] = zs[seq - 2:seq, :]
        _rows_to_tiles(h1_ref, back(h, hb, o_ref, jnp.concatenate(convs, axis=0)))


def _const_spec(shape):
    nd = len(shape)
    return pl.BlockSpec(shape, lambda i, _nd=nd: (0,) * _nd, pipeline_mode=pl.Buffered(1))


def _mixer_weight_specs():
    return [
        _const_spec((1, D_MODEL)), _const_spec((1, D_MODEL)),
        _const_spec((D_MODEL, W_MAIN_COLS)),
        _const_spec((D_MODEL, RANK_PAD)), _const_spec((RANK_PAD, GLA_DK)), _const_spec((1, GLA_DK)),
        _const_spec((1, DV)), _const_spec((GLA_DV, D_MODEL)),
        _const_spec((3, D_MODEL)), _const_spec((D_MODEL, D_MODEL)),
        _const_spec((D_MODEL, D_MODEL)),
        _const_spec((1, D_MODEL)), _const_spec((1, D_MODEL)),
    ]


def _mixer(meta, x_prompt, x_sample, state_gla, state_conv, weights):
    n_streams = state_conv.shape[0]
    seq = x_sample.shape[0] // n_streams
    streams = TM // seq
    npt = x_prompt.shape[0] // TM
    nst = n_streams // streams
    ptile = lambda i: (jnp.clip(i - 1, 0, npt - 1), 0)
    stile = lambda i: jnp.clip(i - 1 - npt, 0, nst - 1)
    kern = functools.partial(_mixer_kernel, n_prompt_tiles=npt, seq=seq, streams=streams)
    return pl.pallas_call(
        kern,
        grid=(1 + npt + nst,),
        in_specs=[_const_spec((N_META, D_MODEL)),
                  pl.BlockSpec((TM, D_MODEL), ptile),
                  pl.BlockSpec((TM, D_MODEL), lambda i: (stile(i), 0), pipeline_mode=pl.Buffered(1)),
                  pl.BlockSpec((streams, HEADS, DK, DV), lambda i: (stile(i), 0, 0, 0),
                               pipeline_mode=pl.Buffered(1)),
                  pl.BlockSpec((streams, 2, D_MODEL), lambda i: (stile(i), 0, 0))] + _mixer_weight_specs(),
        out_specs=[pl.BlockSpec((TM * SUBLANES, LANES), lambda i: (jnp.maximum(i - 1, 0), 0)),
                   pl.BlockSpec((HEADS, DK, DV), lambda i: (0, 0, 0)),
                   pl.BlockSpec((2, D_MODEL), lambda i: (0, 0)),
                   pl.BlockSpec((streams, HEADS, DK, DV), lambda i: (stile(i), 0, 0, 0)),
                   pl.BlockSpec((streams, 2, D_MODEL), lambda i: (stile(i), 0, 0))],
        out_shape=[jax.ShapeDtypeStruct(((x_prompt.shape[0] + x_sample.shape[0]) * SUBLANES, LANES), F32),
                   jax.ShapeDtypeStruct((HEADS, DK, DV), F32),
                   jax.ShapeDtypeStruct((2, D_MODEL), F32),
                   jax.ShapeDtypeStruct(state_gla.shape, F32),
                   jax.ShapeDtypeStruct(state_conv.shape, F32)],
        scratch_shapes=[pltpu.VMEM((HEADS, DV, DK), F32),
                        pltpu.VMEM((max(TM + 8, streams * (seq + 8)), D_MODEL), F32),
                        pltpu.VMEM((TM, G0), F32),
                        pltpu.VMEM((TM, GLA_DK), F32),
                        pltpu.VMEM((TM, GLA_DV), F32)],
        compiler_params=pltpu.CompilerParams(dimension_semantics=("arbitrary",), vmem_limit_bytes=VMEM_LIMIT),
        name="mixer",
    )(meta, x_prompt, x_sample, state_gla, state_conv, *weights)


def _first_index_of_max(vals, ids, sentinel):
    m = jnp.max(vals, axis=0, keepdims=True)
    first = jnp.min(jnp.where(vals == m, ids, sentinel), axis=0, keepdims=True)
    return m, first


def _router_kernel(h_ref, wr_ref, bias_ref, idx_ref, w_ref, rank_ref, counts_ref):
    @pl.when(pl.program_id(0) == 0)
    def _init():
        counts_ref[...] = jnp.zeros_like(counts_ref)

    h = _tiles_to_rows(h_ref, TN_ROUTER)
    hh = h.astype(BF16)
    hl = (h - hh.astype(F32)).astype(BF16)
    wh, wl = wr_ref[0], wr_ref[1]
    dg = functools.partial(lax.dot_general, dimension_numbers=_DN_TB, preferred_element_type=F32)
    logits = dg(wh, hh) + dg(wl, hh) + dg(wh, hl)
    scores = _sigmoid(logits)
    biased = scores + bias_ref[...]
    n_tok = biased.shape[1]
    neg_inf = jnp.float32(-jnp.inf)

    eid = lax.broadcasted_iota(I32, (N_EXPERTS, n_tok), 0).astype(F32)
    lid = lax.broadcasted_iota(I32, (GROUP_SIZE, n_tok), 0).astype(F32)
    group_scores = []
    for g in range(N_GROUPS):
        blk = biased[g * GROUP_SIZE:(g + 1) * GROUP_SIZE]
        m1, first = _first_index_of_max(blk, lid, float(GROUP_SIZE))
        m2 = jnp.max(jnp.where(lid == first, neg_inf, blk), axis=0, keepdims=True)
        group_scores.append(m1 + m2)
    gsc = jnp.concatenate(group_scores, axis=0)

    gid = lax.broadcasted_iota(I32, (N_GROUPS, n_tok), 0).astype(F32)
    keep = jnp.zeros((N_GROUPS, n_tok), F32)
    cur = gsc
    for _ in range(TOPK_GROUPS):
        _, first = _first_index_of_max(cur, gid, float(N_GROUPS))
        sel = gid == first
        keep = jnp.where(sel, 1.0, keep)
        cur = jnp.where(sel, neg_inf, cur)

    masked = jnp.concatenate(
        [jnp.where(keep[g:g + 1] > 0.5, biased[g * GROUP_SIZE:(g + 1) * GROUP_SIZE], neg_inf)
         for g in range(N_GROUPS)], axis=0)
    idxs, wts, sels = [], [], []
    cur = masked
    for _ in range(TOP_K):
        _, first = _first_index_of_max(cur, eid, float(N_EXPERTS))
        sel = eid == first
        idxs.append(first)
        sels.append(sel)
        wts.append(jnp.sum(jnp.where(sel, scores, 0.0), axis=0, keepdims=True))
        cur = jnp.where(sel, neg_inf, cur)
    w = jnp.concatenate(wts, axis=0)
    idx_ref[...] = jnp.concatenate(idxs, axis=0).astype(I32)
    w_ref[...] = w / jnp.sum(w, axis=0, keepdims=True) * ROUTED_SCALE

    chosen = jnp.where(sels[0], 1.0, 0.0)
    for sel in sels[1:]:
        chosen = chosen + jnp.where(sel, 1.0, 0.0)
    t_src = lax.broadcasted_iota(I32, (n_tok, n_tok), 0)
    t_dst = lax.broadcasted_iota(I32, (n_tok, n_tok), 1)
    earlier = jnp.where(t_src < t_dst, 1.0, 0.0).astype(BF16)
    arrivals = counts_ref[...] + _dot(chosen.astype(BF16), earlier)
    rank_ref[...] = jnp.concatenate(
        [jnp.sum(jnp.where(sel, arrivals, 0.0), axis=0, keepdims=True) for sel in sels], axis=0).astype(I32)
    counts_ref[...] += jnp.sum(chosen, axis=1, keepdims=True)


def _router(h1_tiles, wr_split, bias_col):
    n = h1_tiles.shape[0] // SUBLANES
    tok = pl.BlockSpec((TOP_K, TN_ROUTER), lambda i: (0, i))
    return pl.pallas_call(
        _router_kernel,
        grid=(n // TN_ROUTER,),
        in_specs=[pl.BlockSpec((TN_ROUTER * SUBLANES, LANES), lambda i: (i, 0)),
                  pl.BlockSpec((2, N_EXPERTS, D_MODEL), lambda i: (0, 0, 0)),
                  pl.BlockSpec((N_EXPERTS, 1), lambda i: (0, 0))],
        out_specs=[tok, tok, tok, pl.BlockSpec((N_EXPERTS, 1), lambda i: (0, 0))],
        out_shape=[jax.ShapeDtypeStruct((TOP_K, n), I32), jax.ShapeDtypeStruct((TOP_K, n), F32),
                   jax.ShapeDtypeStruct((TOP_K, n), I32), jax.ShapeDtypeStruct((N_EXPERTS, 1), F32)],
        compiler_params=pltpu.CompilerParams(dimension_semantics=("arbitrary",), vmem_limit_bytes=VMEM_LIMIT),
        name="router",
    )(h1_tiles, wr_split, bias_col)


def _positions_kernel(idx_ref, rank_ref, seg_ref, pos_ref):
    n_tok = idx_ref.shape[1]
    eid = lax.broadcasted_iota(I32, (N_EXPERTS, n_tok), 0)
    seg = seg_ref[...]
    rows = []
    for k in range(TOP_K):
        onehot = eid == idx_ref[k:k + 1, :]
        rows.append(jnp.sum(jnp.where(onehot, seg, 0.0), axis=0, keepdims=True))
    pos_ref[...] = jnp.concatenate(rows, axis=0).astype(I32) + rank_ref[...]


def _positions(idx_t, rank_t, seg_start_col):
    n = idx_t.shape[1]
    tok = pl.BlockSpec((TOP_K, TN_ROUTER), lambda i: (0, i))
    return pl.pallas_call(
        _positions_kernel,
        grid=(n // TN_ROUTER,),
        in_specs=[tok, tok, pl.BlockSpec((N_EXPERTS, 1), lambda i: (0, 0))],
        out_specs=tok,
        out_shape=jax.ShapeDtypeStruct((TOP_K, n), I32),
        compiler_params=pltpu.CompilerParams(dimension_semantics=("arbitrary",)),
        name="positions",
    )(idx_t, rank_t, seg_start_col)


def _tile_copy(src, src_tile, dst, dst_tile, sem):
    return pltpu.make_async_copy(src.at[pl.ds(pl.multiple_of(src_tile * SUBLANES, SUBLANES), SUBLANES)],
                                 dst.at[pl.ds(pl.multiple_of(dst_tile * SUBLANES, SUBLANES), SUBLANES)], sem)


def _dispatch_kernel(pos_ref, h_ref, x_hbm, sem):
    def body(t, carry):
        for k in range(TOP_K):
            _tile_copy(h_ref, t, x_hbm, pos_ref[0, 0, k * TN_DISPATCH + t], sem.at[k]).start(priority=k % 2)
        return carry
    lax.fori_loop(0, TN_DISPATCH, body, 0, unroll=2)
    for k in range(TOP_K):
        pltpu.make_async_copy(h_ref, x_hbm.at[pl.ds(0, TN_DISPATCH * SUBLANES)], sem.at[k]).wait()


def _dispatch(pos_tiles, h1_tiles):
    n_t = pos_tiles.shape[0]
    n_assign = n_t * TN_DISPATCH * TOP_K
    return pl.pallas_call(
        _dispatch_kernel,
        grid=(n_t,),
        in_specs=[pl.BlockSpec((1, 1, TOP_K * TN_DISPATCH), lambda t: (t, 0, 0), memory_space=pltpu.SMEM),
                  pl.BlockSpec((TN_DISPATCH * SUBLANES, LANES), lambda t: (t, 0))],
        out_specs=pl.BlockSpec(memory_space=pl.ANY),
        out_shape=jax.ShapeDtypeStruct((n_assign * SUBLANES, LANES), F32),
        scratch_shapes=[pltpu.SemaphoreType.DMA((TOP_K,))],
        compiler_params=pltpu.CompilerParams(dimension_semantics=("arbitrary",), has_side_effects=True),
        name="dispatch",
    )(pos_tiles, h1_tiles)


def _expert_kernel(blk_ref, exp_ref, lo_ref, hi_ref, x_ref, wg_ref, wu_ref, wd_ref, y_ref,
                   acc, wgb, wub, wdb, cached):
    i = pl.program_id(0)
    n = pl.num_programs(0)
    blk, lo, hi = blk_ref[i], lo_ref[i], hi_ref[i]
    first = (i == 0) | (blk != blk_ref[jnp.maximum(i - 1, 0)])
    last = (i == n - 1) | (blk != blk_ref[jnp.minimum(i + 1, n - 1)])

    @pl.when(i == 0)
    def _no_cached_weights():
        cached[0] = -1

    @pl.when(first)
    def _clear():
        acc[...] = jnp.zeros_like(acc)

    @pl.when(hi > lo)
    def _work():
        e = exp_ref[i]

        @pl.when(e != cached[0])
        def _cast_weights():
            wgb[...] = wg_ref[0].astype(BF16)
            wub[...] = wu_ref[0].astype(BF16)
            wdb[...] = wd_ref[0].astype(BF16)
            cached[0] = e

        xb = _tiles_to_rows(x_ref, BM).astype(BF16)
        g = _dot(xb, wgb[...])
        u = _dot(xb, wub[...])
        res = _dot(((g * _sigmoid(g)) * u).astype(BF16), wdb[...])
        row = blk * BM + lax.broadcasted_iota(I32, (BM, 1), 0)
        acc[...] = jnp.where((row >= lo) & (row < hi), res, acc[...])

    @pl.when(last)
    def _emit():
        _rows_to_tiles(y_ref, acc[...])


def _experts(item_blk, item_exp, item_lo, item_hi, x_sorted, w_gate, w_up, w_down):
    rows = pl.BlockSpec((BM * SUBLANES, LANES), lambda i, blk, exp, lo, hi: (blk[i], 0))
    wspec = lambda shape: pl.BlockSpec(shape, lambda i, blk, exp, lo, hi: (exp[i], 0, 0))
    grid_spec = pltpu.PrefetchScalarGridSpec(
        num_scalar_prefetch=4,
        grid=(item_blk.shape[0],),
        in_specs=[rows, wspec((1, D_MODEL, EXPERT_HIDDEN)), wspec((1, D_MODEL, EXPERT_HIDDEN)),
                  wspec((1, EXPERT_HIDDEN, D_MODEL))],
        out_specs=rows,
        scratch_shapes=[pltpu.VMEM((BM, D_MODEL), F32),
                        pltpu.VMEM((D_MODEL, EXPERT_HIDDEN), BF16),
                        pltpu.VMEM((D_MODEL, EXPERT_HIDDEN), BF16),
                        pltpu.VMEM((EXPERT_HIDDEN, D_MODEL), BF16),
                        pltpu.SMEM((1,), I32)],
    )
    return pl.pallas_call(
        _expert_kernel,
        grid_spec=grid_spec,
        out_shape=jax.ShapeDtypeStruct(x_sorted.shape, F32),
        compiler_params=pltpu.CompilerParams(dimension_semantics=("arbitrary",), vmem_limit_bytes=VMEM_LIMIT),
        name="experts",
    )(item_blk, item_exp, item_lo, item_hi, x_sorted, w_gate, w_up, w_down)


def _combine_kernel(pos_cur_ref, pos_nxt_ref, y_hbm, h_ref, w_ref, wsg_ref, wsu_ref, wsd_ref, l2g_ref, l2b_ref,
                    outp_ref, outs_ref, ybuf, sem, *, n_prompt_tiles):
    t = pl.program_id(0)
    n_t = pl.num_programs(0)
    slot = t % 2

    def gather_start(pos_ref, dst, dst_sem):
        def body(r, carry):
            for k in range(TOP_K):
                j = k * TN_COMBINE + r
                _tile_copy(y_hbm, pos_ref[0, 0, j], dst, j, dst_sem).start(priority=k % 2)
            return carry
        lax.fori_loop(0, TN_COMBINE, body, 0, unroll=2)

    @pl.when(t == 0)
    def _first():
        gather_start(pos_cur_ref, ybuf.at[0], sem.at[0])

    @pl.when(t + 1 < n_t)
    def _prefetch():
        gather_start(pos_nxt_ref, ybuf.at[1 - slot], sem.at[1 - slot])

    h = _tiles_to_rows(h_ref, TN_COMBINE)
    hb = h.astype(BF16)
    g = _dot(hb, wsg_ref[...])
    u = _dot(hb, wsu_ref[...])
    shared = _dot(((g * _sigmoid(g)) * u).astype(BF16), wsd_ref[...])

    yslot = ybuf.at[slot]
    pltpu.make_async_copy(y_hbm.at[pl.ds(0, TOP_K * TN_COMBINE * SUBLANES)], yslot, sem.at[slot]).wait()
    w = w_ref[...]
    wk = [jnp.broadcast_to(w[:, k:k + 1], (TN_COMBINE, LANES)) for k in range(TOP_K)]
    chunks = []
    for s in range(SUBLANES):
        acc = wk[0] * yslot[pl.ds(s, TN_COMBINE, stride=SUBLANES), :]
        for k in range(1, TOP_K):
            acc = acc + wk[k] * yslot[pl.ds(k * TN_COMBINE * SUBLANES + s, TN_COMBINE, stride=SUBLANES), :]
        chunks.append(acc)
    routed = jnp.concatenate(chunks, axis=1)
    out = _layer_norm(ALPHA * h + (routed + shared), l2g_ref[...], l2b_ref[...])

    @pl.when(t < n_prompt_tiles)
    def _prompt_rows():
        outp_ref[...] = out

    @pl.when(t >= n_prompt_tiles)
    def _sample_rows():
        outs_ref[...] = out


def _combine(pos_tiles, y_sorted, h1_tiles, w_tok, wsg, wsu, wsd, l2g, l2b, n_prompt):
    n = h1_tiles.shape[0] // SUBLANES
    n_t = n // TN_COMBINE
    npt = n_prompt // TN_COMBINE
    n_idx = TOP_K * TN_COMBINE
    cspec = lambda shape: pl.BlockSpec(shape, lambda t: (0,) * len(shape))
    return pl.pallas_call(
        functools.partial(_combine_kernel, n_prompt_tiles=npt),
        grid=(n_t,),
        in_specs=[pl.BlockSpec((1, 1, n_idx), lambda t: (t, 0, 0), memory_space=pltpu.SMEM),
                  pl.BlockSpec((1, 1, n_idx), lambda t: (jnp.minimum(t + 1, n_t - 1), 0, 0),
                               memory_space=pltpu.SMEM),
                  pl.BlockSpec(memory_space=pl.ANY),
                  pl.BlockSpec((TN_COMBINE * SUBLANES, LANES), lambda t: (t, 0)),
                  pl.BlockSpec((TN_COMBINE, TOP_K), lambda t: (t, 0)),
                  cspec((D_MODEL, SHARED_HIDDEN)), cspec((D_MODEL, SHARED_HIDDEN)), cspec((SHARED_HIDDEN, D_MODEL)),
                  cspec((1, D_MODEL)), cspec((1, D_MODEL))],
        out_specs=[pl.BlockSpec((TN_COMBINE, D_MODEL), lambda t: (jnp.minimum(t, npt - 1), 0)),
                   pl.BlockSpec((TN_COMBINE, D_MODEL), lambda t: (jnp.maximum(t - npt, 0), 0))],
        out_shape=[jax.ShapeDtypeStruct((n_prompt, D_MODEL), F32),
                   jax.ShapeDtypeStruct((n - n_prompt, D_MODEL), F32)],
        scratch_shapes=[pltpu.VMEM((2, n_idx * SUBLANES, LANES), F32), pltpu.SemaphoreType.DMA((2,))],
        compiler_params=pltpu.CompilerParams(dimension_semantics=("arbitrary",), vmem_limit_bytes=VMEM_LIMIT),
        name="combine",
    )(pos_tiles, pos_tiles, y_sorted, h1_tiles, w_tok, wsg, wsu, wsd, l2g, l2b)


def _work_items(seg_start, seg_end, n_assign):
    n_blocks = n_assign // BM
    block_start = jnp.arange(n_blocks, dtype=I32) * BM
    lo = jnp.sort(jnp.concatenate([block_start, seg_start]))
    hi = jnp.concatenate([lo[1:], jnp.full((1,), n_assign, I32)])
    blk = jnp.minimum(lo // BM, n_blocks - 1)
    exp = jnp.minimum(jnp.sum((seg_end[None, :] <= lo[:, None]).astype(I32), axis=1), N_EXPERTS - 1)
    return blk, exp, lo, hi


def _pack_mixer_weights(ln_emb_g, ln_emb_b, w_in, w_gate_a2, b_gate, gla_norm_g, w_gla_out, conv_w,
                        w_conv_out, w_o, ln1_g, ln1_b):
    q, k, v, g, a, cb, cc, ch, ga, gb = jnp.split(
        w_in, [512, 1024, 2048, 3072, 3088, 4112, 5136, 6160, 7184], axis=-1)
    w_main = jnp.concatenate([q, k, v, g, cb, cc, ch, ga, gb], axis=-1).astype(BF16)
    w_a = jnp.pad(a, ((0, 0), (0, RANK_PAD - GATE_RANK))).astype(BF16)
    w_a2 = jnp.pad(w_gate_a2, ((0, RANK_PAD - GATE_RANK), (0, 0))).astype(BF16)
    row = lambda x: x.reshape(1, -1).astype(F32)
    return [row(ln_emb_g), row(ln_emb_b), w_main, w_a, w_a2, row(b_gate), row(gla_norm_g),
            w_gla_out.astype(BF16), conv_w.astype(F32), w_conv_out.astype(BF16), w_o.astype(BF16),
            row(ln1_g), row(ln1_b)]


def kernel(x_prompt, x_sample, state_gla, state_conv, meta_tokens, ln_emb_g, ln_emb_b, w_in, w_gate_a2, b_gate, gla_norm_g, w_gla_out, conv_w, w_conv_out, w_o, ln1_g, ln1_b, w_router, router_bias, w_exp_gate, w_exp_up, w_exp_down, w_sh_gate, w_sh_up, w_sh_down, ln2_g, ln2_b):
    batch, seq, _ = x_prompt.shape
    dec_batch, dec_seq, _ = x_sample.shape
    depth = w_in.shape[0]
    assert batch == 1 and depth == 1 and seq % TM == 0 and TM % dec_seq == 0 and dec_seq % SUB == 0
    n_prompt, n_sample = batch * seq, dec_batch * dec_seq
    n_total = n_prompt + n_sample
    assert n_sample % TM == 0 and n_total % TN_ROUTER == 0
    assert n_prompt % TN_COMBINE == 0 and n_sample % TN_COMBINE == 0 and n_total % TN_DISPATCH == 0
    n_assign = n_total * TOP_K
    assert n_assign % BM == 0

    weights = _pack_mixer_weights(ln_emb_g, ln_emb_b, w_in[0], w_gate_a2[0], b_gate[0], gla_norm_g[0],
                                  w_gla_out[0], conv_w[0], w_conv_out[0], w_o[0], ln1_g[0], ln1_b[0])
    h1, sg_p, sc_p, sg_s, sc_s = _mixer(meta_tokens.astype(F32), x_prompt.reshape(n_prompt, D_MODEL),
                                        x_sample.reshape(n_sample, D_MODEL), state_gla[0], state_conv[0], weights)

    wr_t = w_router[0].T
    wr_hi = wr_t.astype(BF16)
    wr_split = jnp.stack([wr_hi, (wr_t - wr_hi.astype(F32)).astype(BF16)])
    idx_t, w_t, rank_t, counts = _router(h1, wr_split, router_bias[0].reshape(N_EXPERTS, 1).astype(F32))

    counts = counts.reshape(N_EXPERTS).astype(I32)
    seg_end = jnp.cumsum(counts).astype(I32)
    seg_start = seg_end - counts
    pos_t = _positions(idx_t, rank_t, seg_start.astype(F32).reshape(N_EXPERTS, 1))
    pos_tiles = lambda tn: pos_t.reshape(TOP_K, n_total // tn, tn).transpose(1, 0, 2).reshape(n_total // tn, 1, -1)

    x_sorted = _dispatch(pos_tiles(TN_DISPATCH), h1)
    y_sorted = _experts(*_work_items(seg_start, seg_end, n_assign), x_sorted,
                        w_exp_gate[0], w_exp_up[0], w_exp_down[0])
    y_prompt, y_sample = _combine(pos_tiles(TN_COMBINE), y_sorted, h1, w_t.T,
                                  w_sh_gate[0].astype(BF16), w_sh_up[0].astype(BF16), w_sh_down[0].astype(BF16),
                                  ln2_g[0].reshape(1, -1), ln2_b[0].reshape(1, -1), n_prompt)
    y_prompt = y_prompt.reshape(batch, seq, D_MODEL)
    y_sample = y_sample.reshape(dec_batch, dec_seq, D_MODEL)
    return (y_prompt, y_sample,
            sg_p.reshape(depth, batch, HEADS, DK, DV), sc_p.reshape(depth, batch, 2, D_MODEL),
            sg_s.reshape(depth, dec_batch, HEADS, DK, DV), sc_s.reshape(depth, dec_batch, 2, D_MODEL))
```

```python
import functools

import jax
import jax.numpy as jnp
from jax import lax
from jax.experimental import pallas as pl
from jax.experimental.pallas import tpu as pltpu

F32 = jnp.float32
BF16 = jnp.bfloat16
I32 = jnp.int32

D_MODEL = 1024
N_META = 16
HEADS = 4
DK = 128
DV = 256
GLA_DK = HEADS * DK
GLA_DV = HEADS * DV
GATE_RANK = 16
GATE_TAU = 16.0
N_EXPERTS = 256
TOP_K = 8
N_GROUPS = 8
GROUP_SIZE = N_EXPERTS // N_GROUPS
TOPK_GROUPS = 4
EXPERT_HIDDEN = 256
SHARED_HIDDEN = 256
ROUTED_SCALE = 2.5
LN_EPS = 1e-5
RMS_EPS = 1e-6
ALPHA = 2.0 ** 0.25
QSCALE = DK ** -0.5

Q0, K0, V0, G0, CB0, CC0, CH0, GA0, GB0 = 0, 512, 1024, 2048, 3072, 4096, 5120, 6144, 7168
W_MAIN_COLS = 8192
RANK_PAD = 128

SUB = 16
PROMPT_CHUNK = 64
TM = 256
TN_ROUTER = 512
TN_DISPATCH = 256
BM = 256
TN_COMBINE = 256
VMEM_LIMIT = 60 * 1024 * 1024
SUBLANES = 8
LANES = 128
assert D_MODEL == SUBLANES * LANES

_DN_TB = (((1,), (1,)), ((), ()))
_DN_TA = (((0,), (0,)), ((), ()))


def _dot(a, b):
    return jnp.dot(a, b, preferred_element_type=F32)


def _tiles_to_rows(ref, n, base=0):
    return jnp.concatenate(
        [ref[pl.ds(SUBLANES * base + s, n, stride=SUBLANES), :] for s in range(SUBLANES)], axis=1)


def _rows_to_tiles(ref, val):
    n = val.shape[0]
    for s in range(SUBLANES):
        ref[pl.ds(s, n, stride=SUBLANES), :] = val[:, s * LANES:(s + 1) * LANES]


def _layer_norm(x, g, b):
    mu = jnp.mean(x, axis=-1, keepdims=True)
    xc = x - mu
    var = jnp.mean(xc * xc, axis=-1, keepdims=True)
    return xc * lax.rsqrt(var + LN_EPS) * g + b


def _sigmoid(x):
    return 1.0 / (1.0 + jnp.exp(-x))


def _log_sigmoid(x):
    return jnp.minimum(x, 0.0) - jnp.log1p(jnp.exp(-jnp.abs(x)))


def _chunk_cumsum(g, chunk):
    t = g.shape[0]
    shift = chunk.bit_length() - 1
    r = lax.broadcasted_iota(I32, (t, t), 0)
    c = lax.broadcasted_iota(I32, (t, t), 1)
    tri = jnp.where(((r >> shift) == (c >> shift)) & (c <= r), 1.0, 0.0).astype(BF16)
    g1 = g.astype(BF16)
    r1 = g - g1.astype(F32)
    g2 = r1.astype(BF16)
    g3 = (r1 - g2.astype(F32)).astype(BF16)
    return _dot(tri, g1) + _dot(tri, g2) + _dot(tri, g3)


def _gate_log_decay(hb, wa_ref, wa2_ref, bg_ref):
    a = _dot(hb, wa_ref[...])
    return _log_sigmoid(_dot(a.astype(BF16), wa2_ref[...]) + bg_ref[...]) * (1.0 / GATE_TAU)


def _gla_state_update(k, v, b, st):
    n = k.shape[0]
    b_last = b[n - 1:n, :]
    khat = (k * jnp.exp(b_last - b)).astype(BF16)
    return st * jnp.exp(b_last) + lax.dot_general(v.astype(BF16), khat, _DN_TA, preferred_element_type=F32)


def _gla_chunk(q, k, v, b, st):
    n = q.shape[0]
    vb = v.astype(BF16)
    q0 = (q * (jnp.exp(b) * QSCALE)).astype(BF16)
    o = lax.dot_general(q0, st.astype(BF16), _DN_TB, preferred_element_type=F32)
    col = lax.broadcasted_iota(I32, (SUB, n), 1)
    row = lax.broadcasted_iota(I32, (SUB, n), 0)
    blocks = []
    for i in range(n // SUB):
        lo, hi = SUB * i, SUB * (i + 1)
        bref = b[lo:lo + 1, :]
        qi = (q[lo:hi] * (jnp.exp(b[lo:hi] - bref) * QSCALE)).astype(BF16)
        kk = (k[:hi] * jnp.exp(bref - b[:hi])).astype(BF16)
        if hi < n:
            kk = jnp.concatenate([kk, jnp.zeros((n - hi, DK), BF16)], axis=0)
        a = lax.dot_general(qi, kk, _DN_TB, preferred_element_type=F32)
        blocks.append(jnp.where(col <= row + lo, a, 0.0))
    scores = blocks[0] if len(blocks) == 1 else jnp.concatenate(blocks, axis=0)
    o = o + _dot(scores.astype(BF16), vb)
    return o, _gla_state_update(k, v, b, st)


def _mixer_back(h, hb, o_ref, conv, wm_ref, ng_ref, wgo_ref, wco_ref, wo_ref, l1g_ref, l1b_ref):
    g = _dot(hb, wm_ref[:, G0:G0 + GLA_DV])
    parts = []
    for hd in range(HEADS):
        oh = o_ref[:, hd * DV:(hd + 1) * DV]
        ms = jnp.mean(oh * oh, axis=-1, keepdims=True)
        parts.append(oh * lax.rsqrt(ms + RMS_EPS) * ng_ref[...])
    on = jnp.concatenate(parts, axis=1) * (g * _sigmoid(g))
    branch_a = _dot(on.astype(BF16), wgo_ref[...])
    mix = _sigmoid(_dot(hb, wm_ref[:, GA0:GA0 + D_MODEL])) * branch_a
    yc = _dot(hb, wm_ref[:, CB0:CB0 + D_MODEL]) * conv
    branch_b = _dot(yc.astype(BF16), wco_ref[...])
    mix = mix + _sigmoid(_dot(hb, wm_ref[:, GB0:GB0 + D_MODEL])) * branch_b
    mixed = _dot(mix.astype(BF16), wo_ref[...])
    return _layer_norm(ALPHA * h + mixed, l1g_ref[...], l1b_ref[...])


def _conv_input(hb, wm_ref):
    return _dot(hb, wm_ref[:, CC0:CC0 + D_MODEL]) * _dot(hb, wm_ref[:, CH0:CH0 + D_MODEL])


def _gla_rows(qkv_ref, b_ref, rows, hd, st):
    kc = slice(hd * DK, (hd + 1) * DK)
    return _gla_chunk(qkv_ref[rows, Q0 + hd * DK:Q0 + (hd + 1) * DK],
                      qkv_ref[rows, K0 + hd * DK:K0 + (hd + 1) * DK],
                      qkv_ref[rows, V0 + hd * DV:V0 + (hd + 1) * DV],
                      b_ref[rows, kc], st)


def _mixer_kernel(meta_ref, xp_ref, xs_ref, sgin_ref, scin_ref,
                  lng_ref, lnb_ref, wm_ref, wa_ref, wa2_ref, bg_ref,
                  ng_ref, wgo_ref, cw_ref, wco_ref, wo_ref, l1g_ref, l1b_ref,
                  h1_ref, sgp_ref, scp_ref, sgs_ref, scs_ref,
                  st_ref, zbuf_ref, qkv_ref, b_ref, o_ref, *, n_prompt_tiles, seq, streams):
    i = pl.program_id(0)
    back = functools.partial(_mixer_back, wm_ref=wm_ref, ng_ref=ng_ref, wgo_ref=wgo_ref, wco_ref=wco_ref,
                             wo_ref=wo_ref, l1g_ref=l1g_ref, l1b_ref=l1b_ref)

    def front(x):
        h = _layer_norm(x, lng_ref[...], lnb_ref[...])
        hb = h.astype(BF16)
        return h, hb, _gate_log_decay(hb, wa_ref, wa2_ref, bg_ref)

    @pl.when(i == 0)
    def _meta():
        _, hb, gk = front(meta_ref[...])
        k = _dot(hb, wm_ref[:, K0:K0 + GLA_DK])
        v = _dot(hb, wm_ref[:, V0:V0 + GLA_DV])
        b = _chunk_cumsum(gk, N_META)
        zero_state = jnp.zeros((DV, DK), F32)
        for hd in range(HEADS):
            kc = slice(hd * DK, (hd + 1) * DK)
            st_ref[hd] = _gla_state_update(k[:, kc], v[:, hd * DV:(hd + 1) * DV], b[:, kc], zero_state)
        zbuf_ref[0:8, :] = _conv_input(hb, wm_ref)[N_META - 8:N_META, :]

    @pl.when((i >= 1) & (i <= n_prompt_tiles))
    def _prompt_tile():
        h, hb, gk = front(xp_ref[...])
        qkv_ref[...] = _dot(hb, wm_ref[:, Q0:G0])
        b_ref[...] = _chunk_cumsum(gk, PROMPT_CHUNK)
        for c in range(TM // PROMPT_CHUNK):
            rows = slice(c * PROMPT_CHUNK, (c + 1) * PROMPT_CHUNK)
            for hd in range(HEADS):
                o, st_new = _gla_rows(qkv_ref, b_ref, rows, hd, st_ref[hd])
                st_ref[hd] = st_new
                o_ref[rows, hd * DV:(hd + 1) * DV] = o
        z = _conv_input(hb, wm_ref)
        zbuf_ref[8:8 + TM, :] = z
        cw = cw_ref[...]
        conv = cw[0:1, :] * zbuf_ref[6:6 + TM, :] + cw[1:2, :] * zbuf_ref[7:7 + TM, :] + cw[2:3, :] * z
        zbuf_ref[0:8, :] = z[TM - 8:TM, :]
        _rows_to_tiles(h1_ref, back(h, hb, o_ref, conv))

        @pl.when(i == n_prompt_tiles)
        def _prompt_states():
            for hd in range(HEADS):
                sgp_ref[hd] = st_ref[hd].T
            scp_ref[...] = z[TM - 2:TM, :]

    @pl.when(i > n_prompt_tiles)
    def _sample_tile():
        h, hb, gk = front(xs_ref[...])
        qkv_ref[...] = _dot(hb, wm_ref[:, Q0:G0])
        b_ref[...] = _chunk_cumsum(gk, seq)
        z = _conv_input(hb, wm_ref)
        pitch = seq + 8
        cw = cw_ref[...]
        convs = []
        for s in range(streams):
            rows = slice(s * seq, (s + 1) * seq)
            for hd in range(HEADS):
                o, st_new = _gla_rows(qkv_ref, b_ref, rows, hd, sgin_ref[s, hd].T)
                sgs_ref[s, hd] = st_new.T
                o_ref[rows, hd * DV:(hd + 1) * DV] = o
            zs = z[rows, :]
            base = s * pitch
            zbuf_ref[base + 6:base + 8, :] = scin_ref[s]
            zbuf_ref[base + 8:base + 8 + seq, :] = zs
            convs.append(cw[0:1, :] * zbuf_ref[base + 6:base + 6 + seq, :]
                         + cw[1:2, :] * zbuf_ref[base + 7:base + 7 + seq, :] + cw[2:3, :] * zs)
            scs_ref[s] = zs[seq - 2:seq, :]
        _rows_to_tiles(h1_ref, back(h, hb, o_ref, jnp.concatenate(convs, axis=0)))


def _const_spec(shape):
    nd = len(shape)
    return pl.BlockSpec(shape, lambda i, _nd=nd: (0,) * _nd, pipeline_mode=pl.Buffered(1))


def _mixer_weight_specs():
    return [
        _const_spec((1, D_MODEL)), _const_spec((1, D_MODEL)),
        _const_spec((D_MODEL, W_MAIN_COLS)),
        _const_spec((D_MODEL, RANK_PAD)), _const_spec((RANK_PAD, GLA_DK)), _const_spec((1, GLA_DK)),
        _const_spec((1, DV)), _const_spec((GLA_DV, D_MODEL)),
        _const_spec((3, D_MODEL)), _const_spec((D_MODEL, D_MODEL)),
        _const_spec((D_MODEL, D_MODEL)),
        _const_spec((1, D_MODEL)), _const_spec((1, D_MODEL)),
    ]


def _mixer(meta, x_prompt, x_sample, state_gla, state_conv, weights):
    n_streams = state_conv.shape[0]
    seq = x_sample.shape[0] // n_streams
    streams = TM // seq
    npt = x_prompt.shape[0] // TM
    nst = n_streams // streams
    ptile = lambda i: (jnp.clip(i - 1, 0, npt - 1), 0)
    stile = lambda i: jnp.clip(i - 1 - npt, 0, nst - 1)
    kern = functools.partial(_mixer_kernel, n_prompt_tiles=npt, seq=seq, streams=streams)
    return pl.pallas_call(
        kern,
        grid=(1 + npt + nst,),
        in_specs=[_const_spec((N_META, D_MODEL)),
                  pl.BlockSpec((TM, D_MODEL), ptile),
                  pl.BlockSpec((TM, D_MODEL), lambda i: (stile(i), 0), pipeline_mode=pl.Buffered(1)),
                  pl.BlockSpec((streams, HEADS, DK, DV), lambda i: (stile(i), 0, 0, 0),
                               pipeline_mode=pl.Buffered(1)),
                  pl.BlockSpec((streams, 2, D_MODEL), lambda i: (stile(i), 0, 0))] + _mixer_weight_specs(),
        out_specs=[pl.BlockSpec((TM * SUBLANES, LANES), lambda i: (jnp.maximum(i - 1, 0), 0)),
                   pl.BlockSpec((HEADS, DK, DV), lambda i: (0, 0, 0)),
                   pl.BlockSpec((2, D_MODEL), lambda i: (0, 0)),
                   pl.BlockSpec((streams, HEADS, DK, DV), lambda i: (stile(i), 0, 0, 0)),
                   pl.BlockSpec((streams, 2, D_MODEL), lambda i: (stile(i), 0, 0))],
        out_shape=[jax.ShapeDtypeStruct(((x_prompt.shape[0] + x_sample.shape[0]) * SUBLANES, LANES), F32),
                   jax.ShapeDtypeStruct((HEADS, DK, DV), F32),
                   jax.ShapeDtypeStruct((2, D_MODEL), F32),
                   jax.ShapeDtypeStruct(state_gla.shape, F32),
                   jax.ShapeDtypeStruct(state_conv.shape, F32)],
        scratch_shapes=[pltpu.VMEM((HEADS, DV, DK), F32),
                        pltpu.VMEM((max(TM + 8, streams * (seq + 8)), D_MODEL), F32),
                        pltpu.VMEM((TM, G0), F32),
                        pltpu.VMEM((TM, GLA_DK), F32),
                        pltpu.VMEM((TM, GLA_DV), F32)],
        compiler_params=pltpu.CompilerParams(dimension_semantics=("arbitrary",), vmem_limit_bytes=VMEM_LIMIT),
        name="mixer",
    )(meta, x_prompt, x_sample, state_gla, state_conv, *weights)


def _first_index_of_max(vals, ids, sentinel):
    m = jnp.max(vals, axis=0, keepdims=True)
    first = jnp.min(jnp.where(vals == m, ids, sentinel), axis=0, keepdims=True)
    return m, first


def _router_kernel(h_ref, wr_ref, bias_ref, idx_ref, w_ref, rank_ref, counts_ref):
    @pl.when(pl.program_id(0) == 0)
    def _init():
        counts_ref[...] = jnp.zeros_like(counts_ref)

    h = _tiles_to_rows(h_ref, TN_ROUTER)
    hh = h.astype(BF16)
    hl = (h - hh.astype(F32)).astype(BF16)
    wh, wl = wr_ref[0], wr_ref[1]
    dg = functools.partial(lax.dot_general, dimension_numbers=_DN_TB, preferred_element_type=F32)
    logits = dg(wh, hh) + dg(wl, hh) + dg(wh, hl)
    scores = _sigmoid(logits)
    biased = scores + bias_ref[...]
    n_tok = biased.shape[1]
    neg_inf = jnp.float32(-jnp.inf)

    eid = lax.broadcasted_iota(I32, (N_EXPERTS, n_tok), 0).astype(F32)
    lid = lax.broadcasted_iota(I32, (GROUP_SIZE, n_tok), 0).astype(F32)
    group_scores = []
    for g in range(N_GROUPS):
        blk = biased[g * GROUP_SIZE:(g + 1) * GROUP_SIZE]
        m1, first = _first_index_of_max(blk, lid, float(GROUP_SIZE))
        m2 = jnp.max(jnp.where(lid == first, neg_inf, blk), axis=0, keepdims=True)
        group_scores.append(m1 + m2)
    gsc = jnp.concatenate(group_scores, axis=0)

    gid = lax.broadcasted_iota(I32, (N_GROUPS, n_tok), 0).astype(F32)
    keep = jnp.zeros((N_GROUPS, n_tok), F32)
    cur = gsc
    for _ in range(TOPK_GROUPS):
        _, first = _first_index_of_max(cur, gid, float(N_GROUPS))
        sel = gid == first
        keep = jnp.where(sel, 1.0, keep)
        cur = jnp.where(sel, neg_inf, cur)

    masked = jnp.concatenate(
        [jnp.where(keep[g:g + 1] > 0.5, biased[g * GROUP_SIZE:(g + 1) * GROUP_SIZE], neg_inf)
         for g in range(N_GROUPS)], axis=0)
    idxs, wts, sels = [], [], []
    cur = masked
    for _ in range(TOP_K):
        _, first = _first_index_of_max(cur, eid, float(N_EXPERTS))
        sel = eid == first
        idxs.append(first)
        sels.append(sel)
        wts.append(jnp.sum(jnp.where(sel, scores, 0.0), axis=0, keepdims=True))
        cur = jnp.where(sel, neg_inf, cur)
    w = jnp.concatenate(wts, axis=0)
    idx_ref[...] = jnp.concatenate(idxs, axis=0).astype(I32)
    w_ref[...] = w / jnp.sum(w, axis=0, keepdims=True) * ROUTED_SCALE

    chosen = jnp.where(sels[0], 1.0, 0.0)
    for sel in sels[1:]:
        chosen = chosen + jnp.where(sel, 1.0, 0.0)
    t_src = lax.broadcasted_iota(I32, (n_tok, n_tok), 0)
    t_dst = lax.broadcasted_iota(I32, (n_tok, n_tok), 1)
    earlier = jnp.where(t_src < t_dst, 1.0, 0.0).astype(BF16)
    arrivals = counts_ref[...] + _dot(chosen.astype(BF16), earlier)
    rank_ref[...] = jnp.concatenate(
        [jnp.sum(jnp.where(sel, arrivals, 0.0), axis=0, keepdims=True) for sel in sels], axis=0).astype(I32)
    counts_ref[...] += jnp.sum(chosen, axis=1, keepdims=True)


def _router(h1_tiles, wr_split, bias_col):
    n = h1_tiles.shape[0] // SUBLANES
    tok = pl.BlockSpec((TOP_K, TN_ROUTER), lambda i: (0, i))
    return pl.pallas_call(
        _router_kernel,
        grid=(n // TN_ROUTER,),
        in_specs=[pl.BlockSpec((TN_ROUTER * SUBLANES, LANES), lambda i: (i, 0)),
                  pl.BlockSpec((2, N_EXPERTS, D_MODEL), lambda i: (0, 0, 0)),
                  pl.BlockSpec((N_EXPERTS, 1), lambda i: (0, 0))],
        out_specs=[tok, tok, tok, pl.BlockSpec((N_EXPERTS, 1), lambda i: (0, 0))],
        out_shape=[jax.ShapeDtypeStruct((TOP_K, n), I32), jax.ShapeDtypeStruct((TOP_K, n), F32),
                   jax.ShapeDtypeStruct((TOP_K, n), I32), jax.ShapeDtypeStruct((N_EXPERTS, 1), F32)],
        compiler_params=pltpu.CompilerParams(dimension_semantics=("arbitrary",), vmem_limit_bytes=VMEM_LIMIT),
        name="router",
    )(h1_tiles, wr_split, bias_col)


def _positions_kernel(idx_ref, rank_ref, seg_ref, pos_ref):
    n_tok = idx_ref.shape[1]
    eid = lax.broadcasted_iota(I32, (N_EXPERTS, n_tok), 0)
    seg = seg_ref[...]
    rows = []
    for k in range(TOP_K):
        onehot = eid == idx_ref[k:k + 1, :]
        rows.append(jnp.sum(jnp.where(onehot, seg, 0.0), axis=0, keepdims=True))
    pos_ref[...] = jnp.concatenate(rows, axis=0).astype(I32) + rank_ref[...]


def _positions(idx_t, rank_t, seg_start_col):
    n = idx_t.shape[1]
    tok = pl.BlockSpec((TOP_K, TN_ROUTER), lambda i: (0, i))
    return pl.pallas_call(
        _positions_kernel,
        grid=(n // TN_ROUTER,),
        in_specs=[tok, tok, pl.BlockSpec((N_EXPERTS, 1), lambda i: (0, 0))],
        out_specs=tok,
        out_shape=jax.ShapeDtypeStruct((TOP_K, n), I32),
        compiler_params=pltpu.CompilerParams(dimension_semantics=("arbitrary",)),
        name="positions",
    )(idx_t, rank_t, seg_start_col)


def _tile_copy(src, src_tile, dst, dst_tile, sem):
    return pltpu.make_async_copy(src.at[pl.ds(pl.multiple_of(src_tile * SUBLANES, SUBLANES), SUBLANES)],
                                 dst.at[pl.ds(pl.multiple_of(dst_tile * SUBLANES, SUBLANES), SUBLANES)], sem)


def _dispatch_kernel(pos_ref, h_ref, x_hbm, sem):
    def body(t, carry):
        for k in range(TOP_K):
            _tile_copy(h_ref, t, x_hbm, pos_ref[0, 0, k * TN_DISPATCH + t], sem.at[k]).start(priority=k % 2)
        return carry
    lax.fori_loop(0, TN_DISPATCH, body, 0, unroll=2)
    for k in range(TOP_K):
        pltpu.make_async_copy(h_ref, x_hbm.at[pl.ds(0, TN_DISPATCH * SUBLANES)], sem.at[k]).wait()


def _dispatch(pos_tiles, h1_tiles):
    n_t = pos_tiles.shape[0]
    n_assign = n_t * TN_DISPATCH * TOP_K
    return pl.pallas_call(
        _dispatch_kernel,
        grid=(n_t,),
        in_specs=[pl.BlockSpec((1, 1, TOP_K * TN_DISPATCH), lambda t: (t, 0, 0), memory_space=pltpu.SMEM),
                  pl.BlockSpec((TN_DISPATCH * SUBLANES, LANES), lambda t: (t, 0))],
        out_specs=pl.BlockSpec(memory_space=pl.ANY),
        out_shape=jax.ShapeDtypeStruct((n_assign * SUBLANES, LANES), F32),
        scratch_shapes=[pltpu.SemaphoreType.DMA((TOP_K,))],
        compiler_params=pltpu.CompilerParams(dimension_semantics=("arbitrary",), has_side_effects=True),
        name="dispatch",
    )(pos_tiles, h1_tiles)


def _expert_kernel(blk_ref, exp_ref, lo_ref, hi_ref, seq_ref, nxt_ref, x_ref, wg_hbm, wu_hbm, wd_hbm, y_ref,
                   acc, wg32, wu32, wd32, wgb, wub, wdb, sem):
    i = pl.program_id(0)
    n = pl.num_programs(0)
    blk, lo, hi = blk_ref[i], lo_ref[i], hi_ref[i]
    first = (i == 0) | (blk != blk_ref[jnp.maximum(i - 1, 0)])
    last = (i == n - 1) | (blk != blk_ref[jnp.minimum(i + 1, n - 1)])

    def weight_copies(e, slot):
        return (pltpu.make_async_copy(wg_hbm.at[e], wg32.at[slot], sem.at[slot]),
                pltpu.make_async_copy(wu_hbm.at[e], wu32.at[slot], sem.at[slot]),
                pltpu.make_async_copy(wd_hbm.at[e], wd32.at[slot], sem.at[slot]))

    @pl.when(first)
    def _clear():
        acc[...] = jnp.zeros_like(acc)

    @pl.when(hi > lo)
    def _work():
        seq = seq_ref[i]

        @pl.when(seq >= 0)
        def _new_expert():
            slot = seq % 2

            @pl.when(seq == 0)
            def _fetch_own():
                for c in weight_copies(exp_ref[i], slot):
                    c.start()

            @pl.when(nxt_ref[i] >= 0)
            def _prefetch_next():
                for c in weight_copies(nxt_ref[i], 1 - slot):
                    c.start()

            for c in weight_copies(exp_ref[i], slot):
                c.wait()
            wgb[...] = wg32[slot].astype(BF16)
            wub[...] = wu32[slot].astype(BF16)
            wdb[...] = wd32[slot].astype(BF16)

        xb = _tiles_to_rows(x_ref, BM).astype(BF16)
        g = _dot(xb, wgb[...])
        u = _dot(xb, wub[...])
        res = _dot(((g * _sigmoid(g)) * u).astype(BF16), wdb[...])
        row = blk * BM + lax.broadcasted_iota(I32, (BM, 1), 0)
        acc[...] = jnp.where((row >= lo) & (row < hi), res, acc[...])

    @pl.when(last)
    def _emit():
        _rows_to_tiles(y_ref, acc[...])


def _experts(items, x_sorted, w_gate, w_up, w_down):
    rows = pl.BlockSpec((BM * SUBLANES, LANES), lambda i, blk, *_: (blk[i], 0))
    hbm = pl.BlockSpec(memory_space=pl.ANY)
    grid_spec = pltpu.PrefetchScalarGridSpec(
        num_scalar_prefetch=len(items),
        grid=(items[0].shape[0],),
        in_specs=[rows, hbm, hbm, hbm],
        out_specs=rows,
        scratch_shapes=[pltpu.VMEM((BM, D_MODEL), F32),
                        pltpu.VMEM((2, D_MODEL, EXPERT_HIDDEN), F32),
                        pltpu.VMEM((2, D_MODEL, EXPERT_HIDDEN), F32),
                        pltpu.VMEM((2, EXPERT_HIDDEN, D_MODEL), F32),
                        pltpu.VMEM((D_MODEL, EXPERT_HIDDEN), BF16),
                        pltpu.VMEM((D_MODEL, EXPERT_HIDDEN), BF16),
                        pltpu.VMEM((EXPERT_HIDDEN, D_MODEL), BF16),
                        pltpu.SemaphoreType.DMA((2,))],
    )
    return pl.pallas_call(
        _expert_kernel,
        grid_spec=grid_spec,
        out_shape=jax.ShapeDtypeStruct(x_sorted.shape, F32),
        compiler_params=pltpu.CompilerParams(dimension_semantics=("arbitrary",), vmem_limit_bytes=VMEM_LIMIT),
        name="experts",
    )(*items, x_sorted, w_gate, w_up, w_down)


def _combine_kernel(pos_cur_ref, pos_nxt_ref, y_hbm, h_ref, w_ref, wsg_ref, wsu_ref, wsd_ref, l2g_ref, l2b_ref,
                    outp_ref, outs_ref, ybuf, sem, *, n_prompt_tiles):
    t = pl.program_id(0)
    n_t = pl.num_programs(0)
    slot = t % 2

    def gather_start(pos_ref, dst, dst_sem):
        def body(r, carry):
            for k in range(TOP_K):
                j = k * TN_COMBINE + r
                _tile_copy(y_hbm, pos_ref[0, 0, j], dst, j, dst_sem).start(priority=k % 2)
            return carry
        lax.fori_loop(0, TN_COMBINE, body, 0, unroll=2)

    @pl.when(t == 0)
    def _first():
        gather_start(pos_cur_ref, ybuf.at[0], sem.at[0])

    @pl.when(t + 1 < n_t)
    def _prefetch():
        gather_start(pos_nxt_ref, ybuf.at[1 - slot], sem.at[1 - slot])

    h = _tiles_to_rows(h_ref, TN_COMBINE)
    hb = h.astype(BF16)
    g = _dot(hb, wsg_ref[...])
    u = _dot(hb, wsu_ref[...])
    shared = _dot(((g * _sigmoid(g)) * u).astype(BF16), wsd_ref[...])

    yslot = ybuf.at[slot]
    pltpu.make_async_copy(y_hbm.at[pl.ds(0, TOP_K * TN_COMBINE * SUBLANES)], yslot, sem.at[slot]).wait()
    w = w_ref[...]
    wk = [jnp.broadcast_to(w[:, k:k + 1], (TN_COMBINE, LANES)) for k in range(TOP_K)]
    chunks = []
    for s in range(SUBLANES):
        acc = wk[0] * yslot[pl.ds(s, TN_COMBINE, stride=SUBLANES), :]
        for k in range(1, TOP_K):
            acc = acc + wk[k] * yslot[pl.ds(k * TN_COMBINE * SUBLANES + s, TN_COMBINE, stride=SUBLANES), :]
        chunks.append(acc)
    routed = jnp.concatenate(chunks, axis=1)
    out = _layer_norm(ALPHA * h + (routed + shared), l2g_ref[...], l2b_ref[...])

    @pl.when(t < n_prompt_tiles)
    def _prompt_rows():
        outp_ref[...] = out

    @pl.when(t >= n_prompt_tiles)
    def _sample_rows():
        outs_ref[...] = out


def _combine(pos_tiles, y_sorted, h1_tiles, w_tok, wsg, wsu, wsd, l2g, l2b, n_prompt):
    n = h1_tiles.shape[0] // SUBLANES
    n_t = n // TN_COMBINE
    npt = n_prompt // TN_COMBINE
    n_idx = TOP_K * TN_COMBINE
    cspec = lambda shape: pl.BlockSpec(shape, lambda t: (0,) * len(shape))
    return pl.pallas_call(
        functools.partial(_combine_kernel, n_prompt_tiles=npt),
        grid=(n_t,),
        in_specs=[pl.BlockSpec((1, 1, n_idx), lambda t: (t, 0, 0), memory_space=pltpu.SMEM),
                  pl.BlockSpec((1, 1, n_idx), lambda t: (jnp.minimum(t + 1, n_t - 1), 0, 0),
                               memory_space=pltpu.SMEM),
                  pl.BlockSpec(memory_space=pl.ANY),
                  pl.BlockSpec((TN_COMBINE * SUBLANES, LANES), lambda t: (t, 0)),
                  pl.BlockSpec((TN_COMBINE, TOP_K), lambda t: (t, 0)),
                  cspec((D_MODEL, SHARED_HIDDEN)), cspec((D_MODEL, SHARED_HIDDEN)), cspec((SHARED_HIDDEN, D_MODEL)),
                  cspec((1, D_MODEL)), cspec((1, D_MODEL))],
        out_specs=[pl.BlockSpec((TN_COMBINE, D_MODEL), lambda t: (jnp.minimum(t, npt - 1), 0)),
                   pl.BlockSpec((TN_COMBINE, D_MODEL), lambda t: (jnp.maximum(t - npt, 0), 0))],
        out_shape=[jax.ShapeDtypeStruct((n_prompt, D_MODEL), F32),
                   jax.ShapeDtypeStruct((n - n_prompt, D_MODEL), F32)],
        scratch_shapes=[pltpu.VMEM((2, n_idx * SUBLANES, LANES), F32), pltpu.SemaphoreType.DMA((2,))],
        compiler_params=pltpu.CompilerParams(dimension_semantics=("arbitrary",), vmem_limit_bytes=VMEM_LIMIT),
        name="combine",
    )(pos_tiles, pos_tiles, y_sorted, h1_tiles, w_tok, wsg, wsu, wsd, l2g, l2b)


def _work_items(seg_start, seg_end, n_assign):
    n_blocks = n_assign // BM
    block_start = jnp.arange(n_blocks, dtype=I32) * BM
    lo = jnp.sort(jnp.concatenate([block_start, seg_start]))
    hi = jnp.concatenate([lo[1:], jnp.full((1,), n_assign, I32)])
    blk = jnp.minimum(lo // BM, n_blocks - 1)
    exp = jnp.minimum(jnp.sum((seg_end[None, :] <= lo[:, None]).astype(I32), axis=1), N_EXPERTS - 1)
    item = jnp.arange(lo.shape[0], dtype=I32)
    exp_nonempty = jnp.where(hi > lo, exp, -1)
    prev_max = jnp.max(jnp.where(item[None, :] < item[:, None], exp_nonempty[None, :], -1), axis=1)
    opens = (hi > lo) & (exp > prev_max)
    seq = jnp.where(opens, jnp.cumsum(opens.astype(I32)) - 1, -1).astype(I32)
    later_open = (item[None, :] > item[:, None]) & opens[None, :]
    nxt = jnp.min(jnp.where(later_open, exp[None, :], N_EXPERTS), axis=1)
    nxt = jnp.where(nxt == N_EXPERTS, -1, nxt).astype(I32)
    return blk, exp, lo, hi, seq, nxt


def _pack_mixer_weights(ln_emb_g, ln_emb_b, w_in, w_gate_a2, b_gate, gla_norm_g, w_gla_out, conv_w,
                        w_conv_out, w_o, ln1_g, ln1_b):
    q, k, v, g, a, cb, cc, ch, ga, gb = jnp.split(
        w_in, [512, 1024, 2048, 3072, 3088, 4112, 5136, 6160, 7184], axis=-1)
    w_main = jnp.concatenate([q, k, v, g, cb, cc, ch, ga, gb], axis=-1).astype(BF16)
    w_a = jnp.pad(a, ((0, 0), (0, RANK_PAD - GATE_RANK))).astype(BF16)
    w_a2 = jnp.pad(w_gate_a2, ((0, RANK_PAD - GATE_RANK), (0, 0))).astype(BF16)
    row = lambda x: x.reshape(1, -1).astype(F32)
    return [row(ln_emb_g), row(ln_emb_b), w_main, w_a, w_a2, row(b_gate), row(gla_norm_g),
            w_gla_out.astype(BF16), conv_w.astype(F32), w_conv_out.astype(BF16), w_o.astype(BF16),
            row(ln1_g), row(ln1_b)]


def kernel(x_prompt, x_sample, state_gla, state_conv, meta_tokens, ln_emb_g, ln_emb_b, w_in, w_gate_a2, b_gate, gla_norm_g, w_gla_out, conv_w, w_conv_out, w_o, ln1_g, ln1_b, w_router, router_bias, w_exp_gate, w_exp_up, w_exp_down, w_sh_gate, w_sh_up, w_sh_down, ln2_g, ln2_b):
    batch, seq, _ = x_prompt.shape
    dec_batch, dec_seq, _ = x_sample.shape
    depth = w_in.shape[0]
    assert batch == 1 and depth == 1 and seq % TM == 0 and TM % dec_seq == 0 and dec_seq % SUB == 0
    n_prompt, n_sample = batch * seq, dec_batch * dec_seq
    n_total = n_prompt + n_sample
    assert n_sample % TM == 0 and n_total % TN_ROUTER == 0
    assert n_prompt % TN_COMBINE == 0 and n_sample % TN_COMBINE == 0 and n_total % TN_DISPATCH == 0
    n_assign = n_total * TOP_K
    assert n_assign % BM == 0

    weights = _pack_mixer_weights(ln_emb_g, ln_emb_b, w_in[0], w_gate_a2[0], b_gate[0], gla_norm_g[0],
                                  w_gla_out[0], conv_w[0], w_conv_out[0], w_o[0], ln1_g[0], ln1_b[0])
    h1, sg_p, sc_p, sg_s, sc_s = _mixer(meta_tokens.astype(F32), x_prompt.reshape(n_prompt, D_MODEL),
                                        x_sample.reshape(n_sample, D_MODEL), state_gla[0], state_conv[0], weights)

    wr_t = w_router[0].T
    wr_hi = wr_t.astype(BF16)
    wr_split = jnp.stack([wr_hi, (wr_t - wr_hi.astype(F32)).astype(BF16)])
    idx_t, w_t, rank_t, counts = _router(h1, wr_split, router_bias[0].reshape(N_EXPERTS, 1).astype(F32))

    counts = counts.reshape(N_EXPERTS).astype(I32)
    seg_end = jnp.cumsum(counts).astype(I32)
    seg_start = seg_end - counts
    pos_t = _positions(idx_t, rank_t, seg_start.astype(F32).reshape(N_EXPERTS, 1))
    pos_tiles = lambda tn: pos_t.reshape(TOP_K, n_total // tn, tn).transpose(1, 0, 2).reshape(n_total // tn, 1, -1)

    x_sorted = _dispatch(pos_tiles(TN_DISPATCH), h1)
    y_sorted = _experts(_work_items(seg_start, seg_end, n_assign), x_sorted,
                        w_exp_gate[0], w_exp_up[0], w_exp_down[0])
    y_prompt, y_sample = _combine(pos_tiles(TN_COMBINE), y_sorted, h1, w_t.T,
                                  w_sh_gate[0].astype(BF16), w_sh_up[0].astype(BF16), w_sh_down[0].astype(BF16),
                                  ln2_g[0].reshape(1, -1), ln2_b[0].reshape(1, -1), n_prompt)
    y_prompt = y_prompt.reshape(batch, seq, D_MODEL)
    y_sample = y_sample.reshape(dec_batch, dec_seq, D_MODEL)
    return (y_prompt, y_sample,
            sg_p.reshape(depth, batch, HEADS, DK, DV), sc_p.reshape(depth, batch, 2, D_MODEL),
            sg_s.reshape(depth, dec_batch, HEADS, DK, DV), sc_s.reshape(depth, dec_batch, 2, D_MODEL))
```

```python
import functools

import jax
import jax.numpy as jnp
from jax import lax
from jax.experimental import pallas as pl
from jax.experimental.pallas import tpu as pltpu

F32 = jnp.float32
BF16 = jnp.bfloat16
I32 = jnp.int32

D_MODEL = 1024
N_META = 16
HEADS = 4
DK = 128
DV = 256
GLA_DK = HEADS * DK
GLA_DV = HEADS * DV
GATE_RANK = 16
GATE_TAU = 16.0
N_EXPERTS = 256
TOP_K = 8
N_GROUPS = 8
GROUP_SIZE = N_EXPERTS // N_GROUPS
TOPK_GROUPS = 4
EXPERT_HIDDEN = 256
SHARED_HIDDEN = 256
ROUTED_SCALE = 2.5
LN_EPS = 1e-5
RMS_EPS = 1e-6
ALPHA = 2.0 ** 0.25
QSCALE = DK ** -0.5

Q0, K0, V0, G0, CB0, CC0, CH0, GA0, GB0 = 0, 512, 1024, 2048, 3072, 4096, 5120, 6144, 7168
W_MAIN_COLS = 8192
RANK_PAD = 128

SUB = 16
PROMPT_CHUNK = 64
TM = 256
TN_ROUTER = 512
TN_DISPATCH = 256
CH = 128
TN_COMBINE = 256
VMEM_LIMIT = 60 * 1024 * 1024
SUBLANES = 8
LANES = 128
assert D_MODEL == SUBLANES * LANES

_DN_TB = (((1,), (1,)), ((), ()))
_DN_TA = (((0,), (0,)), ((), ()))


def _dot(a, b):
    return jnp.dot(a, b, preferred_element_type=F32)


def _tiles_to_rows(ref, n, base=0):
    return jnp.concatenate(
        [ref[pl.ds(SUBLANES * base + s, n, stride=SUBLANES), :] for s in range(SUBLANES)], axis=1)


def _rows_to_tiles(ref, val):
    n = val.shape[0]
    for s in range(SUBLANES):
        ref[pl.ds(s, n, stride=SUBLANES), :] = val[:, s * LANES:(s + 1) * LANES]


def _layer_norm(x, g, b):
    mu = jnp.mean(x, axis=-1, keepdims=True)
    xc = x - mu
    var = jnp.mean(xc * xc, axis=-1, keepdims=True)
    return xc * lax.rsqrt(var + LN_EPS) * g + b


def _sigmoid(x):
    return 1.0 / (1.0 + jnp.exp(-x))


def _log_sigmoid(x):
    return jnp.minimum(x, 0.0) - jnp.log1p(jnp.exp(-jnp.abs(x)))


def _chunk_cumsum(g, chunk):
    t = g.shape[0]
    shift = chunk.bit_length() - 1
    r = lax.broadcasted_iota(I32, (t, t), 0)
    c = lax.broadcasted_iota(I32, (t, t), 1)
    tri = jnp.where(((r >> shift) == (c >> shift)) & (c <= r), 1.0, 0.0).astype(BF16)
    g1 = g.astype(BF16)
    r1 = g - g1.astype(F32)
    g2 = r1.astype(BF16)
    g3 = (r1 - g2.astype(F32)).astype(BF16)
    return _dot(tri, g1) + _dot(tri, g2) + _dot(tri, g3)


def _gate_log_decay(hb, wa_ref, wa2_ref, bg_ref):
    a = _dot(hb, wa_ref[...])
    return _log_sigmoid(_dot(a.astype(BF16), wa2_ref[...]) + bg_ref[...]) * (1.0 / GATE_TAU)


def _gla_state_update(k, v, b, st):
    n = k.shape[0]
    b_last = b[n - 1:n, :]
    khat = (k * jnp.exp(b_last - b)).astype(BF16)
    return st * jnp.exp(b_last) + lax.dot_general(v.astype(BF16), khat, _DN_TA, preferred_element_type=F32)


def _gla_chunk(q, k, v, b, st):
    n = q.shape[0]
    vb = v.astype(BF16)
    q0 = (q * (jnp.exp(b) * QSCALE)).astype(BF16)
    o = lax.dot_general(q0, st.astype(BF16), _DN_TB, preferred_element_type=F32)
    col = lax.broadcasted_iota(I32, (SUB, n), 1)
    row = lax.broadcasted_iota(I32, (SUB, n), 0)
    blocks = []
    for i in range(n // SUB):
        lo, hi = SUB * i, SUB * (i + 1)
        bref = b[lo:lo + 1, :]
        qi = (q[lo:hi] * (jnp.exp(b[lo:hi] - bref) * QSCALE)).astype(BF16)
        kk = (k[:hi] * jnp.exp(bref - b[:hi])).astype(BF16)
        if hi < n:
            kk = jnp.concatenate([kk, jnp.zeros((n - hi, DK), BF16)], axis=0)
        a = lax.dot_general(qi, kk, _DN_TB, preferred_element_type=F32)
        blocks.append(jnp.where(col <= row + lo, a, 0.0))
    scores = blocks[0] if len(blocks) == 1 else jnp.concatenate(blocks, axis=0)
    o = o + _dot(scores.astype(BF16), vb)
    return o, _gla_state_update(k, v, b, st)


def _mixer_back(h, hb, o_ref, conv, wm_ref, ng_ref, wgo_ref, wco_ref, wo_ref, l1g_ref, l1b_ref):
    g = _dot(hb, wm_ref[:, G0:G0 + GLA_DV])
    parts = []
    for hd in range(HEADS):
        oh = o_ref[:, hd * DV:(hd + 1) * DV]
        ms = jnp.mean(oh * oh, axis=-1, keepdims=True)
        parts.append(oh * lax.rsqrt(ms + RMS_EPS) * ng_ref[...])
    on = jnp.concatenate(parts, axis=1) * (g * _sigmoid(g))
    branch_a = _dot(on.astype(BF16), wgo_ref[...])
    mix = _sigmoid(_dot(hb, wm_ref[:, GA0:GA0 + D_MODEL])) * branch_a
    yc = _dot(hb, wm_ref[:, CB0:CB0 + D_MODEL]) * conv
    branch_b = _dot(yc.astype(BF16), wco_ref[...])
    mix = mix + _sigmoid(_dot(hb, wm_ref[:, GB0:GB0 + D_MODEL])) * branch_b
    mixed = _dot(mix.astype(BF16), wo_ref[...])
    return _layer_norm(ALPHA * h + mixed, l1g_ref[...], l1b_ref[...])


def _conv_input(hb, wm_ref):
    return _dot(hb, wm_ref[:, CC0:CC0 + D_MODEL]) * _dot(hb, wm_ref[:, CH0:CH0 + D_MODEL])


def _gla_rows(qkv_ref, b_ref, rows, hd, st):
    kc = slice(hd * DK, (hd + 1) * DK)
    return _gla_chunk(qkv_ref[rows, Q0 + hd * DK:Q0 + (hd + 1) * DK],
                      qkv_ref[rows, K0 + hd * DK:K0 + (hd + 1) * DK],
                      qkv_ref[rows, V0 + hd * DV:V0 + (hd + 1) * DV],
                      b_ref[rows, kc], st)


def _mixer_kernel(meta_ref, xp_ref, xs_ref, sgin_ref, scin_ref,
                  lng_ref, lnb_ref, wm_ref, wa_ref, wa2_ref, bg_ref,
                  ng_ref, wgo_ref, cw_ref, wco_ref, wo_ref, l1g_ref, l1b_ref,
                  h1_ref, sgp_ref, scp_ref, sgs_ref, scs_ref,
                  st_ref, zbuf_ref, qkv_ref, b_ref, o_ref, *, n_prompt_tiles, seq, streams):
    i = pl.program_id(0)
    back = functools.partial(_mixer_back, wm_ref=wm_ref, ng_ref=ng_ref, wgo_ref=wgo_ref, wco_ref=wco_ref,
                             wo_ref=wo_ref, l1g_ref=l1g_ref, l1b_ref=l1b_ref)

    def front(x):
        h = _layer_norm(x, lng_ref[...], lnb_ref[...])
        hb = h.astype(BF16)
        return h, hb, _gate_log_decay(hb, wa_ref, wa2_ref, bg_ref)

    @pl.when(i == 0)
    def _meta():
        _, hb, gk = front(meta_ref[...])
        k = _dot(hb, wm_ref[:, K0:K0 + GLA_DK])
        v = _dot(hb, wm_ref[:, V0:V0 + GLA_DV])
        b = _chunk_cumsum(gk, N_META)
        zero_state = jnp.zeros((DV, DK), F32)
        for hd in range(HEADS):
            kc = slice(hd * DK, (hd + 1) * DK)
            st_ref[hd] = _gla_state_update(k[:, kc], v[:, hd * DV:(hd + 1) * DV], b[:, kc], zero_state)
        zbuf_ref[0:8, :] = _conv_input(hb, wm_ref)[N_META - 8:N_META, :]

    @pl.when((i >= 1) & (i <= n_prompt_tiles))
    def _prompt_tile():
        h, hb, gk = front(xp_ref[...])
        qkv_ref[...] = _dot(hb, wm_ref[:, Q0:G0])
        b_ref[...] = _chunk_cumsum(gk, PROMPT_CHUNK)
        for c in range(TM // PROMPT_CHUNK):
            rows = slice(c * PROMPT_CHUNK, (c + 1) * PROMPT_CHUNK)
            for hd in range(HEADS):
                o, st_new = _gla_rows(qkv_ref, b_ref, rows, hd, st_ref[hd])
                st_ref[hd] = st_new
                o_ref[rows, hd * DV:(hd + 1) * DV] = o
        z = _conv_input(hb, wm_ref)
        zbuf_ref[8:8 + TM, :] = z
        cw = cw_ref[...]
        conv = cw[0:1, :] * zbuf_ref[6:6 + TM, :] + cw[1:2, :] * zbuf_ref[7:7 + TM, :] + cw[2:3, :] * z
        zbuf_ref[0:8, :] = z[TM - 8:TM, :]
        _rows_to_tiles(h1_ref, back(h, hb, o_ref, conv))

        @pl.when(i == n_prompt_tiles)
        def _prompt_states():
            for hd in range(HEADS):
                sgp_ref[hd] = st_ref[hd].T
            scp_ref[...] = z[TM - 2:TM, :]

    @pl.when(i > n_prompt_tiles)
    def _sample_tile():
        h, hb, gk = front(xs_ref[...])
        qkv_ref[...] = _dot(hb, wm_ref[:, Q0:G0])
        b_ref[...] = _chunk_cumsum(gk, seq)
        z = _conv_input(hb, wm_ref)
        pitch = seq + 8
        cw = cw_ref[...]
        convs = []
        for s in range(streams):
            rows = slice(s * seq, (s + 1) * seq)
            for hd in range(HEADS):
                o, st_new = _gla_rows(qkv_ref, b_ref, rows, hd, sgin_ref[s, hd].T)
                sgs_ref[s, hd] = st_new.T
                o_ref[rows, hd * DV:(hd + 1) * DV] = o
            zs = z[rows, :]
            base = s * pitch
            zbuf_ref[base + 6:base + 8, :] = scin_ref[s]
            zbuf_ref[base + 8:base + 8 + seq, :] = zs
            convs.append(cw[0:1, :] * zbuf_ref[base + 6:base + 6 + seq, :]
                         + cw[1:2, :] * zbuf_ref[base + 7:base + 7 + seq, :] + cw[2:3, :] * zs)
            scs_ref[s] = zs[seq - 2:seq, :]
        _rows_to_tiles(h1_ref, back(h, hb, o_ref, jnp.concatenate(convs, axis=0)))


def _const_spec(shape):
    nd = len(shape)
    return pl.BlockSpec(shape, lambda i, _nd=nd: (0,) * _nd, pipeline_mode=pl.Buffered(1))


def _mixer_weight_specs():
    return [
        _const_spec((1, D_MODEL)), _const_spec((1, D_MODEL)),
        _const_spec((D_MODEL, W_MAIN_COLS)),
        _const_spec((D_MODEL, RANK_PAD)), _const_spec((RANK_PAD, GLA_DK)), _const_spec((1, GLA_DK)),
        _const_spec((1, DV)), _const_spec((GLA_DV, D_MODEL)),
        _const_spec((3, D_MODEL)), _const_spec((D_MODEL, D_MODEL)),
        _const_spec((D_MODEL, D_MODEL)),
        _const_spec((1, D_MODEL)), _const_spec((1, D_MODEL)),
    ]


def _mixer(meta, x_prompt, x_sample, state_gla, state_conv, weights):
    n_streams = state_conv.shape[0]
    seq = x_sample.shape[0] // n_streams
    streams = TM // seq
    npt = x_prompt.shape[0] // TM
    nst = n_streams // streams
    ptile = lambda i: (jnp.clip(i - 1, 0, npt - 1), 0)
    stile = lambda i: jnp.clip(i - 1 - npt, 0, nst - 1)
    kern = functools.partial(_mixer_kernel, n_prompt_tiles=npt, seq=seq, streams=streams)
    return pl.pallas_call(
        kern,
        grid=(1 + npt + nst,),
        in_specs=[_const_spec((N_META, D_MODEL)),
                  pl.BlockSpec((TM, D_MODEL), ptile),
                  pl.BlockSpec((TM, D_MODEL), lambda i: (stile(i), 0), pipeline_mode=pl.Buffered(1)),
                  pl.BlockSpec((streams, HEADS, DK, DV), lambda i: (stile(i), 0, 0, 0),
                               pipeline_mode=pl.Buffered(1)),
                  pl.BlockSpec((streams, 2, D_MODEL), lambda i: (stile(i), 0, 0))] + _mixer_weight_specs(),
        out_specs=[pl.BlockSpec((TM * SUBLANES, LANES), lambda i: (jnp.maximum(i - 1, 0), 0)),
                   pl.BlockSpec((HEADS, DK, DV), lambda i: (0, 0, 0)),
                   pl.BlockSpec((2, D_MODEL), lambda i: (0, 0)),
                   pl.BlockSpec((streams, HEADS, DK, DV), lambda i: (stile(i), 0, 0, 0)),
                   pl.BlockSpec((streams, 2, D_MODEL), lambda i: (stile(i), 0, 0))],
        out_shape=[jax.ShapeDtypeStruct(((x_prompt.shape[0] + x_sample.shape[0]) * SUBLANES, LANES), F32),
                   jax.ShapeDtypeStruct((HEADS, DK, DV), F32),
                   jax.ShapeDtypeStruct((2, D_MODEL), F32),
                   jax.ShapeDtypeStruct(state_gla.shape, F32),
                   jax.ShapeDtypeStruct(state_conv.shape, F32)],
        scratch_shapes=[pltpu.VMEM((HEADS, DV, DK), F32),
                        pltpu.VMEM((max(TM + 8, streams * (seq + 8)), D_MODEL), F32),
                        pltpu.VMEM((TM, G0), F32),
                        pltpu.VMEM((TM, GLA_DK), F32),
                        pltpu.VMEM((TM, GLA_DV), F32)],
        compiler_params=pltpu.CompilerParams(dimension_semantics=("arbitrary",), vmem_limit_bytes=VMEM_LIMIT),
        name="mixer",
    )(meta, x_prompt, x_sample, state_gla, state_conv, *weights)


def _first_index_of_max(vals, ids, sentinel):
    m = jnp.max(vals, axis=0, keepdims=True)
    first = jnp.min(jnp.where(vals == m, ids, sentinel), axis=0, keepdims=True)
    return m, first


def _router_kernel(h_ref, wr_ref, bias_ref, idx_ref, w_ref, rank_ref, counts_ref):
    @pl.when(pl.program_id(0) == 0)
    def _init():
        counts_ref[...] = jnp.zeros_like(counts_ref)

    h = _tiles_to_rows(h_ref, TN_ROUTER)
    hh = h.astype(BF16)
    hl = (h - hh.astype(F32)).astype(BF16)
    wh, wl = wr_ref[0], wr_ref[1]
    dg = functools.partial(lax.dot_general, dimension_numbers=_DN_TB, preferred_element_type=F32)
    logits = dg(wh, hh) + dg(wl, hh) + dg(wh, hl)
    scores = _sigmoid(logits)
    biased = scores + bias_ref[...]
    n_tok = biased.shape[1]
    neg_inf = jnp.float32(-jnp.inf)

    eid = lax.broadcasted_iota(I32, (N_EXPERTS, n_tok), 0).astype(F32)
    lid = lax.broadcasted_iota(I32, (GROUP_SIZE, n_tok), 0).astype(F32)
    group_scores = []
    for g in range(N_GROUPS):
        blk = biased[g * GROUP_SIZE:(g + 1) * GROUP_SIZE]
        m1, first = _first_index_of_max(blk, lid, float(GROUP_SIZE))
        m2 = jnp.max(jnp.where(lid == first, neg_inf, blk), axis=0, keepdims=True)
        group_scores.append(m1 + m2)
    gsc = jnp.concatenate(group_scores, axis=0)

    gid = lax.broadcasted_iota(I32, (N_GROUPS, n_tok), 0).astype(F32)
    keep = jnp.zeros((N_GROUPS, n_tok), F32)
    cur = gsc
    for _ in range(TOPK_GROUPS):
        _, first = _first_index_of_max(cur, gid, float(N_GROUPS))
        sel = gid == first
        keep = jnp.where(sel, 1.0, keep)
        cur = jnp.where(sel, neg_inf, cur)

    masked = jnp.concatenate(
        [jnp.where(keep[g:g + 1] > 0.5, biased[g * GROUP_SIZE:(g + 1) * GROUP_SIZE], neg_inf)
         for g in range(N_GROUPS)], axis=0)
    idxs, wts, sels = [], [], []
    cur = masked
    for _ in range(TOP_K):
        _, first = _first_index_of_max(cur, eid, float(N_EXPERTS))
        sel = eid == first
        idxs.append(first)
        sels.append(sel)
        wts.append(jnp.sum(jnp.where(sel, scores, 0.0), axis=0, keepdims=True))
        cur = jnp.where(sel, neg_inf, cur)
    w = jnp.concatenate(wts, axis=0)
    idx_ref[...] = jnp.concatenate(idxs, axis=0).astype(I32)
    w_ref[...] = w / jnp.sum(w, axis=0, keepdims=True) * ROUTED_SCALE

    chosen = jnp.where(sels[0], 1.0, 0.0)
    for sel in sels[1:]:
        chosen = chosen + jnp.where(sel, 1.0, 0.0)
    t_src = lax.broadcasted_iota(I32, (n_tok, n_tok), 0)
    t_dst = lax.broadcasted_iota(I32, (n_tok, n_tok), 1)
    earlier = jnp.where(t_src < t_dst, 1.0, 0.0).astype(BF16)
    arrivals = counts_ref[...] + _dot(chosen.astype(BF16), earlier)
    rank_ref[...] = jnp.concatenate(
        [jnp.sum(jnp.where(sel, arrivals, 0.0), axis=0, keepdims=True) for sel in sels], axis=0).astype(I32)
    counts_ref[...] += jnp.sum(chosen, axis=1, keepdims=True)


def _router(h1_tiles, wr_split, bias_col):
    n = h1_tiles.shape[0] // SUBLANES
    tok = pl.BlockSpec((TOP_K, TN_ROUTER), lambda i: (0, i))
    return pl.pallas_call(
        _router_kernel,
        grid=(n // TN_ROUTER,),
        in_specs=[pl.BlockSpec((TN_ROUTER * SUBLANES, LANES), lambda i: (i, 0)),
                  pl.BlockSpec((2, N_EXPERTS, D_MODEL), lambda i: (0, 0, 0)),
                  pl.BlockSpec((N_EXPERTS, 1), lambda i: (0, 0))],
        out_specs=[tok, tok, tok, pl.BlockSpec((N_EXPERTS, 1), lambda i: (0, 0))],
        out_shape=[jax.ShapeDtypeStruct((TOP_K, n), I32), jax.ShapeDtypeStruct((TOP_K, n), F32),
                   jax.ShapeDtypeStruct((TOP_K, n), I32), jax.ShapeDtypeStruct((N_EXPERTS, 1), F32)],
        compiler_params=pltpu.CompilerParams(dimension_semantics=("arbitrary",), vmem_limit_bytes=VMEM_LIMIT),
        name="router",
    )(h1_tiles, wr_split, bias_col)


def _positions_kernel(idx_ref, rank_ref, seg_ref, pos_ref):
    n_tok = idx_ref.shape[1]
    eid = lax.broadcasted_iota(I32, (N_EXPERTS, n_tok), 0)
    seg = seg_ref[...]
    rows = []
    for k in range(TOP_K):
        onehot = eid == idx_ref[k:k + 1, :]
        rows.append(jnp.sum(jnp.where(onehot, seg, 0.0), axis=0, keepdims=True))
    pos_ref[...] = jnp.concatenate(rows, axis=0).astype(I32) + rank_ref[...]


def _positions(idx_t, rank_t, seg_start_col):
    n = idx_t.shape[1]
    tok = pl.BlockSpec((TOP_K, TN_ROUTER), lambda i: (0, i))
    return pl.pallas_call(
        _positions_kernel,
        grid=(n // TN_ROUTER,),
        in_specs=[tok, tok, pl.BlockSpec((N_EXPERTS, 1), lambda i: (0, 0))],
        out_specs=tok,
        out_shape=jax.ShapeDtypeStruct((TOP_K, n), I32),
        compiler_params=pltpu.CompilerParams(dimension_semantics=("arbitrary",)),
        name="positions",
    )(idx_t, rank_t, seg_start_col)


def _tile_copy(src, src_tile, dst, dst_tile, sem):
    return pltpu.make_async_copy(src.at[pl.ds(pl.multiple_of(src_tile * SUBLANES, SUBLANES), SUBLANES)],
                                 dst.at[pl.ds(pl.multiple_of(dst_tile * SUBLANES, SUBLANES), SUBLANES)], sem)


def _dispatch_kernel(pos_ref, h_ref, x_hbm, zeros, sem, zsem, *, n_assign):
    @pl.when(pl.program_id(0) == 0)
    def _zero_slack():
        zeros[...] = jnp.zeros_like(zeros)
        fill = pltpu.make_async_copy(zeros, x_hbm.at[pl.ds(n_assign * SUBLANES, CH * SUBLANES)], zsem)
        fill.start()
        fill.wait()

    def body(t, carry):
        for k in range(TOP_K):
            _tile_copy(h_ref, t, x_hbm, pos_ref[0, 0, k * TN_DISPATCH + t], sem.at[k]).start(priority=k % 2)
        return carry
    lax.fori_loop(0, TN_DISPATCH, body, 0, unroll=2)
    for k in range(TOP_K):
        pltpu.make_async_copy(h_ref, x_hbm.at[pl.ds(0, TN_DISPATCH * SUBLANES)], sem.at[k]).wait()


def _dispatch(pos_tiles, h1_tiles):
    n_t = pos_tiles.shape[0]
    n_assign = n_t * TN_DISPATCH * TOP_K
    return pl.pallas_call(
        functools.partial(_dispatch_kernel, n_assign=n_assign),
        grid=(n_t,),
        in_specs=[pl.BlockSpec((1, 1, TOP_K * TN_DISPATCH), lambda t: (t, 0, 0), memory_space=pltpu.SMEM),
                  pl.BlockSpec((TN_DISPATCH * SUBLANES, LANES), lambda t: (t, 0))],
        out_specs=pl.BlockSpec(memory_space=pl.ANY),
        out_shape=jax.ShapeDtypeStruct(((n_assign + CH) * SUBLANES, LANES), F32),
        scratch_shapes=[pltpu.VMEM((CH * SUBLANES, LANES), F32), pltpu.SemaphoreType.DMA((TOP_K,)),
                        pltpu.SemaphoreType.DMA(())],
        compiler_params=pltpu.CompilerParams(dimension_semantics=("arbitrary",), has_side_effects=True),
        name="dispatch",
    )(pos_tiles, h1_tiles)


def _expert_kernel(start_ref, cnt_ref, seq_ref, nxt_ref, x_hbm, wg_hbm, wu_hbm, wd_hbm, y_hbm,
                   xbuf, ybuf, wg32, wu32, wd32, wgb, wub, wdb, xsem, ysem, wsem, state, *, n_assign):
    e = pl.program_id(0)
    cnt = cnt_ref[e]

    def x_copy(tile0, slot):
        return pltpu.make_async_copy(x_hbm.at[pl.ds(pl.multiple_of(tile0 * SUBLANES, SUBLANES), CH * SUBLANES)],
                                     xbuf.at[slot], xsem.at[slot])

    def y_copy(tile0, slot):
        return pltpu.make_async_copy(ybuf.at[slot],
                                     y_hbm.at[pl.ds(pl.multiple_of(tile0 * SUBLANES, SUBLANES), CH * SUBLANES)],
                                     ysem.at[slot])

    def weight_copies(ex, slot):
        return (pltpu.make_async_copy(wg_hbm.at[ex], wg32.at[slot], wsem.at[slot]),
                pltpu.make_async_copy(wu_hbm.at[ex], wu32.at[slot], wsem.at[slot]),
                pltpu.make_async_copy(wd_hbm.at[ex], wd32.at[slot], wsem.at[slot]))

    @pl.when(e == 0)
    def _init():
        state[0] = 0
        state[1] = 0
        state[2] = 0
        ybuf[0] = jnp.zeros(ybuf.shape[1:], F32)
        fill = y_copy(n_assign, 0)
        fill.start()
        fill.wait()

    @pl.when(cnt > 0)
    def _expert():
        seq, nxt, start = seq_ref[e], nxt_ref[e], start_ref[e]
        wslot = seq % 2

        @pl.when(seq == 0)
        def _fetch_own():
            for c in weight_copies(e, wslot):
                c.start()
            x_copy(start, state[0]).start()

        @pl.when(nxt >= 0)
        def _prefetch_weights():
            for c in weight_copies(nxt, 1 - wslot):
                c.start()

        for c in weight_copies(e, wslot):
            c.wait()
        wgb[...] = wg32[wslot].astype(BF16)
        wub[...] = wu32[wslot].astype(BF16)
        wdb[...] = wd32[wslot].astype(BF16)

        n_chunks = (cnt + CH - 1) // CH

        def chunk(j, carry):
            xs = state[0]
            x_copy(0, xs).wait()
            more = j + 1 < n_chunks

            @pl.when(more)
            def _next_chunk():
                x_copy(start + (j + 1) * CH, 1 - xs).start()

            @pl.when(jnp.logical_not(more) & (nxt >= 0))
            def _next_expert():
                x_copy(start_ref[jnp.maximum(nxt, 0)], 1 - xs).start()

            state[0] = 1 - xs
            xb = _tiles_to_rows(xbuf.at[xs], CH).astype(BF16)
            g = _dot(xb, wgb[...])
            u = _dot(xb, wub[...])
            res = _dot(((g * _sigmoid(g)) * u).astype(BF16), wdb[...])
            ys = state[1]
            _rows_to_tiles(ybuf.at[ys], res)

            @pl.when(state[2] == 1)
            def _previous_y_done():
                y_copy(0, 1 - ys).wait()

            y_copy(start + j * CH, ys).start()
            state[1] = 1 - ys
            state[2] = 1
            return carry

        lax.fori_loop(0, n_chunks, chunk, 0)

    @pl.when(e == pl.num_programs(0) - 1)
    def _drain():
        @pl.when(state[2] == 1)
        def _last_y_done():
            y_copy(0, 1 - state[1]).wait()


def _experts(seg_start, counts, x_sorted, w_gate, w_up, w_down, n_assign):
    has_rows = counts > 0
    seq = jnp.where(has_rows, jnp.cumsum(has_rows.astype(I32)) - 1, -1).astype(I32)
    eid = jnp.arange(N_EXPERTS, dtype=I32)
    nxt = jnp.min(jnp.where((eid[None, :] > eid[:, None]) & has_rows[None, :], eid[None, :], N_EXPERTS), axis=1)
    nxt = jnp.where(nxt == N_EXPERTS, -1, nxt).astype(I32)
    hbm = pl.BlockSpec(memory_space=pl.ANY)
    grid_spec = pltpu.PrefetchScalarGridSpec(
        num_scalar_prefetch=4,
        grid=(N_EXPERTS,),
        in_specs=[hbm, hbm, hbm, hbm],
        out_specs=hbm,
        scratch_shapes=[pltpu.VMEM((2, CH * SUBLANES, LANES), F32),
                        pltpu.VMEM((2, CH * SUBLANES, LANES), F32),
                        pltpu.VMEM((2, D_MODEL, EXPERT_HIDDEN), F32),
                        pltpu.VMEM((2, D_MODEL, EXPERT_HIDDEN), F32),
                        pltpu.VMEM((2, EXPERT_HIDDEN, D_MODEL), F32),
                        pltpu.VMEM((D_MODEL, EXPERT_HIDDEN), BF16),
                        pltpu.VMEM((D_MODEL, EXPERT_HIDDEN), BF16),
                        pltpu.VMEM((EXPERT_HIDDEN, D_MODEL), BF16),
                        pltpu.SemaphoreType.DMA((2,)), pltpu.SemaphoreType.DMA((2,)),
                        pltpu.SemaphoreType.DMA((2,)), pltpu.SMEM((3,), I32)],
    )
    return pl.pallas_call(
        functools.partial(_expert_kernel, n_assign=n_assign),
        grid_spec=grid_spec,
        out_shape=jax.ShapeDtypeStruct(x_sorted.shape, F32),
        compiler_params=pltpu.CompilerParams(dimension_semantics=("arbitrary",), vmem_limit_bytes=VMEM_LIMIT,
                                             has_side_effects=True),
        name="experts",
    )(seg_start, counts, seq, nxt, x_sorted, w_gate, w_up, w_down)


def _combine_kernel(pos_cur_ref, pos_nxt_ref, y_hbm, h_ref, w_ref, wsg_ref, wsu_ref, wsd_ref, l2g_ref, l2b_ref,
                    outp_ref, outs_ref, ybuf, sem, *, n_prompt_tiles):
    t = pl.program_id(0)
    n_t = pl.num_programs(0)
    slot = t % 2

    def gather_start(pos_ref, dst, dst_sem):
        def body(r, carry):
            for k in range(TOP_K):
                j = k * TN_COMBINE + r
                _tile_copy(y_hbm, pos_ref[0, 0, j], dst, j, dst_sem).start(priority=k % 2)
            return carry
        lax.fori_loop(0, TN_COMBINE, body, 0, unroll=2)

    @pl.when(t == 0)
    def _first():
        gather_start(pos_cur_ref, ybuf.at[0], sem.at[0])

    @pl.when(t + 1 < n_t)
    def _prefetch():
        gather_start(pos_nxt_ref, ybuf.at[1 - slot], sem.at[1 - slot])

    h = _tiles_to_rows(h_ref, TN_COMBINE)
    hb = h.astype(BF16)
    g = _dot(hb, wsg_ref[...])
    u = _dot(hb, wsu_ref[...])
    shared = _dot(((g * _sigmoid(g)) * u).astype(BF16), wsd_ref[...])

    yslot = ybuf.at[slot]
    pltpu.make_async_copy(y_hbm.at[pl.ds(0, TOP_K * TN_COMBINE * SUBLANES)], yslot, sem.at[slot]).wait()
    w = w_ref[...]
    wk = [jnp.broadcast_to(w[:, k:k + 1], (TN_COMBINE, LANES)) for k in range(TOP_K)]
    chunks = []
    for s in range(SUBLANES):
        acc = wk[0] * yslot[pl.ds(s, TN_COMBINE, stride=SUBLANES), :]
        for k in range(1, TOP_K):
            acc = acc + wk[k] * yslot[pl.ds(k * TN_COMBINE * SUBLANES + s, TN_COMBINE, stride=SUBLANES), :]
        chunks.append(acc)
    routed = jnp.concatenate(chunks, axis=1)
    out = _layer_norm(ALPHA * h + (routed + shared), l2g_ref[...], l2b_ref[...])

    @pl.when(t < n_prompt_tiles)
    def _prompt_rows():
        outp_ref[...] = out

    @pl.when(t >= n_prompt_tiles)
    def _sample_rows():
        outs_ref[...] = out


def _combine(pos_tiles, y_sorted, h1_tiles, w_tok, wsg, wsu, wsd, l2g, l2b, n_prompt):
    n = h1_tiles.shape[0] // SUBLANES
    n_t = n // TN_COMBINE
    npt = n_prompt // TN_COMBINE
    n_idx = TOP_K * TN_COMBINE
    cspec = lambda shape: pl.BlockSpec(shape, lambda t: (0,) * len(shape))
    return pl.pallas_call(
        functools.partial(_combine_kernel, n_prompt_tiles=npt),
        grid=(n_t,),
        in_specs=[pl.BlockSpec((1, 1, n_idx), lambda t: (t, 0, 0), memory_space=pltpu.SMEM),
                  pl.BlockSpec((1, 1, n_idx), lambda t: (jnp.minimum(t + 1, n_t - 1), 0, 0),
                               memory_space=pltpu.SMEM),
                  pl.BlockSpec(memory_space=pl.ANY),
                  pl.BlockSpec((TN_COMBINE * SUBLANES, LANES), lambda t: (t, 0)),
                  pl.BlockSpec((TN_COMBINE, TOP_K), lambda t: (t, 0)),
                  cspec((D_MODEL, SHARED_HIDDEN)), cspec((D_MODEL, SHARED_HIDDEN)), cspec((SHARED_HIDDEN, D_MODEL)),
                  cspec((1, D_MODEL)), cspec((1, D_MODEL))],
        out_specs=[pl.BlockSpec((TN_COMBINE, D_MODEL), lambda t: (jnp.minimum(t, npt - 1), 0)),
                   pl.BlockSpec((TN_COMBINE, D_MODEL), lambda t: (jnp.maximum(t - npt, 0), 0))],
        out_shape=[jax.ShapeDtypeStruct((n_prompt, D_MODEL), F32),
                   jax.ShapeDtypeStruct((n - n_prompt, D_MODEL), F32)],
        scratch_shapes=[pltpu.VMEM((2, n_idx * SUBLANES, LANES), F32), pltpu.SemaphoreType.DMA((2,))],
        compiler_params=pltpu.CompilerParams(dimension_semantics=("arbitrary",), vmem_limit_bytes=VMEM_LIMIT),
        name="combine",
    )(pos_tiles, pos_tiles, y_sorted, h1_tiles, w_tok, wsg, wsu, wsd, l2g, l2b)


def _pack_mixer_weights(ln_emb_g, ln_emb_b, w_in, w_gate_a2, b_gate, gla_norm_g, w_gla_out, conv_w,
                        w_conv_out, w_o, ln1_g, ln1_b):
    q, k, v, g, a, cb, cc, ch, ga, gb = jnp.split(
        w_in, [512, 1024, 2048, 3072, 3088, 4112, 5136, 6160, 7184], axis=-1)
    w_main = jnp.concatenate([q, k, v, g, cb, cc, ch, ga, gb], axis=-1).astype(BF16)
    w_a = jnp.pad(a, ((0, 0), (0, RANK_PAD - GATE_RANK))).astype(BF16)
    w_a2 = jnp.pad(w_gate_a2, ((0, RANK_PAD - GATE_RANK), (0, 0))).astype(BF16)
    row = lambda x: x.reshape(1, -1).astype(F32)
    return [row(ln_emb_g), row(ln_emb_b), w_main, w_a, w_a2, row(b_gate), row(gla_norm_g),
            w_gla_out.astype(BF16), conv_w.astype(F32), w_conv_out.astype(BF16), w_o.astype(BF16),
            row(ln1_g), row(ln1_b)]


def kernel(x_prompt, x_sample, state_gla, state_conv, meta_tokens, ln_emb_g, ln_emb_b, w_in, w_gate_a2, b_gate, gla_norm_g, w_gla_out, conv_w, w_conv_out, w_o, ln1_g, ln1_b, w_router, router_bias, w_exp_gate, w_exp_up, w_exp_down, w_sh_gate, w_sh_up, w_sh_down, ln2_g, ln2_b):
    batch, seq, _ = x_prompt.shape
    dec_batch, dec_seq, _ = x_sample.shape
    depth = w_in.shape[0]
    assert batch == 1 and depth == 1 and seq % TM == 0 and TM % dec_seq == 0 and dec_seq % SUB == 0
    n_prompt, n_sample = batch * seq, dec_batch * dec_seq
    n_total = n_prompt + n_sample
    assert n_sample % TM == 0 and n_total % TN_ROUTER == 0
    assert n_prompt % TN_COMBINE == 0 and n_sample % TN_COMBINE == 0 and n_total % TN_DISPATCH == 0
    n_assign = n_total * TOP_K

    weights = _pack_mixer_weights(ln_emb_g, ln_emb_b, w_in[0], w_gate_a2[0], b_gate[0], gla_norm_g[0],
                                  w_gla_out[0], conv_w[0], w_conv_out[0], w_o[0], ln1_g[0], ln1_b[0])
    h1, sg_p, sc_p, sg_s, sc_s = _mixer(meta_tokens.astype(F32), x_prompt.reshape(n_prompt, D_MODEL),
                                        x_sample.reshape(n_sample, D_MODEL), state_gla[0], state_conv[0], weights)

    wr_t = w_router[0].T
    wr_hi = wr_t.astype(BF16)
    wr_split = jnp.stack([wr_hi, (wr_t - wr_hi.astype(F32)).astype(BF16)])
    idx_t, w_t, rank_t, counts = _router(h1, wr_split, router_bias[0].reshape(N_EXPERTS, 1).astype(F32))

    counts = counts.reshape(N_EXPERTS).astype(I32)
    seg_end = jnp.cumsum(counts).astype(I32)
    seg_start = seg_end - counts
    pos_t = _positions(idx_t, rank_t, seg_start.astype(F32).reshape(N_EXPERTS, 1))
    pos_tiles = lambda tn: pos_t.reshape(TOP_K, n_total // tn, tn).transpose(1, 0, 2).reshape(n_total // tn, 1, -1)

    x_sorted = _dispatch(pos_tiles(TN_DISPATCH), h1)
    y_sorted = _experts(seg_start, counts, x_sorted, w_exp_gate[0], w_exp_up[0], w_exp_down[0], n_assign)
    y_prompt, y_sample = _combine(pos_tiles(TN_COMBINE), y_sorted, h1, w_t.T,
                                  w_sh_gate[0].astype(BF16), w_sh_up[0].astype(BF16), w_sh_down[0].astype(BF16),
                                  ln2_g[0].reshape(1, -1), ln2_b[0].reshape(1, -1), n_prompt)
    y_prompt = y_prompt.reshape(batch, seq, D_MODEL)
    y_sample = y_sample.reshape(dec_batch, dec_seq, D_MODEL)
    return (y_prompt, y_sample,
            sg_p.reshape(depth, batch, HEADS, DK, DV), sc_p.reshape(depth, batch, 2, D_MODEL),
            sg_s.reshape(depth, dec_batch, HEADS, DK, DV), sc_s.reshape(depth, dec_batch, 2, D_MODEL))
```

```python
import functools

import jax
import jax.numpy as jnp
from jax import lax
from jax.experimental import pallas as pl
from jax.experimental.pallas import tpu as pltpu

F32 = jnp.float32
BF16 = jnp.bfloat16
I32 = jnp.int32

D_MODEL = 1024
N_META = 16
HEADS = 4
DK = 128
DV = 256
GLA_DK = HEADS * DK
GLA_DV = HEADS * DV
GATE_RANK = 16
GATE_TAU = 16.0
N_EXPERTS = 256
TOP_K = 8
N_GROUPS = 8
GROUP_SIZE = N_EXPERTS // N_GROUPS
TOPK_GROUPS = 4
EXPERT_HIDDEN = 256
SHARED_HIDDEN = 256
ROUTED_SCALE = 2.5
LN_EPS = 1e-5
RMS_EPS = 1e-6
ALPHA = 2.0 ** 0.25
QSCALE = DK ** -0.5

Q0, K0, V0, G0, CB0, CC0, CH0, GA0, GB0 = 0, 512, 1024, 2048, 3072, 4096, 5120, 6144, 7168
W_MAIN_COLS = 8192
RANK_PAD = 128

SUB = 16
PROMPT_CHUNK = 64
TM = 256
TN_ROUTER = 512
TN_DISPATCH = 256
CH = 256
NX = 3
NY = 2
TN_COMBINE = 256
VMEM_LIMIT = 60 * 1024 * 1024
SUBLANES = 8
LANES = 128
assert D_MODEL == SUBLANES * LANES

_DN_TB = (((1,), (1,)), ((), ()))
_DN_TA = (((0,), (0,)), ((), ()))


def _dot(a, b):
    return jnp.dot(a, b, preferred_element_type=F32)


def _tiles_to_rows(ref, n, base=0):
    return jnp.concatenate(
        [ref[pl.ds(SUBLANES * base + s, n, stride=SUBLANES), :] for s in range(SUBLANES)], axis=1)


def _rows_to_tiles(ref, val):
    n = val.shape[0]
    for s in range(SUBLANES):
        ref[pl.ds(s, n, stride=SUBLANES), :] = val[:, s * LANES:(s + 1) * LANES]


def _layer_norm(x, g, b):
    mu = jnp.mean(x, axis=-1, keepdims=True)
    xc = x - mu
    var = jnp.mean(xc * xc, axis=-1, keepdims=True)
    return xc * lax.rsqrt(var + LN_EPS) * g + b


def _sigmoid(x):
    return 1.0 / (1.0 + jnp.exp(-x))


def _log_sigmoid(x):
    return jnp.minimum(x, 0.0) - jnp.log1p(jnp.exp(-jnp.abs(x)))


def _chunk_cumsum(g, chunk):
    t = g.shape[0]
    shift = chunk.bit_length() - 1
    r = lax.broadcasted_iota(I32, (t, t), 0)
    c = lax.broadcasted_iota(I32, (t, t), 1)
    tri = jnp.where(((r >> shift) == (c >> shift)) & (c <= r), 1.0, 0.0).astype(BF16)
    g1 = g.astype(BF16)
    r1 = g - g1.astype(F32)
    g2 = r1.astype(BF16)
    g3 = (r1 - g2.astype(F32)).astype(BF16)
    return _dot(tri, g1) + _dot(tri, g2) + _dot(tri, g3)


def _gate_log_decay(hb, wa_ref, wa2_ref, bg_ref):
    a = _dot(hb, wa_ref[...])
    return _log_sigmoid(_dot(a.astype(BF16), wa2_ref[...]) + bg_ref[...]) * (1.0 / GATE_TAU)


def _gla_state_update(k, v, b, st):
    n = k.shape[0]
    b_last = b[n - 1:n, :]
    khat = (k * jnp.exp(b_last - b)).astype(BF16)
    return st * jnp.exp(b_last) + lax.dot_general(v.astype(BF16), khat, _DN_TA, preferred_element_type=F32)


def _gla_chunk(q, k, v, b, st):
    n = q.shape[0]
    vb = v.astype(BF16)
    q0 = (q * (jnp.exp(b) * QSCALE)).astype(BF16)
    o = lax.dot_general(q0, st.astype(BF16), _DN_TB, preferred_element_type=F32)
    col = lax.broadcasted_iota(I32, (SUB, n), 1)
    row = lax.broadcasted_iota(I32, (SUB, n), 0)
    blocks = []
    for i in range(n // SUB):
        lo, hi = SUB * i, SUB * (i + 1)
        bref = b[lo:lo + 1, :]
        qi = (q[lo:hi] * (jnp.exp(b[lo:hi] - bref) * QSCALE)).astype(BF16)
        kk = (k[:hi] * jnp.exp(bref - b[:hi])).astype(BF16)
        if hi < n:
            kk = jnp.concatenate([kk, jnp.zeros((n - hi, DK), BF16)], axis=0)
        a = lax.dot_general(qi, kk, _DN_TB, preferred_element_type=F32)
        blocks.append(jnp.where(col <= row + lo, a, 0.0))
    scores = blocks[0] if len(blocks) == 1 else jnp.concatenate(blocks, axis=0)
    o = o + _dot(scores.astype(BF16), vb)
    return o, _gla_state_update(k, v, b, st)


def _mixer_back(h, hb, o_ref, conv, wm_ref, ng_ref, wgo_ref, wco_ref, wo_ref, l1g_ref, l1b_ref):
    g = _dot(hb, wm_ref[:, G0:G0 + GLA_DV])
    parts = []
    for hd in range(HEADS):
        oh = o_ref[:, hd * DV:(hd + 1) * DV]
        ms = jnp.mean(oh * oh, axis=-1, keepdims=True)
        parts.append(oh * lax.rsqrt(ms + RMS_EPS) * ng_ref[...])
    on = jnp.concatenate(parts, axis=1) * (g * _sigmoid(g))
    branch_a = _dot(on.astype(BF16), wgo_ref[...])
    mix = _sigmoid(_dot(hb, wm_ref[:, GA0:GA0 + D_MODEL])) * branch_a
    yc = _dot(hb, wm_ref[:, CB0:CB0 + D_MODEL]) * conv
    branch_b = _dot(yc.astype(BF16), wco_ref[...])
    mix = mix + _sigmoid(_dot(hb, wm_ref[:, GB0:GB0 + D_MODEL])) * branch_b
    mixed = _dot(mix.astype(BF16), wo_ref[...])
    return _layer_norm(ALPHA * h + mixed, l1g_ref[...], l1b_ref[...])


def _conv_input(hb, wm_ref):
    return _dot(hb, wm_ref[:, CC0:CC0 + D_MODEL]) * _dot(hb, wm_ref[:, CH0:CH0 + D_MODEL])


def _gla_rows(qkv_ref, b_ref, rows, hd, st):
    kc = slice(hd * DK, (hd + 1) * DK)
    return _gla_chunk(qkv_ref[rows, Q0 + hd * DK:Q0 + (hd + 1) * DK],
                      qkv_ref[rows, K0 + hd * DK:K0 + (hd + 1) * DK],
                      qkv_ref[rows, V0 + hd * DV:V0 + (hd + 1) * DV],
                      b_ref[rows, kc], st)


def _mixer_kernel(meta_ref, xp_ref, xs_ref, sgin_ref, scin_ref,
                  lng_ref, lnb_ref, wm_ref, wa_ref, wa2_ref, bg_ref,
                  ng_ref, wgo_ref, cw_ref, wco_ref, wo_ref, l1g_ref, l1b_ref,
                  h1_ref, sgp_ref, scp_ref, sgs_ref, scs_ref,
                  st_ref, zbuf_ref, qkv_ref, b_ref, o_ref, *, n_prompt_tiles, seq, streams):
    i = pl.program_id(0)
    back = functools.partial(_mixer_back, wm_ref=wm_ref, ng_ref=ng_ref, wgo_ref=wgo_ref, wco_ref=wco_ref,
                             wo_ref=wo_ref, l1g_ref=l1g_ref, l1b_ref=l1b_ref)

    def front(x):
        h = _layer_norm(x, lng_ref[...], lnb_ref[...])
        hb = h.astype(BF16)
        return h, hb, _gate_log_decay(hb, wa_ref, wa2_ref, bg_ref)

    @pl.when(i == 0)
    def _meta():
        _, hb, gk = front(meta_ref[...])
        k = _dot(hb, wm_ref[:, K0:K0 + GLA_DK])
        v = _dot(hb, wm_ref[:, V0:V0 + GLA_DV])
        b = _chunk_cumsum(gk, N_META)
        zero_state = jnp.zeros((DV, DK), F32)
        for hd in range(HEADS):
            kc = slice(hd * DK, (hd + 1) * DK)
            st_ref[hd] = _gla_state_update(k[:, kc], v[:, hd * DV:(hd + 1) * DV], b[:, kc], zero_state)
        zbuf_ref[0:8, :] = _conv_input(hb, wm_ref)[N_META - 8:N_META, :]

    @pl.when((i >= 1) & (i <= n_prompt_tiles))
    def _prompt_tile():
        h, hb, gk = front(xp_ref[...])
        qkv_ref[...] = _dot(hb, wm_ref[:, Q0:G0])
        b_ref[...] = _chunk_cumsum(gk, PROMPT_CHUNK)
        for c in range(TM // PROMPT_CHUNK):
            rows = slice(c * PROMPT_CHUNK, (c + 1) * PROMPT_CHUNK)
            for hd in range(HEADS):
                o, st_new = _gla_rows(qkv_ref, b_ref, rows, hd, st_ref[hd])
                st_ref[hd] = st_new
                o_ref[rows, hd * DV:(hd + 1) * DV] = o
        z = _conv_input(hb, wm_ref)
        zbuf_ref[8:8 + TM, :] = z
        cw = cw_ref[...]
        conv = cw[0:1, :] * zbuf_ref[6:6 + TM, :] + cw[1:2, :] * zbuf_ref[7:7 + TM, :] + cw[2:3, :] * z
        zbuf_ref[0:8, :] = z[TM - 8:TM, :]
        _rows_to_tiles(h1_ref, back(h, hb, o_ref, conv))

        @pl.when(i == n_prompt_tiles)
        def _prompt_states():
            for hd in range(HEADS):
                sgp_ref[hd] = st_ref[hd].T
            scp_ref[...] = z[TM - 2:TM, :]

    @pl.when(i > n_prompt_tiles)
    def _sample_tile():
        h, hb, gk = front(xs_ref[...])
        qkv_ref[...] = _dot(hb, wm_ref[:, Q0:G0])
        b_ref[...] = _chunk_cumsum(gk, seq)
        z = _conv_input(hb, wm_ref)
        pitch = seq + 8
        cw = cw_ref[...]
        convs = []
        for s in range(streams):
            rows = slice(s * seq, (s + 1) * seq)
            for hd in range(HEADS):
                o, st_new = _gla_rows(qkv_ref, b_ref, rows, hd, sgin_ref[s, hd].T)
                sgs_ref[s, hd] = st_new.T
                o_ref[rows, hd * DV:(hd + 1) * DV] = o
            zs = z[rows, :]
            base = s * pitch
            zbuf_ref[base + 6:base + 8, :] = scin_ref[s]
            zbuf_ref[base + 8:base + 8 + seq, :] = zs
            convs.append(cw[0:1, :] * zbuf_ref[base + 6:base + 6 + seq, :]
                         + cw[1:2, :] * zbuf_ref[base + 7:base + 7 + seq, :] + cw[2:3, :] * zs)
            scs_ref[s] = zs[seq - 2:seq, :]
        _rows_to_tiles(h1_ref, back(h, hb, o_ref, jnp.concatenate(convs, axis=0)))


def _const_spec(shape):
    nd = len(shape)
    return pl.BlockSpec(shape, lambda i, _nd=nd: (0,) * _nd, pipeline_mode=pl.Buffered(1))


def _mixer_weight_specs():
    return [
        _const_spec((1, D_MODEL)), _const_spec((1, D_MODEL)),
        _const_spec((D_MODEL, W_MAIN_COLS)),
        _const_spec((D_MODEL, RANK_PAD)), _const_spec((RANK_PAD, GLA_DK)), _const_spec((1, GLA_DK)),
        _const_spec((1, DV)), _const_spec((GLA_DV, D_MODEL)),
        _const_spec((3, D_MODEL)), _const_spec((D_MODEL, D_MODEL)),
        _const_spec((D_MODEL, D_MODEL)),
        _const_spec((1, D_MODEL)), _const_spec((1, D_MODEL)),
    ]


def _mixer(meta, x_prompt, x_sample, state_gla, state_conv, weights):
    n_streams = state_conv.shape[0]
    seq = x_sample.shape[0] // n_streams
    streams = TM // seq
    npt = x_prompt.shape[0] // TM
    nst = n_streams // streams
    ptile = lambda i: (jnp.clip(i - 1, 0, npt - 1), 0)
    stile = lambda i: jnp.clip(i - 1 - npt, 0, nst - 1)
    kern = functools.partial(_mixer_kernel, n_prompt_tiles=npt, seq=seq, streams=streams)
    return pl.pallas_call(
        kern,
        grid=(1 + npt + nst,),
        in_specs=[_const_spec((N_META, D_MODEL)),
                  pl.BlockSpec((TM, D_MODEL), ptile),
                  pl.BlockSpec((TM, D_MODEL), lambda i: (stile(i), 0), pipeline_mode=pl.Buffered(1)),
                  pl.BlockSpec((streams, HEADS, DK, DV), lambda i: (stile(i), 0, 0, 0),
                               pipeline_mode=pl.Buffered(1)),
                  pl.BlockSpec((streams, 2, D_MODEL), lambda i: (stile(i), 0, 0))] + _mixer_weight_specs(),
        out_specs=[pl.BlockSpec((TM * SUBLANES, LANES), lambda i: (jnp.maximum(i - 1, 0), 0)),
                   pl.BlockSpec((HEADS, DK, DV), lambda i: (0, 0, 0)),
                   pl.BlockSpec((2, D_MODEL), lambda i: (0, 0)),
                   pl.BlockSpec((streams, HEADS, DK, DV), lambda i: (stile(i), 0, 0, 0)),
                   pl.BlockSpec((streams, 2, D_MODEL), lambda i: (stile(i), 0, 0))],
        out_shape=[jax.ShapeDtypeStruct(((x_prompt.shape[0] + x_sample.shape[0]) * SUBLANES, LANES), F32),
                   jax.ShapeDtypeStruct((HEADS, DK, DV), F32),
                   jax.ShapeDtypeStruct((2, D_MODEL), F32),
                   jax.ShapeDtypeStruct(state_gla.shape, F32),
                   jax.ShapeDtypeStruct(state_conv.shape, F32)],
        scratch_shapes=[pltpu.VMEM((HEADS, DV, DK), F32),
                        pltpu.VMEM((max(TM + 8, streams * (seq + 8)), D_MODEL), F32),
                        pltpu.VMEM((TM, G0), F32),
                        pltpu.VMEM((TM, GLA_DK), F32),
                        pltpu.VMEM((TM, GLA_DV), F32)],
        compiler_params=pltpu.CompilerParams(dimension_semantics=("arbitrary",), vmem_limit_bytes=VMEM_LIMIT),
        name="mixer",
    )(meta, x_prompt, x_sample, state_gla, state_conv, *weights)


def _first_index_of_max(vals, ids, sentinel):
    m = jnp.max(vals, axis=0, keepdims=True)
    first = jnp.min(jnp.where(vals == m, ids, sentinel), axis=0, keepdims=True)
    return m, first


def _router_kernel(h_ref, wr_ref, bias_ref, idx_ref, w_ref, rank_ref, counts_ref):
    @pl.when(pl.program_id(0) == 0)
    def _init():
        counts_ref[...] = jnp.zeros_like(counts_ref)

    h = _tiles_to_rows(h_ref, TN_ROUTER)
    hh = h.astype(BF16)
    hl = (h - hh.astype(F32)).astype(BF16)
    wh, wl = wr_ref[0], wr_ref[1]
    dg = functools.partial(lax.dot_general, dimension_numbers=_DN_TB, preferred_element_type=F32)
    logits = dg(wh, hh) + dg(wl, hh) + dg(wh, hl)
    scores = _sigmoid(logits)
    biased = scores + bias_ref[...]
    n_tok = biased.shape[1]
    neg_inf = jnp.float32(-jnp.inf)

    eid = lax.broadcasted_iota(I32, (N_EXPERTS, n_tok), 0).astype(F32)
    lid = lax.broadcasted_iota(I32, (GROUP_SIZE, n_tok), 0).astype(F32)
    group_scores = []
    for g in range(N_GROUPS):
        blk = biased[g * GROUP_SIZE:(g + 1) * GROUP_SIZE]
        m1, first = _first_index_of_max(blk, lid, float(GROUP_SIZE))
        m2 = jnp.max(jnp.where(lid == first, neg_inf, blk), axis=0, keepdims=True)
        group_scores.append(m1 + m2)
    gsc = jnp.concatenate(group_scores, axis=0)

    gid = lax.broadcasted_iota(I32, (N_GROUPS, n_tok), 0).astype(F32)
    keep = jnp.zeros((N_GROUPS, n_tok), F32)
    cur = gsc
    for _ in range(TOPK_GROUPS):
        _, first = _first_index_of_max(cur, gid, float(N_GROUPS))
        sel = gid == first
        keep = jnp.where(sel, 1.0, keep)
        cur = jnp.where(sel, neg_inf, cur)

    masked = jnp.concatenate(
        [jnp.where(keep[g:g + 1] > 0.5, biased[g * GROUP_SIZE:(g + 1) * GROUP_SIZE], neg_inf)
         for g in range(N_GROUPS)], axis=0)
    idxs, wts, sels = [], [], []
    cur = masked
    for _ in range(TOP_K):
        _, first = _first_index_of_max(cur, eid, float(N_EXPERTS))
        sel = eid == first
        idxs.append(first)
        sels.append(sel)
        wts.append(jnp.sum(jnp.where(sel, scores, 0.0), axis=0, keepdims=True))
        cur = jnp.where(sel, neg_inf, cur)
    w = jnp.concatenate(wts, axis=0)
    idx_ref[...] = jnp.concatenate(idxs, axis=0).astype(I32)
    w_ref[...] = w / jnp.sum(w, axis=0, keepdims=True) * ROUTED_SCALE

    chosen = jnp.where(sels[0], 1.0, 0.0)
    for sel in sels[1:]:
        chosen = chosen + jnp.where(sel, 1.0, 0.0)
    t_src = lax.broadcasted_iota(I32, (n_tok, n_tok), 0)
    t_dst = lax.broadcasted_iota(I32, (n_tok, n_tok), 1)
    earlier = jnp.where(t_src < t_dst, 1.0, 0.0).astype(BF16)
    arrivals = counts_ref[...] + _dot(chosen.astype(BF16), earlier)
    rank_ref[...] = jnp.concatenate(
        [jnp.sum(jnp.where(sel, arrivals, 0.0), axis=0, keepdims=True) for sel in sels], axis=0).astype(I32)
    counts_ref[...] += jnp.sum(chosen, axis=1, keepdims=True)


def _router(h1_tiles, wr_split, bias_col):
    n = h1_tiles.shape[0] // SUBLANES
    tok = pl.BlockSpec((TOP_K, TN_ROUTER), lambda i: (0, i))
    return pl.pallas_call(
        _router_kernel,
        grid=(n // TN_ROUTER,),
        in_specs=[pl.BlockSpec((TN_ROUTER * SUBLANES, LANES), lambda i: (i, 0)),
                  pl.BlockSpec((2, N_EXPERTS, D_MODEL), lambda i: (0, 0, 0)),
                  pl.BlockSpec((N_EXPERTS, 1), lambda i: (0, 0))],
        out_specs=[tok, tok, tok, pl.BlockSpec((N_EXPERTS, 1), lambda i: (0, 0))],
        out_shape=[jax.ShapeDtypeStruct((TOP_K, n), I32), jax.ShapeDtypeStruct((TOP_K, n), F32),
                   jax.ShapeDtypeStruct((TOP_K, n), I32), jax.ShapeDtypeStruct((N_EXPERTS, 1), F32)],
        compiler_params=pltpu.CompilerParams(dimension_semantics=("arbitrary",), vmem_limit_bytes=VMEM_LIMIT),
        name="router",
    )(h1_tiles, wr_split, bias_col)


def _positions_kernel(idx_ref, rank_ref, seg_ref, pos_ref):
    n_tok = idx_ref.shape[1]
    eid = lax.broadcasted_iota(I32, (N_EXPERTS, n_tok), 0)
    seg = seg_ref[...]
    rows = []
    for k in range(TOP_K):
        onehot = eid == idx_ref[k:k + 1, :]
        rows.append(jnp.sum(jnp.where(onehot, seg, 0.0), axis=0, keepdims=True))
    pos_ref[...] = jnp.concatenate(rows, axis=0).astype(I32) + rank_ref[...]


def _positions(idx_t, rank_t, seg_start_col):
    n = idx_t.shape[1]
    tok = pl.BlockSpec((TOP_K, TN_ROUTER), lambda i: (0, i))
    return pl.pallas_call(
        _positions_kernel,
        grid=(n // TN_ROUTER,),
        in_specs=[tok, tok, pl.BlockSpec((N_EXPERTS, 1), lambda i: (0, 0))],
        out_specs=tok,
        out_shape=jax.ShapeDtypeStruct((TOP_K, n), I32),
        compiler_params=pltpu.CompilerParams(dimension_semantics=("arbitrary",)),
        name="positions",
    )(idx_t, rank_t, seg_start_col)


def _tile_copy(src, src_tile, dst, dst_tile, sem):
    return pltpu.make_async_copy(src.at[pl.ds(pl.multiple_of(src_tile * SUBLANES, SUBLANES), SUBLANES)],
                                 dst.at[pl.ds(pl.multiple_of(dst_tile * SUBLANES, SUBLANES), SUBLANES)], sem)


def _chunk_rows(first_tile):
    return pl.ds(pl.multiple_of(first_tile * SUBLANES, SUBLANES), CH * SUBLANES)


def _dispatch_kernel(start_ref, cnt_ref, gtot_ref, pos_ref, h_ref, x_hbm, zeros, sem, zsem, *, g_max):
    @pl.when(pl.program_id(0) == 0)
    def _zero_fill():
        zeros[...] = jnp.zeros_like(zeros)
        fill = lambda first_tile: pltpu.make_async_copy(zeros, x_hbm.at[_chunk_rows(first_tile)], zsem)
        partial_chunk = lambda e: lax.rem(cnt_ref[e], CH) != 0

        def pad_start(e, carry):
            @pl.when(partial_chunk(e))
            def _():
                fill(start_ref[e] + cnt_ref[e] - lax.rem(cnt_ref[e], CH)).start()
            return carry

        def pad_wait(e, carry):
            @pl.when(partial_chunk(e))
            def _():
                fill(0).wait()
            return carry

        def tail_start(g, carry):
            fill(g * CH).start()
            return carry

        def tail_wait(g, carry):
            fill(0).wait()
            return carry

        lax.fori_loop(0, N_EXPERTS, pad_start, 0)
        lax.fori_loop(gtot_ref[0], g_max, tail_start, 0)
        lax.fori_loop(0, N_EXPERTS, pad_wait, 0)
        lax.fori_loop(gtot_ref[0], g_max, tail_wait, 0)

    def body(t, carry):
        for k in range(TOP_K):
            _tile_copy(h_ref, t, x_hbm, pos_ref[0, 0, k * TN_DISPATCH + t], sem.at[k]).start(priority=k % 2)
        return carry
    lax.fori_loop(0, TN_DISPATCH, body, 0, unroll=2)
    for k in range(TOP_K):
        pltpu.make_async_copy(h_ref, x_hbm.at[pl.ds(0, TN_DISPATCH * SUBLANES)], sem.at[k]).wait()


def _dispatch(seg_start, counts, g_total, pos_tiles, h1_tiles, g_max):
    grid_spec = pltpu.PrefetchScalarGridSpec(
        num_scalar_prefetch=3,
        grid=(pos_tiles.shape[0],),
        in_specs=[pl.BlockSpec((1, 1, TOP_K * TN_DISPATCH), lambda t, *_: (t, 0, 0), memory_space=pltpu.SMEM),
                  pl.BlockSpec((TN_DISPATCH * SUBLANES, LANES), lambda t, *_: (t, 0))],
        out_specs=pl.BlockSpec(memory_space=pl.ANY),
        scratch_shapes=[pltpu.VMEM((CH * SUBLANES, LANES), F32), pltpu.SemaphoreType.DMA((TOP_K,)),
                        pltpu.SemaphoreType.DMA(())],
    )
    return pl.pallas_call(
        functools.partial(_dispatch_kernel, g_max=g_max),
        grid_spec=grid_spec,
        out_shape=jax.ShapeDtypeStruct((g_max * CH * SUBLANES, LANES), F32),
        compiler_params=pltpu.CompilerParams(dimension_semantics=("arbitrary",), has_side_effects=True),
        name="dispatch",
    )(seg_start, counts, g_total, pos_tiles, h1_tiles)


def _expert_kernel(start_ref, cnt_ref, seq_ref, nxt_ref, gtot_ref, x_hbm, wg_hbm, wu_hbm, wd_hbm, y_hbm,
                   xbuf, ybuf, wg32, wu32, wd32, wgb, wub, wdb, xsem, ysem, wsem, *, g_max):
    e = pl.program_id(0)
    cnt = cnt_ref[e]
    g_total = gtot_ref[0]

    def x_copy(g, slot):
        return pltpu.make_async_copy(x_hbm.at[_chunk_rows(g * CH)], xbuf.at[slot], xsem.at[slot])

    def y_copy(g, slot):
        return pltpu.make_async_copy(ybuf.at[slot], y_hbm.at[_chunk_rows(g * CH)], ysem.at[slot])

    def weight_copies(ex, slot):
        return (pltpu.make_async_copy(wg_hbm.at[ex], wg32.at[slot], wsem.at[slot]),
                pltpu.make_async_copy(wu_hbm.at[ex], wu32.at[slot], wsem.at[slot]),
                pltpu.make_async_copy(wd_hbm.at[ex], wd32.at[slot], wsem.at[slot]))

    @pl.when(e == 0)
    def _prime_x_ring():
        for g in range(NX - 1):
            @pl.when(g < g_total)
            def _():
                x_copy(g, g).start()

    @pl.when(cnt > 0)
    def _expert():
        seq, nxt = seq_ref[e], nxt_ref[e]
        wslot = lax.rem(seq, 2)

        @pl.when(seq == 0)
        def _fetch_own():
            for c in weight_copies(e, wslot):
                c.start(priority=1)

        @pl.when(nxt >= 0)
        def _prefetch_weights():
            for c in weight_copies(nxt, 1 - wslot):
                c.start(priority=1)

        for c in weight_copies(e, wslot):
            c.wait()
        wgb[...] = wg32[wslot].astype(BF16)
        wub[...] = wu32[wslot].astype(BF16)
        wdb[...] = wd32[wslot].astype(BF16)

        g_first = start_ref[e] // CH

        def chunk(j, carry):
            g = g_first + j
            xs = lax.rem(g, NX)
            x_copy(0, xs).wait()
            ahead = g + (NX - 1)

            @pl.when(ahead < g_total)
            def _fetch_ahead():
                x_copy(ahead, lax.rem(ahead, NX)).start()

            xb = _tiles_to_rows(xbuf.at[xs], CH).astype(BF16)
            gate = _dot(xb, wgb[...])
            up = _dot(xb, wub[...])
            res = _dot(((gate * _sigmoid(gate)) * up).astype(BF16), wdb[...])
            ys = lax.rem(g, NY)

            @pl.when(g >= NY)
            def _slot_written_back():
                y_copy(0, ys).wait()

            _rows_to_tiles(ybuf.at[ys], res)
            y_copy(g, ys).start()
            return carry

        lax.fori_loop(0, (cnt + CH - 1) // CH, chunk, 0)

    @pl.when(e == pl.num_programs(0) - 1)
    def _finish():
        for back in range(1, NY + 1):
            @pl.when(g_total - back >= 0)
            def _():
                y_copy(0, lax.rem(g_total - back, NY)).wait()

        ybuf[0] = jnp.zeros(ybuf.shape[1:], F32)

        def tail_start(g, carry):
            y_copy(g, 0).start()
            return carry

        def tail_wait(g, carry):
            y_copy(0, 0).wait()
            return carry

        lax.fori_loop(g_total, g_max, tail_start, 0)
        lax.fori_loop(g_total, g_max, tail_wait, 0)


def _experts(seg_start, counts, g_total, x_sorted, w_gate, w_up, w_down, g_max):
    has_rows = counts > 0
    seq = jnp.where(has_rows, jnp.cumsum(has_rows.astype(I32)) - 1, -1).astype(I32)
    eid = jnp.arange(N_EXPERTS, dtype=I32)
    nxt = jnp.min(jnp.where((eid[None, :] > eid[:, None]) & has_rows[None, :], eid[None, :], N_EXPERTS), axis=1)
    nxt = jnp.where(nxt == N_EXPERTS, -1, nxt).astype(I32)
    hbm = pl.BlockSpec(memory_space=pl.ANY)
    grid_spec = pltpu.PrefetchScalarGridSpec(
        num_scalar_prefetch=5,
        grid=(N_EXPERTS,),
        in_specs=[hbm, hbm, hbm, hbm],
        out_specs=hbm,
        scratch_shapes=[pltpu.VMEM((NX, CH * SUBLANES, LANES), F32),
                        pltpu.VMEM((NY, CH * SUBLANES, LANES), F32),
                        pltpu.VMEM((2, D_MODEL, EXPERT_HIDDEN), F32),
                        pltpu.VMEM((2, D_MODEL, EXPERT_HIDDEN), F32),
                        pltpu.VMEM((2, EXPERT_HIDDEN, D_MODEL), F32),
                        pltpu.VMEM((D_MODEL, EXPERT_HIDDEN), BF16),
                        pltpu.VMEM((D_MODEL, EXPERT_HIDDEN), BF16),
                        pltpu.VMEM((EXPERT_HIDDEN, D_MODEL), BF16),
                        pltpu.SemaphoreType.DMA((NX,)), pltpu.SemaphoreType.DMA((NY,)),
                        pltpu.SemaphoreType.DMA((2,))],
    )
    return pl.pallas_call(
        functools.partial(_expert_kernel, g_max=g_max),
        grid_spec=grid_spec,
        out_shape=jax.ShapeDtypeStruct(x_sorted.shape, F32),
        compiler_params=pltpu.CompilerParams(dimension_semantics=("arbitrary",), vmem_limit_bytes=VMEM_LIMIT,
                                             has_side_effects=True),
        name="experts",
    )(seg_start, counts, seq, nxt, g_total, x_sorted, w_gate, w_up, w_down)


def _combine_kernel(pos_cur_ref, pos_nxt_ref, y_hbm, h_ref, w_ref, wsg_ref, wsu_ref, wsd_ref, l2g_ref, l2b_ref,
                    outp_ref, outs_ref, ybuf, sem, *, n_prompt_tiles):
    t = pl.program_id(0)
    n_t = pl.num_programs(0)
    slot = t % 2

    def gather_start(pos_ref, dst, dst_sem):
        def body(r, carry):
            for k in range(TOP_K):
                j = k * TN_COMBINE + r
                _tile_copy(y_hbm, pos_ref[0, 0, j], dst, j, dst_sem).start(priority=k % 2)
            return carry
        lax.fori_loop(0, TN_COMBINE, body, 0, unroll=2)

    @pl.when(t == 0)
    def _first():
        gather_start(pos_cur_ref, ybuf.at[0], sem.at[0])

    @pl.when(t + 1 < n_t)
    def _prefetch():
        gather_start(pos_nxt_ref, ybuf.at[1 - slot], sem.at[1 - slot])

    h = _tiles_to_rows(h_ref, TN_COMBINE)
    hb = h.astype(BF16)
    g = _dot(hb, wsg_ref[...])
    u = _dot(hb, wsu_ref[...])
    shared = _dot(((g * _sigmoid(g)) * u).astype(BF16), wsd_ref[...])

    yslot = ybuf.at[slot]
    pltpu.make_async_copy(y_hbm.at[pl.ds(0, TOP_K * TN_COMBINE * SUBLANES)], yslot, sem.at[slot]).wait()
    w = w_ref[...]
    wk = [jnp.broadcast_to(w[:, k:k + 1], (TN_COMBINE, LANES)) for k in range(TOP_K)]
    chunks = []
    for s in range(SUBLANES):
        acc = wk[0] * yslot[pl.ds(s, TN_COMBINE, stride=SUBLANES), :]
        for k in range(1, TOP_K):
            acc = acc + wk[k] * yslot[pl.ds(k * TN_COMBINE * SUBLANES + s, TN_COMBINE, stride=SUBLANES), :]
        chunks.append(acc)
    routed = jnp.concatenate(chunks, axis=1)
    out = _layer_norm(ALPHA * h + (routed + shared), l2g_ref[...], l2b_ref[...])

    @pl.when(t < n_prompt_tiles)
    def _prompt_rows():
        outp_ref[...] = out

    @pl.when(t >= n_prompt_tiles)
    def _sample_rows():
        outs_ref[...] = out


def _combine(pos_tiles, y_sorted, h1_tiles, w_tok, wsg, wsu, wsd, l2g, l2b, n_prompt):
    n = h1_tiles.shape[0] // SUBLANES
    n_t = n // TN_COMBINE
    npt = n_prompt // TN_COMBINE
    n_idx = TOP_K * TN_COMBINE
    cspec = lambda shape: pl.BlockSpec(shape, lambda t: (0,) * len(shape))
    return pl.pallas_call(
        functools.partial(_combine_kernel, n_prompt_tiles=npt),
        grid=(n_t,),
        in_specs=[pl.BlockSpec((1, 1, n_idx), lambda t: (t, 0, 0), memory_space=pltpu.SMEM),
                  pl.BlockSpec((1, 1, n_idx), lambda t: (jnp.minimum(t + 1, n_t - 1), 0, 0),
                               memory_space=pltpu.SMEM),
                  pl.BlockSpec(memory_space=pl.ANY),
                  pl.BlockSpec((TN_COMBINE * SUBLANES, LANES), lambda t: (t, 0)),
                  pl.BlockSpec((TN_COMBINE, TOP_K), lambda t: (t, 0)),
                  cspec((D_MODEL, SHARED_HIDDEN)), cspec((D_MODEL, SHARED_HIDDEN)), cspec((SHARED_HIDDEN, D_MODEL)),
                  cspec((1, D_MODEL)), cspec((1, D_MODEL))],
        out_specs=[pl.BlockSpec((TN_COMBINE, D_MODEL), lambda t: (jnp.minimum(t, npt - 1), 0)),
                   pl.BlockSpec((TN_COMBINE, D_MODEL), lambda t: (jnp.maximum(t - npt, 0), 0))],
        out_shape=[jax.ShapeDtypeStruct((n_prompt, D_MODEL), F32),
                   jax.ShapeDtypeStruct((n - n_prompt, D_MODEL), F32)],
        scratch_shapes=[pltpu.VMEM((2, n_idx * SUBLANES, LANES), F32), pltpu.SemaphoreType.DMA((2,))],
        compiler_params=pltpu.CompilerParams(dimension_semantics=("arbitrary",), vmem_limit_bytes=VMEM_LIMIT),
        name="combine",
    )(pos_tiles, pos_tiles, y_sorted, h1_tiles, w_tok, wsg, wsu, wsd, l2g, l2b)


def _pack_mixer_weights(ln_emb_g, ln_emb_b, w_in, w_gate_a2, b_gate, gla_norm_g, w_gla_out, conv_w,
                        w_conv_out, w_o, ln1_g, ln1_b):
    q, k, v, g, a, cb, cc, ch, ga, gb = jnp.split(
        w_in, [512, 1024, 2048, 3072, 3088, 4112, 5136, 6160, 7184], axis=-1)
    w_main = jnp.concatenate([q, k, v, g, cb, cc, ch, ga, gb], axis=-1).astype(BF16)
    w_a = jnp.pad(a, ((0, 0), (0, RANK_PAD - GATE_RANK))).astype(BF16)
    w_a2 = jnp.pad(w_gate_a2, ((0, RANK_PAD - GATE_RANK), (0, 0))).astype(BF16)
    row = lambda x: x.reshape(1, -1).astype(F32)
    return [row(ln_emb_g), row(ln_emb_b), w_main, w_a, w_a2, row(b_gate), row(gla_norm_g),
            w_gla_out.astype(BF16), conv_w.astype(F32), w_conv_out.astype(BF16), w_o.astype(BF16),
            row(ln1_g), row(ln1_b)]


def kernel(x_prompt, x_sample, state_gla, state_conv, meta_tokens, ln_emb_g, ln_emb_b, w_in, w_gate_a2, b_gate, gla_norm_g, w_gla_out, conv_w, w_conv_out, w_o, ln1_g, ln1_b, w_router, router_bias, w_exp_gate, w_exp_up, w_exp_down, w_sh_gate, w_sh_up, w_sh_down, ln2_g, ln2_b):
    batch, seq, _ = x_prompt.shape
    dec_batch, dec_seq, _ = x_sample.shape
    depth = w_in.shape[0]
    assert batch == 1 and depth == 1 and seq % TM == 0 and TM % dec_seq == 0 and dec_seq % SUB == 0
    n_prompt, n_sample = batch * seq, dec_batch * dec_seq
    n_total = n_prompt + n_sample
    assert n_sample % TM == 0 and n_total % TN_ROUTER == 0
    assert n_prompt % TN_COMBINE == 0 and n_sample % TN_COMBINE == 0 and n_total % TN_DISPATCH == 0
    n_assign = n_total * TOP_K

    weights = _pack_mixer_weights(ln_emb_g, ln_emb_b, w_in[0], w_gate_a2[0], b_gate[0], gla_norm_g[0],
                                  w_gla_out[0], conv_w[0], w_conv_out[0], w_o[0], ln1_g[0], ln1_b[0])
    h1, sg_p, sc_p, sg_s, sc_s = _mixer(meta_tokens.astype(F32), x_prompt.reshape(n_prompt, D_MODEL),
                                        x_sample.reshape(n_sample, D_MODEL), state_gla[0], state_conv[0], weights)

    wr_t = w_router[0].T
    wr_hi = wr_t.astype(BF16)
    wr_split = jnp.stack([wr_hi, (wr_t - wr_hi.astype(F32)).astype(BF16)])
    idx_t, w_t, rank_t, counts = _router(h1, wr_split, router_bias[0].reshape(N_EXPERTS, 1).astype(F32))

    counts = counts.reshape(N_EXPERTS).astype(I32)
    padded = (counts + CH - 1) // CH * CH
    seg_end = jnp.cumsum(padded).astype(I32)
    seg_start = seg_end - padded
    g_total = (seg_end[-1:] // CH).astype(I32)
    g_max = n_assign // CH + N_EXPERTS
    pos_t = _positions(idx_t, rank_t, seg_start.astype(F32).reshape(N_EXPERTS, 1))
    pos_tiles = lambda tn: pos_t.reshape(TOP_K, n_total // tn, tn).transpose(1, 0, 2).reshape(n_total // tn, 1, -1)

    x_sorted = _dispatch(seg_start, counts, g_total, pos_tiles(TN_DISPATCH), h1, g_max)
    y_sorted = _experts(seg_start, counts, g_total, x_sorted, w_exp_gate[0], w_exp_up[0], w_exp_down[0], g_max)
    y_prompt, y_sample = _combine(pos_tiles(TN_COMBINE), y_sorted, h1, w_t.T,
                                  w_sh_gate[0].astype(BF16), w_sh_up[0].astype(BF16), w_sh_down[0].astype(BF16),
                                  ln2_g[0].reshape(1, -1), ln2_b[0].reshape(1, -1), n_prompt)
    y_prompt = y_prompt.reshape(batch, seq, D_MODEL)
    y_sample = y_sample.reshape(dec_batch, dec_seq, D_MODEL)
    return (y_prompt, y_sample,
            sg_p.reshape(depth, batch, HEADS, DK, DV), sc_p.reshape(depth, batch, 2, D_MODEL),
            sg_s.reshape(depth, dec_batch, HEADS, DK, DV), sc_s.reshape(depth, dec_batch, 2, D_MODEL))
```

```python
import functools

import jax
import jax.numpy as jnp
from jax import lax
from jax.experimental import pallas as pl
from jax.experimental.pallas import tpu as pltpu

F32 = jnp.float32
BF16 = jnp.bfloat16
I32 = jnp.int32

D_MODEL = 1024
N_META = 16
HEADS = 4
DK = 128
DV = 256
GLA_DK = HEADS * DK
GLA_DV = HEADS * DV
GATE_RANK = 16
GATE_TAU = 16.0
N_EXPERTS = 256
TOP_K = 8
N_GROUPS = 8
GROUP_SIZE = N_EXPERTS // N_GROUPS
TOPK_GROUPS = 4
EXPERT_HIDDEN = 256
SHARED_HIDDEN = 256
ROUTED_SCALE = 2.5
LN_EPS = 1e-5
RMS_EPS = 1e-6
ALPHA = 2.0 ** 0.25
QSCALE = DK ** -0.5

Q0, K0, V0, G0, CB0, CC0, CH0, GA0, GB0 = 0, 512, 1024, 2048, 3072, 4096, 5120, 6144, 7168
W_MAIN_COLS = 8192
RANK_PAD = 128

SUB = 16
PROMPT_CHUNK = 64
TM = 256
TN_ROUTER = 512
TN_DISPATCH = 256
CH = 256
NX = 3
NY = 2
TN_COMBINE = 256
VMEM_LIMIT = 60 * 1024 * 1024
SUBLANES = 8
LANES = 128
assert D_MODEL == SUBLANES * LANES

_DN_TB = (((1,), (1,)), ((), ()))
_DN_TA = (((0,), (0,)), ((), ()))


def _dot(a, b):
    return jnp.dot(a, b, preferred_element_type=F32)


def _tiles_to_rows(ref, n, base=0):
    return jnp.concatenate(
        [ref[pl.ds(SUBLANES * base + s, n, stride=SUBLANES), :] for s in range(SUBLANES)], axis=1)


def _rows_to_tiles(ref, val):
    n = val.shape[0]
    for s in range(SUBLANES):
        ref[pl.ds(s, n, stride=SUBLANES), :] = val[:, s * LANES:(s + 1) * LANES]


def _layer_norm(x, g, b):
    mu = jnp.mean(x, axis=-1, keepdims=True)
    xc = x - mu
    var = jnp.mean(xc * xc, axis=-1, keepdims=True)
    return xc * lax.rsqrt(var + LN_EPS) * g + b


def _sigmoid(x):
    return 1.0 / (1.0 + jnp.exp(-x))


def _log_sigmoid(x):
    return jnp.minimum(x, 0.0) - jnp.log1p(jnp.exp(-jnp.abs(x)))


def _chunk_cumsum(g, chunk):
    t = g.shape[0]
    shift = chunk.bit_length() - 1
    r = lax.broadcasted_iota(I32, (t, t), 0)
    c = lax.broadcasted_iota(I32, (t, t), 1)
    tri = jnp.where(((r >> shift) == (c >> shift)) & (c <= r), 1.0, 0.0).astype(BF16)
    g1 = g.astype(BF16)
    r1 = g - g1.astype(F32)
    g2 = r1.astype(BF16)
    g3 = (r1 - g2.astype(F32)).astype(BF16)
    return _dot(tri, g1) + _dot(tri, g2) + _dot(tri, g3)


def _gate_log_decay(hb, wa_ref, wa2_ref, bg_ref):
    a = _dot(hb, wa_ref[...])
    return _log_sigmoid(_dot(a.astype(BF16), wa2_ref[...]) + bg_ref[...]) * (1.0 / GATE_TAU)


def _gla_state_update(k, v, b, st):
    n = k.shape[0]
    b_last = b[n - 1:n, :]
    khat = (k * jnp.exp(b_last - b)).astype(BF16)
    return st * jnp.exp(b_last) + lax.dot_general(v.astype(BF16), khat, _DN_TA, preferred_element_type=F32)


def _gla_chunk(q, k, v, b, st):
    n = q.shape[0]
    vb = v.astype(BF16)
    q0 = (q * (jnp.exp(b) * QSCALE)).astype(BF16)
    o = lax.dot_general(q0, st.astype(BF16), _DN_TB, preferred_element_type=F32)
    col = lax.broadcasted_iota(I32, (SUB, n), 1)
    row = lax.broadcasted_iota(I32, (SUB, n), 0)
    blocks = []
    for i in range(n // SUB):
        lo, hi = SUB * i, SUB * (i + 1)
        bref = b[lo:lo + 1, :]
        qi = (q[lo:hi] * (jnp.exp(b[lo:hi] - bref) * QSCALE)).astype(BF16)
        kk = (k[:hi] * jnp.exp(bref - b[:hi])).astype(BF16)
        if hi < n:
            kk = jnp.concatenate([kk, jnp.zeros((n - hi, DK), BF16)], axis=0)
        a = lax.dot_general(qi, kk, _DN_TB, preferred_element_type=F32)
        blocks.append(jnp.where(col <= row + lo, a, 0.0))
    scores = blocks[0] if len(blocks) == 1 else jnp.concatenate(blocks, axis=0)
    o = o + _dot(scores.astype(BF16), vb)
    return o, _gla_state_update(k, v, b, st)


def _mixer_back(h, hb, o_ref, conv, wm_ref, ng_ref, wgo_ref, wco_ref, wo_ref, l1g_ref, l1b_ref):
    g = _dot(hb, wm_ref[:, G0:G0 + GLA_DV])
    parts = []
    for hd in range(HEADS):
        oh = o_ref[:, hd * DV:(hd + 1) * DV]
        ms = jnp.mean(oh * oh, axis=-1, keepdims=True)
        parts.append(oh * lax.rsqrt(ms + RMS_EPS) * ng_ref[...])
    on = jnp.concatenate(parts, axis=1) * (g * _sigmoid(g))
    branch_a = _dot(on.astype(BF16), wgo_ref[...])
    mix = _sigmoid(_dot(hb, wm_ref[:, GA0:GA0 + D_MODEL])) * branch_a
    yc = _dot(hb, wm_ref[:, CB0:CB0 + D_MODEL]) * conv
    branch_b = _dot(yc.astype(BF16), wco_ref[...])
    mix = mix + _sigmoid(_dot(hb, wm_ref[:, GB0:GB0 + D_MODEL])) * branch_b
    mixed = _dot(mix.astype(BF16), wo_ref[...])
    return _layer_norm(ALPHA * h + mixed, l1g_ref[...], l1b_ref[...])


def _conv_input(hb, wm_ref):
    return _dot(hb, wm_ref[:, CC0:CC0 + D_MODEL]) * _dot(hb, wm_ref[:, CH0:CH0 + D_MODEL])


def _gla_rows(qkv_ref, b_ref, rows, hd, st):
    kc = slice(hd * DK, (hd + 1) * DK)
    return _gla_chunk(qkv_ref[rows, Q0 + hd * DK:Q0 + (hd + 1) * DK],
                      qkv_ref[rows, K0 + hd * DK:K0 + (hd + 1) * DK],
                      qkv_ref[rows, V0 + hd * DV:V0 + (hd + 1) * DV],
                      b_ref[rows, kc], st)


def _mixer_kernel(meta_ref, xp_ref, xs_ref, sgin_ref, scin_ref,
                  lng_ref, lnb_ref, wm_ref, wa_ref, wa2_ref, bg_ref,
                  ng_ref, wgo_ref, cw_ref, wco_ref, wo_ref, l1g_ref, l1b_ref,
                  h1_ref, sgp_ref, scp_ref, sgs_ref, scs_ref,
                  st_ref, zbuf_ref, qkv_ref, b_ref, o_ref, *, n_prompt_tiles, seq, streams):
    i = pl.program_id(0)
    back = functools.partial(_mixer_back, wm_ref=wm_ref, ng_ref=ng_ref, wgo_ref=wgo_ref, wco_ref=wco_ref,
                             wo_ref=wo_ref, l1g_ref=l1g_ref, l1b_ref=l1b_ref)

    def front(x):
        h = _layer_norm(x, lng_ref[...], lnb_ref[...])
        hb = h.astype(BF16)
        return h, hb, _gate_log_decay(hb, wa_ref, wa2_ref, bg_ref)

    @pl.when(i == 0)
    def _meta():
        _, hb, gk = front(meta_ref[...])
        k = _dot(hb, wm_ref[:, K0:K0 + GLA_DK])
        v = _dot(hb, wm_ref[:, V0:V0 + GLA_DV])
        b = _chunk_cumsum(gk, N_META)
        zero_state = jnp.zeros((DV, DK), F32)
        for hd in range(HEADS):
            kc = slice(hd * DK, (hd + 1) * DK)
            st_ref[hd] = _gla_state_update(k[:, kc], v[:, hd * DV:(hd + 1) * DV], b[:, kc], zero_state)
        zbuf_ref[0:8, :] = _conv_input(hb, wm_ref)[N_META - 8:N_META, :]

    @pl.when((i >= 1) & (i <= n_prompt_tiles))
    def _prompt_tile():
        h, hb, gk = front(xp_ref[...])
        qkv_ref[...] = _dot(hb, wm_ref[:, Q0:G0])
        b_ref[...] = _chunk_cumsum(gk, PROMPT_CHUNK)
        for c in range(TM // PROMPT_CHUNK):
            rows = slice(c * PROMPT_CHUNK, (c + 1) * PROMPT_CHUNK)
            for hd in range(HEADS):
                o, st_new = _gla_rows(qkv_ref, b_ref, rows, hd, st_ref[hd])
                st_ref[hd] = st_new
                o_ref[rows, hd * DV:(hd + 1) * DV] = o
        z = _conv_input(hb, wm_ref)
        zbuf_ref[8:8 + TM, :] = z
        cw = cw_ref[...]
        conv = cw[0:1, :] * zbuf_ref[6:6 + TM, :] + cw[1:2, :] * zbuf_ref[7:7 + TM, :] + cw[2:3, :] * z
        zbuf_ref[0:8, :] = z[TM - 8:TM, :]
        _rows_to_tiles(h1_ref, back(h, hb, o_ref, conv))

        @pl.when(i == n_prompt_tiles)
        def _prompt_states():
            for hd in range(HEADS):
                sgp_ref[hd] = st_ref[hd].T
            scp_ref[...] = z[TM - 2:TM, :]

    @pl.when(i > n_prompt_tiles)
    def _sample_tile():
        h, hb, gk = front(xs_ref[...])
        qkv_ref[...] = _dot(hb, wm_ref[:, Q0:G0])
        b_ref[...] = _chunk_cumsum(gk, seq)
        z = _conv_input(hb, wm_ref)
        pitch = seq + 8
        cw = cw_ref[...]
        convs = []
        for s in range(streams):
            rows = slice(s * seq, (s + 1) * seq)
            for hd in range(HEADS):
                o, st_new = _gla_rows(qkv_ref, b_ref, rows, hd, sgin_ref[s, hd].T)
                sgs_ref[s, hd] = st_new.T
                o_ref[rows, hd * DV:(hd + 1) * DV] = o
            zs = z[rows, :]
            base = s * pitch
            zbuf_ref[base + 6:base + 8, :] = scin_ref[s]
            zbuf_ref[base + 8:base + 8 + seq, :] = zs
            convs.append(cw[0:1, :] * zbuf_ref[base + 6:base + 6 + seq, :]
                         + cw[1:2, :] * zbuf_ref[base + 7:base + 7 + seq, :] + cw[2:3, :] * zs)
            scs_ref[s] = zs[seq - 2:seq, :]
        _rows_to_tiles(h1_ref, back(h, hb, o_ref, jnp.concatenate(convs, axis=0)))


def _const_spec(shape):
    nd = len(shape)
    return pl.BlockSpec(shape, lambda i, _nd=nd: (0,) * _nd, pipeline_mode=pl.Buffered(1))


def _mixer_weight_specs():
    return [
        _const_spec((1, D_MODEL)), _const_spec((1, D_MODEL)),
        _const_spec((D_MODEL, W_MAIN_COLS)),
        _const_spec((D_MODEL, RANK_PAD)), _const_spec((RANK_PAD, GLA_DK)), _const_spec((1, GLA_DK)),
        _const_spec((1, DV)), _const_spec((GLA_DV, D_MODEL)),
        _const_spec((3, D_MODEL)), _const_spec((D_MODEL, D_MODEL)),
        _const_spec((D_MODEL, D_MODEL)),
        _const_spec((1, D_MODEL)), _const_spec((1, D_MODEL)),
    ]


def _mixer(meta, x_prompt, x_sample, state_gla, state_conv, weights):
    n_streams = state_conv.shape[0]
    seq = x_sample.shape[0] // n_streams
    streams = TM // seq
    npt = x_prompt.shape[0] // TM
    nst = n_streams // streams
    ptile = lambda i: (jnp.clip(i - 1, 0, npt - 1), 0)
    stile = lambda i: jnp.clip(i - 1 - npt, 0, nst - 1)
    kern = functools.partial(_mixer_kernel, n_prompt_tiles=npt, seq=seq, streams=streams)
    return pl.pallas_call(
        kern,
        grid=(1 + npt + nst,),
        in_specs=[_const_spec((N_META, D_MODEL)),
                  pl.BlockSpec((TM, D_MODEL), ptile),
                  pl.BlockSpec((TM, D_MODEL), lambda i: (stile(i), 0), pipeline_mode=pl.Buffered(1)),
                  pl.BlockSpec((streams, HEADS, DK, DV), lambda i: (stile(i), 0, 0, 0),
                               pipeline_mode=pl.Buffered(1)),
                  pl.BlockSpec((streams, 2, D_MODEL), lambda i: (stile(i), 0, 0))] + _mixer_weight_specs(),
        out_specs=[pl.BlockSpec((TM * SUBLANES, LANES), lambda i: (jnp.maximum(i - 1, 0), 0)),
                   pl.BlockSpec((HEADS, DK, DV), lambda i: (0, 0, 0)),
                   pl.BlockSpec((2, D_MODEL), lambda i: (0, 0)),
                   pl.BlockSpec((streams, HEADS, DK, DV), lambda i: (stile(i), 0, 0, 0)),
                   pl.BlockSpec((streams, 2, D_MODEL), lambda i: (stile(i), 0, 0))],
        out_shape=[jax.ShapeDtypeStruct(((x_prompt.shape[0] + x_sample.shape[0]) * SUBLANES, LANES), F32),
                   jax.ShapeDtypeStruct((HEADS, DK, DV), F32),
                   jax.ShapeDtypeStruct((2, D_MODEL), F32),
                   jax.ShapeDtypeStruct(state_gla.shape, F32),
                   jax.ShapeDtypeStruct(state_conv.shape, F32)],
        scratch_shapes=[pltpu.VMEM((HEADS, DV, DK), F32),
                        pltpu.VMEM((max(TM + 8, streams * (seq + 8)), D_MODEL), F32),
                        pltpu.VMEM((TM, G0), F32),
                        pltpu.VMEM((TM, GLA_DK), F32),
                        pltpu.VMEM((TM, GLA_DV), F32)],
        compiler_params=pltpu.CompilerParams(dimension_semantics=("arbitrary",), vmem_limit_bytes=VMEM_LIMIT),
        name="mixer",
    )(meta, x_prompt, x_sample, state_gla, state_conv, *weights)


def _first_index_of_max(vals, ids, sentinel):
    m = jnp.max(vals, axis=0, keepdims=True)
    first = jnp.min(jnp.where(vals == m, ids, sentinel), axis=0, keepdims=True)
    return m, first


def _router_kernel(h_ref, wr_ref, bias_ref, idx_ref, w_ref, rank_ref, counts_ref):
    @pl.when(pl.program_id(0) == 0)
    def _init():
        counts_ref[...] = jnp.zeros_like(counts_ref)

    h = _tiles_to_rows(h_ref, TN_ROUTER)
    hh = h.astype(BF16)
    hl = (h - hh.astype(F32)).astype(BF16)
    wh, wl = wr_ref[0], wr_ref[1]
    dg = functools.partial(lax.dot_general, dimension_numbers=_DN_TB, preferred_element_type=F32)
    logits = dg(wh, hh) + dg(wl, hh) + dg(wh, hl)
    scores = _sigmoid(logits)
    biased = scores + bias_ref[...]
    n_tok = biased.shape[1]
    neg_inf = jnp.float32(-jnp.inf)

    eid = lax.broadcasted_iota(I32, (N_EXPERTS, n_tok), 0).astype(F32)
    lid = lax.broadcasted_iota(I32, (GROUP_SIZE, n_tok), 0).astype(F32)
    group_scores = []
    for g in range(N_GROUPS):
        blk = biased[g * GROUP_SIZE:(g + 1) * GROUP_SIZE]
        m1, first = _first_index_of_max(blk, lid, float(GROUP_SIZE))
        m2 = jnp.max(jnp.where(lid == first, neg_inf, blk), axis=0, keepdims=True)
        group_scores.append(m1 + m2)
    gsc = jnp.concatenate(group_scores, axis=0)

    gid = lax.broadcasted_iota(I32, (N_GROUPS, n_tok), 0).astype(F32)
    keep = jnp.zeros((N_GROUPS, n_tok), F32)
    cur = gsc
    for _ in range(TOPK_GROUPS):
        _, first = _first_index_of_max(cur, gid, float(N_GROUPS))
        sel = gid == first
        keep = jnp.where(sel, 1.0, keep)
        cur = jnp.where(sel, neg_inf, cur)

    masked = jnp.concatenate(
        [jnp.where(keep[g:g + 1] > 0.5, biased[g * GROUP_SIZE:(g + 1) * GROUP_SIZE], neg_inf)
         for g in range(N_GROUPS)], axis=0)
    idxs, wts, sels = [], [], []
    cur = masked
    for _ in range(TOP_K):
        _, first = _first_index_of_max(cur, eid, float(N_EXPERTS))
        sel = eid == first
        idxs.append(first)
        sels.append(sel)
        wts.append(jnp.sum(jnp.where(sel, scores, 0.0), axis=0, keepdims=True))
        cur = jnp.where(sel, neg_inf, cur)
    w = jnp.concatenate(wts, axis=0)
    idx_ref[...] = jnp.concatenate(idxs, axis=0).astype(I32)
    w_ref[...] = w / jnp.sum(w, axis=0, keepdims=True) * ROUTED_SCALE

    chosen = jnp.where(sels[0], 1.0, 0.0)
    for sel in sels[1:]:
        chosen = chosen + jnp.where(sel, 1.0, 0.0)
    t_src = lax.broadcasted_iota(I32, (n_tok, n_tok), 0)
    t_dst = lax.broadcasted_iota(I32, (n_tok, n_tok), 1)
    earlier = jnp.where(t_src < t_dst, 1.0, 0.0).astype(BF16)
    arrivals = counts_ref[...] + _dot(chosen.astype(BF16), earlier)
    rank_ref[...] = jnp.concatenate(
        [jnp.sum(jnp.where(sel, arrivals, 0.0), axis=0, keepdims=True) for sel in sels], axis=0).astype(I32)
    counts_ref[...] += jnp.sum(chosen, axis=1, keepdims=True)


def _router(h1_tiles, wr_split, bias_col):
    n = h1_tiles.shape[0] // SUBLANES
    tok = pl.BlockSpec((TOP_K, TN_ROUTER), lambda i: (0, i))
    return pl.pallas_call(
        _router_kernel,
        grid=(n // TN_ROUTER,),
        in_specs=[pl.BlockSpec((TN_ROUTER * SUBLANES, LANES), lambda i: (i, 0)),
                  pl.BlockSpec((2, N_EXPERTS, D_MODEL), lambda i: (0, 0, 0)),
                  pl.BlockSpec((N_EXPERTS, 1), lambda i: (0, 0))],
        out_specs=[tok, tok, tok, pl.BlockSpec((N_EXPERTS, 1), lambda i: (0, 0))],
        out_shape=[jax.ShapeDtypeStruct((TOP_K, n), I32), jax.ShapeDtypeStruct((TOP_K, n), F32),
                   jax.ShapeDtypeStruct((TOP_K, n), I32), jax.ShapeDtypeStruct((N_EXPERTS, 1), F32)],
        compiler_params=pltpu.CompilerParams(dimension_semantics=("arbitrary",), vmem_limit_bytes=VMEM_LIMIT),
        name="router",
    )(h1_tiles, wr_split, bias_col)


def _positions_kernel(idx_ref, rank_ref, segx_ref, segy_ref, posx_ref, posy_ref):
    n_tok = idx_ref.shape[1]
    eid = lax.broadcasted_iota(I32, (N_EXPERTS, n_tok), 0)
    segx, segy = segx_ref[...], segy_ref[...]
    rows_x, rows_y = [], []
    for k in range(TOP_K):
        onehot = eid == idx_ref[k:k + 1, :]
        rows_x.append(jnp.sum(jnp.where(onehot, segx, 0.0), axis=0, keepdims=True))
        rows_y.append(jnp.sum(jnp.where(onehot, segy, 0.0), axis=0, keepdims=True))
    posx_ref[...] = jnp.concatenate(rows_x, axis=0).astype(I32) + rank_ref[...]
    posy_ref[...] = jnp.concatenate(rows_y, axis=0).astype(I32) + rank_ref[...]


def _positions(idx_t, rank_t, segx_col, segy_col):
    n = idx_t.shape[1]
    tok = pl.BlockSpec((TOP_K, TN_ROUTER), lambda i: (0, i))
    table = pl.BlockSpec((N_EXPERTS, 1), lambda i: (0, 0))
    return pl.pallas_call(
        _positions_kernel,
        grid=(n // TN_ROUTER,),
        in_specs=[tok, tok, table, table],
        out_specs=[tok, tok],
        out_shape=[jax.ShapeDtypeStruct((TOP_K, n), I32), jax.ShapeDtypeStruct((TOP_K, n), I32)],
        compiler_params=pltpu.CompilerParams(dimension_semantics=("arbitrary",)),
        name="positions",
    )(idx_t, rank_t, segx_col, segy_col)


def _tile_copy(src, src_tile, dst, dst_tile, sem):
    return pltpu.make_async_copy(src.at[pl.ds(pl.multiple_of(src_tile * SUBLANES, SUBLANES), SUBLANES)],
                                 dst.at[pl.ds(pl.multiple_of(dst_tile * SUBLANES, SUBLANES), SUBLANES)], sem)


def _chunk_rows(first_tile):
    return pl.ds(pl.multiple_of(first_tile * SUBLANES, SUBLANES), CH * SUBLANES)


def _dispatch_kernel(pos_ref, h_ref, x_hbm, zeros, sem, zsem, *, n_assign):
    @pl.when(pl.program_id(0) == 0)
    def _zero_slack():
        zeros[...] = jnp.zeros_like(zeros)
        fill = pltpu.make_async_copy(zeros, x_hbm.at[_chunk_rows(n_assign)], zsem)
        fill.start()
        fill.wait()

    def body(t, carry):
        for k in range(TOP_K):
            _tile_copy(h_ref, t, x_hbm, pos_ref[0, 0, k * TN_DISPATCH + t], sem.at[k]).start(priority=k % 2)
        return carry
    lax.fori_loop(0, TN_DISPATCH, body, 0, unroll=2)
    for k in range(TOP_K):
        pltpu.make_async_copy(h_ref, x_hbm.at[pl.ds(0, TN_DISPATCH * SUBLANES)], sem.at[k]).wait()


def _dispatch(pos_tiles, h1_tiles):
    n_t = pos_tiles.shape[0]
    n_assign = n_t * TN_DISPATCH * TOP_K
    return pl.pallas_call(
        functools.partial(_dispatch_kernel, n_assign=n_assign),
        grid=(n_t,),
        in_specs=[pl.BlockSpec((1, 1, TOP_K * TN_DISPATCH), lambda t: (t, 0, 0), memory_space=pltpu.SMEM),
                  pl.BlockSpec((TN_DISPATCH * SUBLANES, LANES), lambda t: (t, 0))],
        out_specs=pl.BlockSpec(memory_space=pl.ANY),
        out_shape=jax.ShapeDtypeStruct(((n_assign + CH) * SUBLANES, LANES), F32),
        scratch_shapes=[pltpu.VMEM((CH * SUBLANES, LANES), F32), pltpu.SemaphoreType.DMA((TOP_K,)),
                        pltpu.SemaphoreType.DMA(())],
        compiler_params=pltpu.CompilerParams(dimension_semantics=("arbitrary",), has_side_effects=True),
        name="dispatch",
    )(pos_tiles, h1_tiles)


def _expert_kernel(gfirst_ref, cnt_ref, seq_ref, nxt_ref, gtot_ref, src_ref, x_hbm, wg_hbm, wu_hbm, wd_hbm, y_hbm,
                   xbuf, ybuf, zbuf, wg32, wu32, wd32, wgb, wub, wdb, xsem, ysem, zsem, wsem, *, g_max):
    e = pl.program_id(0)
    cnt = cnt_ref[e]
    g_total = gtot_ref[0]

    def x_copy(g, slot):
        return pltpu.make_async_copy(x_hbm.at[_chunk_rows(src_ref[g])], xbuf.at[slot], xsem.at[slot])

    def y_copy(g, slot):
        return pltpu.make_async_copy(ybuf.at[slot], y_hbm.at[_chunk_rows(g * CH)], ysem.at[slot])

    def tail_fill(g):
        return pltpu.make_async_copy(zbuf, y_hbm.at[_chunk_rows(g * CH)], zsem)

    def for_tail_chunks(fn):
        def body(g, carry):
            fn(g)
            return carry
        lax.fori_loop(g_total, g_max, body, 0)

    def weight_copies(ex, slot):
        return (pltpu.make_async_copy(wg_hbm.at[ex], wg32.at[slot], wsem.at[slot]),
                pltpu.make_async_copy(wu_hbm.at[ex], wu32.at[slot], wsem.at[slot]),
                pltpu.make_async_copy(wd_hbm.at[ex], wd32.at[slot], wsem.at[slot]))

    @pl.when(e == 0)
    def _prime():
        for g in range(NX - 1):
            @pl.when(g < g_total)
            def _():
                x_copy(g, g).start()
        zbuf[...] = jnp.zeros_like(zbuf)
        for_tail_chunks(lambda g: tail_fill(g).start())

    @pl.when(cnt > 0)
    def _expert():
        seq, nxt = seq_ref[e], nxt_ref[e]
        wslot = lax.rem(seq, 2)

        @pl.when(seq == 0)
        def _fetch_own():
            for c in weight_copies(e, wslot):
                c.start(priority=1)

        @pl.when(nxt >= 0)
        def _prefetch_weights():
            for c in weight_copies(nxt, 1 - wslot):
                c.start(priority=1)

        for c in weight_copies(e, wslot):
            c.wait()
        wgb[...] = wg32[wslot].astype(BF16)
        wub[...] = wu32[wslot].astype(BF16)
        wdb[...] = wd32[wslot].astype(BF16)

        g_first = gfirst_ref[e]

        def chunk(j, carry):
            g = g_first + j
            xs = lax.rem(g, NX)
            x_copy(0, xs).wait()
            ahead = g + (NX - 1)

            @pl.when(ahead < g_total)
            def _fetch_ahead():
                x_copy(ahead, lax.rem(ahead, NX)).start()

            xb = _tiles_to_rows(xbuf.at[xs], CH).astype(BF16)
            gate = _dot(xb, wgb[...])
            up = _dot(xb, wub[...])
            res = _dot(((gate * _sigmoid(gate)) * up).astype(BF16), wdb[...])
            ys = lax.rem(g, NY)

            @pl.when(g >= NY)
            def _slot_written_back():
                y_copy(0, ys).wait()

            _rows_to_tiles(ybuf.at[ys], res)
            y_copy(g, ys).start()
            return carry

        lax.fori_loop(0, (cnt + CH - 1) // CH, chunk, 0)

    @pl.when(e == pl.num_programs(0) - 1)
    def _finish():
        for back in range(1, NY + 1):
            @pl.when(g_total - back >= 0)
            def _():
                y_copy(0, lax.rem(g_total - back, NY)).wait()

        for_tail_chunks(lambda g: tail_fill(0).wait())


def _max_chunks(n_assign):
    return n_assign // CH + N_EXPERTS


def _segments(counts, n_assign):
    n_chunks = (counts + CH - 1) // CH
    g_end = jnp.cumsum(n_chunks).astype(I32)
    g_first = g_end - n_chunks
    x_start = (jnp.cumsum(counts) - counts).astype(I32)
    chunk = jnp.arange(_max_chunks(n_assign), dtype=I32)
    eid = jnp.arange(N_EXPERTS, dtype=I32)
    owner = jnp.minimum(jnp.sum((g_end[None, :] <= chunk[:, None]).astype(I32), axis=1), N_EXPERTS - 1)
    onehot = owner[:, None] == eid[None, :]
    pick = lambda v: jnp.sum(jnp.where(onehot, v[None, :], 0), axis=1)
    src = jnp.minimum(pick(x_start) + (chunk - pick(g_first)) * CH, n_assign).astype(I32)
    has_rows = counts > 0
    seq = jnp.where(has_rows, jnp.cumsum(has_rows.astype(I32)) - 1, -1).astype(I32)
    nxt = jnp.min(jnp.where((eid[None, :] > eid[:, None]) & has_rows[None, :], eid[None, :], N_EXPERTS), axis=1)
    nxt = jnp.where(nxt == N_EXPERTS, -1, nxt).astype(I32)
    return dict(x_start=x_start, y_start=g_first * CH, g_first=g_first, g_total=g_end[-1:], src=src,
                seq=seq, nxt=nxt, counts=counts)


def _experts(seg, x_sorted, w_gate, w_up, w_down, n_assign):
    g_max = _max_chunks(n_assign)
    hbm = pl.BlockSpec(memory_space=pl.ANY)
    grid_spec = pltpu.PrefetchScalarGridSpec(
        num_scalar_prefetch=6,
        grid=(N_EXPERTS,),
        in_specs=[hbm, hbm, hbm, hbm],
        out_specs=hbm,
        scratch_shapes=[pltpu.VMEM((NX, CH * SUBLANES, LANES), F32),
                        pltpu.VMEM((NY, CH * SUBLANES, LANES), F32),
                        pltpu.VMEM((CH * SUBLANES, LANES), F32),
                        pltpu.VMEM((2, D_MODEL, EXPERT_HIDDEN), F32),
                        pltpu.VMEM((2, D_MODEL, EXPERT_HIDDEN), F32),
                        pltpu.VMEM((2, EXPERT_HIDDEN, D_MODEL), F32),
                        pltpu.VMEM((D_MODEL, EXPERT_HIDDEN), BF16),
                        pltpu.VMEM((D_MODEL, EXPERT_HIDDEN), BF16),
                        pltpu.VMEM((EXPERT_HIDDEN, D_MODEL), BF16),
                        pltpu.SemaphoreType.DMA((NX,)), pltpu.SemaphoreType.DMA((NY,)),
                        pltpu.SemaphoreType.DMA(()), pltpu.SemaphoreType.DMA((2,))],
    )
    return pl.pallas_call(
        functools.partial(_expert_kernel, g_max=g_max),
        grid_spec=grid_spec,
        out_shape=jax.ShapeDtypeStruct((g_max * CH * SUBLANES, LANES), F32),
        compiler_params=pltpu.CompilerParams(dimension_semantics=("arbitrary",), vmem_limit_bytes=VMEM_LIMIT,
                                             has_side_effects=True),
        name="experts",
    )(seg["g_first"], seg["counts"], seg["seq"], seg["nxt"], seg["g_total"], seg["src"],
      x_sorted, w_gate, w_up, w_down)


def _combine_kernel(pos_cur_ref, pos_nxt_ref, y_hbm, h_ref, w_ref, wsg_ref, wsu_ref, wsd_ref, l2g_ref, l2b_ref,
                    outp_ref, outs_ref, ybuf, sem, *, n_prompt_tiles):
    t = pl.program_id(0)
    n_t = pl.num_programs(0)
    slot = t % 2

    def gather_start(pos_ref, dst, dst_sem):
        def body(r, carry):
            for k in range(TOP_K):
                j = k * TN_COMBINE + r
                _tile_copy(y_hbm, pos_ref[0, 0, j], dst, j, dst_sem).start(priority=k % 2)
            return carry
        lax.fori_loop(0, TN_COMBINE, body, 0, unroll=2)

    @pl.when(t == 0)
    def _first():
        gather_start(pos_cur_ref, ybuf.at[0], sem.at[0])

    @pl.when(t + 1 < n_t)
    def _prefetch():
        gather_start(pos_nxt_ref, ybuf.at[1 - slot], sem.at[1 - slot])

    h = _tiles_to_rows(h_ref, TN_COMBINE)
    hb = h.astype(BF16)
    g = _dot(hb, wsg_ref[...])
    u = _dot(hb, wsu_ref[...])
    shared = _dot(((g * _sigmoid(g)) * u).astype(BF16), wsd_ref[...])

    yslot = ybuf.at[slot]
    pltpu.make_async_copy(y_hbm.at[pl.ds(0, TOP_K * TN_COMBINE * SUBLANES)], yslot, sem.at[slot]).wait()
    w = w_ref[...]
    wk = [jnp.broadcast_to(w[:, k:k + 1], (TN_COMBINE, LANES)) for k in range(TOP_K)]
    chunks = []
    for s in range(SUBLANES):
        acc = wk[0] * yslot[pl.ds(s, TN_COMBINE, stride=SUBLANES), :]
        for k in range(1, TOP_K):
            acc = acc + wk[k] * yslot[pl.ds(k * TN_COMBINE * SUBLANES + s, TN_COMBINE, stride=SUBLANES), :]
        chunks.append(acc)
    routed = jnp.concatenate(chunks, axis=1)
    out = _layer_norm(ALPHA * h + (routed + shared), l2g_ref[...], l2b_ref[...])

    @pl.when(t < n_prompt_tiles)
    def _prompt_rows():
        outp_ref[...] = out

    @pl.when(t >= n_prompt_tiles)
    def _sample_rows():
        outs_ref[...] = out


def _combine(pos_tiles, y_sorted, h1_tiles, w_tok, wsg, wsu, wsd, l2g, l2b, n_prompt):
    n = h1_tiles.shape[0] // SUBLANES
    n_t = n // TN_COMBINE
    npt = n_prompt // TN_COMBINE
    n_idx = TOP_K * TN_COMBINE
    cspec = lambda shape: pl.BlockSpec(shape, lambda t: (0,) * len(shape))
    return pl.pallas_call(
        functools.partial(_combine_kernel, n_prompt_tiles=npt),
        grid=(n_t,),
        in_specs=[pl.BlockSpec((1, 1, n_idx), lambda t: (t, 0, 0), memory_space=pltpu.SMEM),
                  pl.BlockSpec((1, 1, n_idx), lambda t: (jnp.minimum(t + 1, n_t - 1), 0, 0),
                               memory_space=pltpu.SMEM),
                  pl.BlockSpec(memory_space=pl.ANY),
                  pl.BlockSpec((TN_COMBINE * SUBLANES, LANES), lambda t: (t, 0)),
                  pl.BlockSpec((TN_COMBINE, TOP_K), lambda t: (t, 0)),
                  cspec((D_MODEL, SHARED_HIDDEN)), cspec((D_MODEL, SHARED_HIDDEN)), cspec((SHARED_HIDDEN, D_MODEL)),
                  cspec((1, D_MODEL)), cspec((1, D_MODEL))],
        out_specs=[pl.BlockSpec((TN_COMBINE, D_MODEL), lambda t: (jnp.minimum(t, npt - 1), 0)),
                   pl.BlockSpec((TN_COMBINE, D_MODEL), lambda t: (jnp.maximum(t - npt, 0), 0))],
        out_shape=[jax.ShapeDtypeStruct((n_prompt, D_MODEL), F32),
                   jax.ShapeDtypeStruct((n - n_prompt, D_MODEL), F32)],
        scratch_shapes=[pltpu.VMEM((2, n_idx * SUBLANES, LANES), F32), pltpu.SemaphoreType.DMA((2,))],
        compiler_params=pltpu.CompilerParams(dimension_semantics=("arbitrary",), vmem_limit_bytes=VMEM_LIMIT),
        name="combine",
    )(pos_tiles, pos_tiles, y_sorted, h1_tiles, w_tok, wsg, wsu, wsd, l2g, l2b)


def _pack_mixer_weights(ln_emb_g, ln_emb_b, w_in, w_gate_a2, b_gate, gla_norm_g, w_gla_out, conv_w,
                        w_conv_out, w_o, ln1_g, ln1_b):
    q, k, v, g, a, cb, cc, ch, ga, gb = jnp.split(
        w_in, [512, 1024, 2048, 3072, 3088, 4112, 5136, 6160, 7184], axis=-1)
    w_main = jnp.concatenate([q, k, v, g, cb, cc, ch, ga, gb], axis=-1).astype(BF16)
    w_a = jnp.pad(a, ((0, 0), (0, RANK_PAD - GATE_RANK))).astype(BF16)
    w_a2 = jnp.pad(w_gate_a2, ((0, RANK_PAD - GATE_RANK), (0, 0))).astype(BF16)
    row = lambda x: x.reshape(1, -1).astype(F32)
    return [row(ln_emb_g), row(ln_emb_b), w_main, w_a, w_a2, row(b_gate), row(gla_norm_g),
            w_gla_out.astype(BF16), conv_w.astype(F32), w_conv_out.astype(BF16), w_o.astype(BF16),
            row(ln1_g), row(ln1_b)]


def kernel(x_prompt, x_sample, state_gla, state_conv, meta_tokens, ln_emb_g, ln_emb_b, w_in, w_gate_a2, b_gate, gla_norm_g, w_gla_out, conv_w, w_conv_out, w_o, ln1_g, ln1_b, w_router, router_bias, w_exp_gate, w_exp_up, w_exp_down, w_sh_gate, w_sh_up, w_sh_down, ln2_g, ln2_b):
    batch, seq, _ = x_prompt.shape
    dec_batch, dec_seq, _ = x_sample.shape
    depth = w_in.shape[0]
    assert batch == 1 and depth == 1 and seq % TM == 0 and TM % dec_seq == 0 and dec_seq % SUB == 0
    n_prompt, n_sample = batch * seq, dec_batch * dec_seq
    n_total = n_prompt + n_sample
    assert n_sample % TM == 0 and n_total % TN_ROUTER == 0
    assert n_prompt % TN_COMBINE == 0 and n_sample % TN_COMBINE == 0 and n_total % TN_DISPATCH == 0
    n_assign = n_total * TOP_K

    weights = _pack_mixer_weights(ln_emb_g, ln_emb_b, w_in[0], w_gate_a2[0], b_gate[0], gla_norm_g[0],
                                  w_gla_out[0], conv_w[0], w_conv_out[0], w_o[0], ln1_g[0], ln1_b[0])
    h1, sg_p, sc_p, sg_s, sc_s = _mixer(meta_tokens.astype(F32), x_prompt.reshape(n_prompt, D_MODEL),
                                        x_sample.reshape(n_sample, D_MODEL), state_gla[0], state_conv[0], weights)

    wr_t = w_router[0].T
    wr_hi = wr_t.astype(BF16)
    wr_split = jnp.stack([wr_hi, (wr_t - wr_hi.astype(F32)).astype(BF16)])
    idx_t, w_t, rank_t, counts = _router(h1, wr_split, router_bias[0].reshape(N_EXPERTS, 1).astype(F32))

    seg = _segments(counts.reshape(N_EXPERTS).astype(I32), n_assign)
    table = lambda v: v.astype(F32).reshape(N_EXPERTS, 1)
    posx_t, posy_t = _positions(idx_t, rank_t, table(seg["x_start"]), table(seg["y_start"]))
    tiles = lambda p, tn: p.reshape(TOP_K, n_total // tn, tn).transpose(1, 0, 2).reshape(n_total // tn, 1, -1)

    x_sorted = _dispatch(tiles(posx_t, TN_DISPATCH), h1)
    y_sorted = _experts(seg, x_sorted, w_exp_gate[0], w_exp_up[0], w_exp_down[0], n_assign)
    y_prompt, y_sample = _combine(tiles(posy_t, TN_COMBINE), y_sorted, h1, w_t.T,
                                  w_sh_gate[0].astype(BF16), w_sh_up[0].astype(BF16), w_sh_down[0].astype(BF16),
                                  ln2_g[0].reshape(1, -1), ln2_b[0].reshape(1, -1), n_prompt)
    y_prompt = y_prompt.reshape(batch, seq, D_MODEL)
    y_sample = y_sample.reshape(dec_batch, dec_seq, D_MODEL)
    return (y_prompt, y_sample,
            sg_p.reshape(depth, batch, HEADS, DK, DV), sc_p.reshape(depth, batch, 2, D_MODEL),
            sg_s.reshape(depth, dec_batch, HEADS, DK, DV), sc_s.reshape(depth, dec_batch, 2, D_MODEL))
```

```python
import functools

import jax
import jax.numpy as jnp
from jax import lax
from jax.experimental import pallas as pl
from jax.experimental.pallas import tpu as pltpu

F32 = jnp.float32
BF16 = jnp.bfloat16
I32 = jnp.int32

D_MODEL = 1024
N_META = 16
HEADS = 4
DK = 128
DV = 256
GLA_DK = HEADS * DK
GLA_DV = HEADS * DV
GATE_RANK = 16
GATE_TAU = 16.0
N_EXPERTS = 256
TOP_K = 8
N_GROUPS = 8
GROUP_SIZE = N_EXPERTS // N_GROUPS
TOPK_GROUPS = 4
EXPERT_HIDDEN = 256
SHARED_HIDDEN = 256
ROUTED_SCALE = 2.5
LN_EPS = 1e-5
RMS_EPS = 1e-6
ALPHA = 2.0 ** 0.25
QSCALE = DK ** -0.5

Q0, K0, V0, G0, CB0, CC0, CH0, GA0, GB0 = 0, 512, 1024, 2048, 3072, 4096, 5120, 6144, 7168
W_MAIN_COLS = 8192
RANK_PAD = 128

SUB = 16
PROMPT_CHUNK = 64
TM = 256
TN_ROUTER = 512
TN_DISPATCH = 256
CH = 128
GROUP = 4
X_AHEAD = 6
NX = X_AHEAD + GROUP
NY = 2 * GROUP
TN_COMBINE = 256
VMEM_LIMIT = 60 * 1024 * 1024
SUBLANES = 8
LANES = 128
assert D_MODEL == SUBLANES * LANES

_DN_TB = (((1,), (1,)), ((), ()))
_DN_TA = (((0,), (0,)), ((), ()))


def _dot(a, b):
    return jnp.dot(a, b, preferred_element_type=F32)


def _tiles_to_rows(ref, n, base=0):
    return jnp.concatenate(
        [ref[pl.ds(SUBLANES * base + s, n, stride=SUBLANES), :] for s in range(SUBLANES)], axis=1)


def _rows_to_tiles(ref, val):
    n = val.shape[0]
    for s in range(SUBLANES):
        ref[pl.ds(s, n, stride=SUBLANES), :] = val[:, s * LANES:(s + 1) * LANES]


def _layer_norm(x, g, b):
    mu = jnp.mean(x, axis=-1, keepdims=True)
    xc = x - mu
    var = jnp.mean(xc * xc, axis=-1, keepdims=True)
    return xc * lax.rsqrt(var + LN_EPS) * g + b


def _sigmoid(x):
    return 1.0 / (1.0 + jnp.exp(-x))


def _log_sigmoid(x):
    return jnp.minimum(x, 0.0) - jnp.log1p(jnp.exp(-jnp.abs(x)))


def _chunk_cumsum(g, chunk):
    t = g.shape[0]
    shift = chunk.bit_length() - 1
    r = lax.broadcasted_iota(I32, (t, t), 0)
    c = lax.broadcasted_iota(I32, (t, t), 1)
    tri = jnp.where(((r >> shift) == (c >> shift)) & (c <= r), 1.0, 0.0).astype(BF16)
    g1 = g.astype(BF16)
    r1 = g - g1.astype(F32)
    g2 = r1.astype(BF16)
    g3 = (r1 - g2.astype(F32)).astype(BF16)
    return _dot(tri, g1) + _dot(tri, g2) + _dot(tri, g3)


def _gate_log_decay(hb, wa_ref, wa2_ref, bg_ref):
    a = _dot(hb, wa_ref[...])
    return _log_sigmoid(_dot(a.astype(BF16), wa2_ref[...]) + bg_ref[...]) * (1.0 / GATE_TAU)


def _gla_state_update(k, v, b, st):
    n = k.shape[0]
    b_last = b[n - 1:n, :]
    khat = (k * jnp.exp(b_last - b)).astype(BF16)
    return st * jnp.exp(b_last) + lax.dot_general(v.astype(BF16), khat, _DN_TA, preferred_element_type=F32)


def _gla_chunk(q, k, v, b, st):
    n = q.shape[0]
    vb = v.astype(BF16)
    q0 = (q * (jnp.exp(b) * QSCALE)).astype(BF16)
    o = lax.dot_general(q0, st.astype(BF16), _DN_TB, preferred_element_type=F32)
    col = lax.broadcasted_iota(I32, (SUB, n), 1)
    row = lax.broadcasted_iota(I32, (SUB, n), 0)
    blocks = []
    for i in range(n // SUB):
        lo, hi = SUB * i, SUB * (i + 1)
        bref = b[lo:lo + 1, :]
        qi = (q[lo:hi] * (jnp.exp(b[lo:hi] - bref) * QSCALE)).astype(BF16)
        kk = (k[:hi] * jnp.exp(bref - b[:hi])).astype(BF16)
        if hi < n:
            kk = jnp.concatenate([kk, jnp.zeros((n - hi, DK), BF16)], axis=0)
        a = lax.dot_general(qi, kk, _DN_TB, preferred_element_type=F32)
        blocks.append(jnp.where(col <= row + lo, a, 0.0))
    scores = blocks[0] if len(blocks) == 1 else jnp.concatenate(blocks, axis=0)
    o = o + _dot(scores.astype(BF16), vb)
    return o, _gla_state_update(k, v, b, st)


def _mixer_back(h, hb, o_ref, conv, wm_ref, ng_ref, wgo_ref, wco_ref, wo_ref, l1g_ref, l1b_ref):
    g = _dot(hb, wm_ref[:, G0:G0 + GLA_DV])
    parts = []
    for hd in range(HEADS):
        oh = o_ref[:, hd * DV:(hd + 1) * DV]
        ms = jnp.mean(oh * oh, axis=-1, keepdims=True)
        parts.append(oh * lax.rsqrt(ms + RMS_EPS) * ng_ref[...])
    on = jnp.concatenate(parts, axis=1) * (g * _sigmoid(g))
    branch_a = _dot(on.astype(BF16), wgo_ref[...])
    mix = _sigmoid(_dot(hb, wm_ref[:, GA0:GA0 + D_MODEL])) * branch_a
    yc = _dot(hb, wm_ref[:, CB0:CB0 + D_MODEL]) * conv
    branch_b = _dot(yc.astype(BF16), wco_ref[...])
    mix = mix + _sigmoid(_dot(hb, wm_ref[:, GB0:GB0 + D_MODEL])) * branch_b
    mixed = _dot(mix.astype(BF16), wo_ref[...])
    return _layer_norm(ALPHA * h + mixed, l1g_ref[...], l1b_ref[...])


def _conv_input(hb, wm_ref):
    return _dot(hb, wm_ref[:, CC0:CC0 + D_MODEL]) * _dot(hb, wm_ref[:, CH0:CH0 + D_MODEL])


def _gla_rows(qkv_ref, b_ref, rows, hd, st):
    kc = slice(hd * DK, (hd + 1) * DK)
    return _gla_chunk(qkv_ref[rows, Q0 + hd * DK:Q0 + (hd + 1) * DK],
                      qkv_ref[rows, K0 + hd * DK:K0 + (hd + 1) * DK],
                      qkv_ref[rows, V0 + hd * DV:V0 + (hd + 1) * DV],
                      b_ref[rows, kc], st)


def _mixer_kernel(meta_ref, xp_ref, xs_ref, sgin_ref, scin_ref,
                  lng_ref, lnb_ref, wm_ref, wa_ref, wa2_ref, bg_ref,
                  ng_ref, wgo_ref, cw_ref, wco_ref, wo_ref, l1g_ref, l1b_ref,
                  h1_ref, sgp_ref, scp_ref, sgs_ref, scs_ref,
                  st_ref, zbuf_ref, qkv_ref, b_ref, o_ref, *, n_prompt_tiles, seq, streams):
    i = pl.program_id(0)
    back = functools.partial(_mixer_back, wm_ref=wm_ref, ng_ref=ng_ref, wgo_ref=wgo_ref, wco_ref=wco_ref,
                             wo_ref=wo_ref, l1g_ref=l1g_ref, l1b_ref=l1b_ref)

    def front(x):
        h = _layer_norm(x, lng_ref[...], lnb_ref[...])
        hb = h.astype(BF16)
        return h, hb, _gate_log_decay(hb, wa_ref, wa2_ref, bg_ref)

    @pl.when(i == 0)
    def _meta():
        _, hb, gk = front(meta_ref[...])
        k = _dot(hb, wm_ref[:, K0:K0 + GLA_DK])
        v = _dot(hb, wm_ref[:, V0:V0 + GLA_DV])
        b = _chunk_cumsum(gk, N_META)
        zero_state = jnp.zeros((DV, DK), F32)
        for hd in range(HEADS):
            kc = slice(hd * DK, (hd + 1) * DK)
            st_ref[hd] = _gla_state_update(k[:, kc], v[:, hd * DV:(hd + 1) * DV], b[:, kc], zero_state)
        zbuf_ref[0:8, :] = _conv_input(hb, wm_ref)[N_META - 8:N_META, :]

    @pl.when((i >= 1) & (i <= n_prompt_tiles))
    def _prompt_tile():
        h, hb, gk = front(xp_ref[...])
        qkv_ref[...] = _dot(hb, wm_ref[:, Q0:G0])
        b_ref[...] = _chunk_cumsum(gk, PROMPT_CHUNK)
        for c in range(TM // PROMPT_CHUNK):
            rows = slice(c * PROMPT_CHUNK, (c + 1) * PROMPT_CHUNK)
            for hd in range(HEADS):
                o, st_new = _gla_rows(qkv_ref, b_ref, rows, hd, st_ref[hd])
                st_ref[hd] = st_new
                o_ref[rows, hd * DV:(hd + 1) * DV] = o
        z = _conv_input(hb, wm_ref)
        zbuf_ref[8:8 + TM, :] = z
        cw = cw_ref[...]
        conv = cw[0:1, :] * zbuf_ref[6:6 + TM, :] + cw[1:2, :] * zbuf_ref[7:7 + TM, :] + cw[2:3, :] * z
        zbuf_ref[0:8, :] = z[TM - 8:TM, :]
        _rows_to_tiles(h1_ref, back(h, hb, o_ref, conv))

        @pl.when(i == n_prompt_tiles)
        def _prompt_states():
            for hd in range(HEADS):
                sgp_ref[hd] = st_ref[hd].T
            scp_ref[...] = z[TM - 2:TM, :]

    @pl.when(i > n_prompt_tiles)
    def _sample_tile():
        h, hb, gk = front(xs_ref[...])
        qkv_ref[...] = _dot(hb, wm_ref[:, Q0:G0])
        b_ref[...] = _chunk_cumsum(gk, seq)
        z = _conv_input(hb, wm_ref)
        pitch = seq + 8
        cw = cw_ref[...]
        convs = []
        for s in range(streams):
            rows = slice(s * seq, (s + 1) * seq)
            for hd in range(HEADS):
                o, st_new = _gla_rows(qkv_ref, b_ref, rows, hd, sgin_ref[s, hd].T)
                sgs_ref[s, hd] = st_new.T
                o_ref[rows, hd * DV:(hd + 1) * DV] = o
            zs = z[rows, :]
            base = s * pitch
            zbuf_ref[base + 6:base + 8, :] = scin_ref[s]
            zbuf_ref[base + 8:base + 8 + seq, :] = zs
            convs.append(cw[0:1, :] * zbuf_ref[base + 6:base + 6 + seq, :]
                         + cw[1:2, :] * zbuf_ref[base + 7:base + 7 + seq, :] + cw[2:3, :] * zs)
            scs_ref[s] = zs[seq - 2:seq, :]
        _rows_to_tiles(h1_ref, back(h, hb, o_ref, jnp.concatenate(convs, axis=0)))


def _const_spec(shape):
    nd = len(shape)
    return pl.BlockSpec(shape, lambda i, _nd=nd: (0,) * _nd, pipeline_mode=pl.Buffered(1))


def _mixer_weight_specs():
    return [
        _const_spec((1, D_MODEL)), _const_spec((1, D_MODEL)),
        _const_spec((D_MODEL, W_MAIN_COLS)),
        _const_spec((D_MODEL, RANK_PAD)), _const_spec((RANK_PAD, GLA_DK)), _const_spec((1, GLA_DK)),
        _const_spec((1, DV)), _const_spec((GLA_DV, D_MODEL)),
        _const_spec((3, D_MODEL)), _const_spec((D_MODEL, D_MODEL)),
        _const_spec((D_MODEL, D_MODEL)),
        _const_spec((1, D_MODEL)), _const_spec((1, D_MODEL)),
    ]


def _mixer(meta, x_prompt, x_sample, state_gla, state_conv, weights):
    n_streams = state_conv.shape[0]
    seq = x_sample.shape[0] // n_streams
    streams = TM // seq
    npt = x_prompt.shape[0] // TM
    nst = n_streams // streams
    ptile = lambda i: (jnp.clip(i - 1, 0, npt - 1), 0)
    stile = lambda i: jnp.clip(i - 1 - npt, 0, nst - 1)
    kern = functools.partial(_mixer_kernel, n_prompt_tiles=npt, seq=seq, streams=streams)
    return pl.pallas_call(
        kern,
        grid=(1 + npt + nst,),
        in_specs=[_const_spec((N_META, D_MODEL)),
                  pl.BlockSpec((TM, D_MODEL), ptile),
                  pl.BlockSpec((TM, D_MODEL), lambda i: (stile(i), 0), pipeline_mode=pl.Buffered(1)),
                  pl.BlockSpec((streams, HEADS, DK, DV), lambda i: (stile(i), 0, 0, 0),
                               pipeline_mode=pl.Buffered(1)),
                  pl.BlockSpec((streams, 2, D_MODEL), lambda i: (stile(i), 0, 0))] + _mixer_weight_specs(),
        out_specs=[pl.BlockSpec((TM * SUBLANES, LANES), lambda i: (jnp.maximum(i - 1, 0), 0)),
                   pl.BlockSpec((HEADS, DK, DV), lambda i: (0, 0, 0)),
                   pl.BlockSpec((2, D_MODEL), lambda i: (0, 0)),
                   pl.BlockSpec((streams, HEADS, DK, DV), lambda i: (stile(i), 0, 0, 0)),
                   pl.BlockSpec((streams, 2, D_MODEL), lambda i: (stile(i), 0, 0))],
        out_shape=[jax.ShapeDtypeStruct(((x_prompt.shape[0] + x_sample.shape[0]) * SUBLANES, LANES), F32),
                   jax.ShapeDtypeStruct((HEADS, DK, DV), F32),
                   jax.ShapeDtypeStruct((2, D_MODEL), F32),
                   jax.ShapeDtypeStruct(state_gla.shape, F32),
                   jax.ShapeDtypeStruct(state_conv.shape, F32)],
        scratch_shapes=[pltpu.VMEM((HEADS, DV, DK), F32),
                        pltpu.VMEM((max(TM + 8, streams * (seq + 8)), D_MODEL), F32),
                        pltpu.VMEM((TM, G0), F32),
                        pltpu.VMEM((TM, GLA_DK), F32),
                        pltpu.VMEM((TM, GLA_DV), F32)],
        compiler_params=pltpu.CompilerParams(dimension_semantics=("arbitrary",), vmem_limit_bytes=VMEM_LIMIT),
        name="mixer",
    )(meta, x_prompt, x_sample, state_gla, state_conv, *weights)


def _first_index_of_max(vals, ids, sentinel):
    m = jnp.max(vals, axis=0, keepdims=True)
    first = jnp.min(jnp.where(vals == m, ids, sentinel), axis=0, keepdims=True)
    return m, first


def _router_kernel(h_ref, wr_ref, bias_ref, idx_ref, w_ref, rank_ref, counts_ref):
    @pl.when(pl.program_id(0) == 0)
    def _init():
        counts_ref[...] = jnp.zeros_like(counts_ref)

    h = _tiles_to_rows(h_ref, TN_ROUTER)
    hh = h.astype(BF16)
    hl = (h - hh.astype(F32)).astype(BF16)
    wh, wl = wr_ref[0], wr_ref[1]
    dg = functools.partial(lax.dot_general, dimension_numbers=_DN_TB, preferred_element_type=F32)
    logits = dg(wh, hh) + dg(wl, hh) + dg(wh, hl)
    scores = _sigmoid(logits)
    biased = scores + bias_ref[...]
    n_tok = biased.shape[1]
    neg_inf = jnp.float32(-jnp.inf)

    eid = lax.broadcasted_iota(I32, (N_EXPERTS, n_tok), 0).astype(F32)
    lid = lax.broadcasted_iota(I32, (GROUP_SIZE, n_tok), 0).astype(F32)
    group_scores = []
    for g in range(N_GROUPS):
        blk = biased[g * GROUP_SIZE:(g + 1) * GROUP_SIZE]
        m1, first = _first_index_of_max(blk, lid, float(GROUP_SIZE))
        m2 = jnp.max(jnp.where(lid == first, neg_inf, blk), axis=0, keepdims=True)
        group_scores.append(m1 + m2)
    gsc = jnp.concatenate(group_scores, axis=0)

    gid = lax.broadcasted_iota(I32, (N_GROUPS, n_tok), 0).astype(F32)
    keep = jnp.zeros((N_GROUPS, n_tok), F32)
    cur = gsc
    for _ in range(TOPK_GROUPS):
        _, first = _first_index_of_max(cur, gid, float(N_GROUPS))
        sel = gid == first
        keep = jnp.where(sel, 1.0, keep)
        cur = jnp.where(sel, neg_inf, cur)

    masked = jnp.concatenate(
        [jnp.where(keep[g:g + 1] > 0.5, biased[g * GROUP_SIZE:(g + 1) * GROUP_SIZE], neg_inf)
         for g in range(N_GROUPS)], axis=0)
    idxs, wts, sels = [], [], []
    cur = masked
    for _ in range(TOP_K):
        _, first = _first_index_of_max(cur, eid, float(N_EXPERTS))
        sel = eid == first
        idxs.append(first)
        sels.append(sel)
        wts.append(jnp.sum(jnp.where(sel, scores, 0.0), axis=0, keepdims=True))
        cur = jnp.where(sel, neg_inf, cur)
    w = jnp.concatenate(wts, axis=0)
    idx_ref[...] = jnp.concatenate(idxs, axis=0).astype(I32)
    w_ref[...] = w / jnp.sum(w, axis=0, keepdims=True) * ROUTED_SCALE

    chosen = jnp.where(sels[0], 1.0, 0.0)
    for sel in sels[1:]:
        chosen = chosen + jnp.where(sel, 1.0, 0.0)
    t_src = lax.broadcasted_iota(I32, (n_tok, n_tok), 0)
    t_dst = lax.broadcasted_iota(I32, (n_tok, n_tok), 1)
    earlier = jnp.where(t_src < t_dst, 1.0, 0.0).astype(BF16)
    arrivals = counts_ref[...] + _dot(chosen.astype(BF16), earlier)
    rank_ref[...] = jnp.concatenate(
        [jnp.sum(jnp.where(sel, arrivals, 0.0), axis=0, keepdims=True) for sel in sels], axis=0).astype(I32)
    counts_ref[...] += jnp.sum(chosen, axis=1, keepdims=True)


def _router(h1_tiles, wr_split, bias_col):
    n = h1_tiles.shape[0] // SUBLANES
    tok = pl.BlockSpec((TOP_K, TN_ROUTER), lambda i: (0, i))
    return pl.pallas_call(
        _router_kernel,
        grid=(n // TN_ROUTER,),
        in_specs=[pl.BlockSpec((TN_ROUTER * SUBLANES, LANES), lambda i: (i, 0)),
                  pl.BlockSpec((2, N_EXPERTS, D_MODEL), lambda i: (0, 0, 0)),
                  pl.BlockSpec((N_EXPERTS, 1), lambda i: (0, 0))],
        out_specs=[tok, tok, tok, pl.BlockSpec((N_EXPERTS, 1), lambda i: (0, 0))],
        out_shape=[jax.ShapeDtypeStruct((TOP_K, n), I32), jax.ShapeDtypeStruct((TOP_K, n), F32),
                   jax.ShapeDtypeStruct((TOP_K, n), I32), jax.ShapeDtypeStruct((N_EXPERTS, 1), F32)],
        compiler_params=pltpu.CompilerParams(dimension_semantics=("arbitrary",), vmem_limit_bytes=VMEM_LIMIT),
        name="router",
    )(h1_tiles, wr_split, bias_col)


def _positions_kernel(idx_ref, rank_ref, segx_ref, segy_ref, posx_ref, posy_ref):
    n_tok = idx_ref.shape[1]
    eid = lax.broadcasted_iota(I32, (N_EXPERTS, n_tok), 0)
    segx, segy = segx_ref[...], segy_ref[...]
    rows_x, rows_y = [], []
    for k in range(TOP_K):
        onehot = eid == idx_ref[k:k + 1, :]
        rows_x.append(jnp.sum(jnp.where(onehot, segx, 0.0), axis=0, keepdims=True))
        rows_y.append(jnp.sum(jnp.where(onehot, segy, 0.0), axis=0, keepdims=True))
    posx_ref[...] = jnp.concatenate(rows_x, axis=0).astype(I32) + rank_ref[...]
    posy_ref[...] = jnp.concatenate(rows_y, axis=0).astype(I32) + rank_ref[...]


def _positions(idx_t, rank_t, segx_col, segy_col):
    n = idx_t.shape[1]
    tok = pl.BlockSpec((TOP_K, TN_ROUTER), lambda i: (0, i))
    table = pl.BlockSpec((N_EXPERTS, 1), lambda i: (0, 0))
    return pl.pallas_call(
        _positions_kernel,
        grid=(n // TN_ROUTER,),
        in_specs=[tok, tok, table, table],
        out_specs=[tok, tok],
        out_shape=[jax.ShapeDtypeStruct((TOP_K, n), I32), jax.ShapeDtypeStruct((TOP_K, n), I32)],
        compiler_params=pltpu.CompilerParams(dimension_semantics=("arbitrary",)),
        name="positions",
    )(idx_t, rank_t, segx_col, segy_col)


def _tile_copy(src, src_tile, dst, dst_tile, sem):
    return pltpu.make_async_copy(src.at[pl.ds(pl.multiple_of(src_tile * SUBLANES, SUBLANES), SUBLANES)],
                                 dst.at[pl.ds(pl.multiple_of(dst_tile * SUBLANES, SUBLANES), SUBLANES)], sem)


def _chunk_rows(first_tile):
    return pl.ds(pl.multiple_of(first_tile * SUBLANES, SUBLANES), CH * SUBLANES)


def _dispatch_kernel(pos_ref, h_ref, x_hbm, zeros, sem, zsem, *, n_assign):
    @pl.when(pl.program_id(0) == 0)
    def _zero_slack():
        zeros[...] = jnp.zeros_like(zeros)
        fill = pltpu.make_async_copy(zeros, x_hbm.at[_chunk_rows(n_assign)], zsem)
        fill.start()
        fill.wait()

    def body(t, carry):
        for k in range(TOP_K):
            _tile_copy(h_ref, t, x_hbm, pos_ref[0, 0, k * TN_DISPATCH + t], sem.at[k]).start(priority=k % 2)
        return carry
    lax.fori_loop(0, TN_DISPATCH, body, 0, unroll=2)
    for k in range(TOP_K):
        pltpu.make_async_copy(h_ref, x_hbm.at[pl.ds(0, TN_DISPATCH * SUBLANES)], sem.at[k]).wait()


def _dispatch(pos_tiles, h1_tiles):
    n_t = pos_tiles.shape[0]
    n_assign = n_t * TN_DISPATCH * TOP_K
    return pl.pallas_call(
        functools.partial(_dispatch_kernel, n_assign=n_assign),
        grid=(n_t,),
        in_specs=[pl.BlockSpec((1, 1, TOP_K * TN_DISPATCH), lambda t: (t, 0, 0), memory_space=pltpu.SMEM),
                  pl.BlockSpec((TN_DISPATCH * SUBLANES, LANES), lambda t: (t, 0))],
        out_specs=pl.BlockSpec(memory_space=pl.ANY),
        out_shape=jax.ShapeDtypeStruct(((n_assign + CH) * SUBLANES, LANES), F32),
        scratch_shapes=[pltpu.VMEM((CH * SUBLANES, LANES), F32), pltpu.SemaphoreType.DMA((TOP_K,)),
                        pltpu.SemaphoreType.DMA(())],
        compiler_params=pltpu.CompilerParams(dimension_semantics=("arbitrary",), has_side_effects=True),
        name="dispatch",
    )(pos_tiles, h1_tiles)


def _expert_kernel(gfirst_ref, cnt_ref, seq_ref, nxt_ref, gtot_ref, src_ref, x_hbm, wg_hbm, wu_hbm, wd_hbm, y_hbm,
                   xbuf, ybuf, zbuf, wg32, wu32, wd32, wgb, wub, wdb, xsem, ysem, zsem, wsem, *, g_max):
    e = pl.program_id(0)
    cnt = cnt_ref[e]
    g_total = gtot_ref[0]

    def x_copy(g, slot):
        return pltpu.make_async_copy(x_hbm.at[_chunk_rows(src_ref[g])], xbuf.at[slot], xsem.at[slot])

    def y_copy(g, slot):
        return pltpu.make_async_copy(ybuf.at[slot], y_hbm.at[_chunk_rows(g * CH)], ysem.at[slot])

    def tail_fill(g):
        return pltpu.make_async_copy(zbuf, y_hbm.at[_chunk_rows(g * CH)], zsem)

    def for_tail_chunks(fn):
        def body(g, carry):
            fn(g)
            return carry
        lax.fori_loop(g_total, g_max, body, 0)

    def weight_copies(ex, slot):
        return (pltpu.make_async_copy(wg_hbm.at[ex], wg32.at[slot], wsem.at[slot]),
                pltpu.make_async_copy(wu_hbm.at[ex], wu32.at[slot], wsem.at[slot]),
                pltpu.make_async_copy(wd_hbm.at[ex], wd32.at[slot], wsem.at[slot]))

    @pl.when(e == 0)
    def _prime():
        for g in range(X_AHEAD):
            @pl.when(g < g_total)
            def _():
                x_copy(g, g).start()
        zbuf[...] = jnp.zeros_like(zbuf)
        for_tail_chunks(lambda g: tail_fill(g).start())

    @pl.when(cnt > 0)
    def _expert():
        seq, nxt = seq_ref[e], nxt_ref[e]
        wslot = lax.rem(seq, 2)

        @pl.when(seq == 0)
        def _fetch_own():
            for c in weight_copies(e, wslot):
                c.start(priority=1)

        @pl.when(nxt >= 0)
        def _prefetch_weights():
            for c in weight_copies(nxt, 1 - wslot):
                c.start(priority=1)

        for c in weight_copies(e, wslot):
            c.wait()
        wgb[...] = wg32[wslot].astype(BF16)
        wub[...] = wu32[wslot].astype(BF16)
        wdb[...] = wd32[wslot].astype(BF16)

        def process(g, width):
            for i in range(width):
                x_copy(0, lax.rem(g + i, NX)).wait()
            for i in range(width):
                ahead = g + i + X_AHEAD

                @pl.when(ahead < g_total)
                def _fetch_ahead():
                    x_copy(ahead, lax.rem(ahead, NX)).start()

            xb = jnp.concatenate([_tiles_to_rows(xbuf.at[lax.rem(g + i, NX)], CH) for i in range(width)],
                                 axis=0).astype(BF16)
            gate = _dot(xb, wgb[...])
            up = _dot(xb, wub[...])
            res = _dot(((gate * _sigmoid(gate)) * up).astype(BF16), wdb[...])
            for i in range(width):
                ys = lax.rem(g + i, NY)

                @pl.when(g + i >= NY)
                def _slot_written_back():
                    y_copy(0, ys).wait()

                _rows_to_tiles(ybuf.at[ys], res[i * CH:(i + 1) * CH])
                y_copy(g + i, ys).start()

        g_first = gfirst_ref[e]
        n_chunks = (cnt + CH - 1) // CH
        n_wide = n_chunks // GROUP
        rest = n_chunks - n_wide * GROUP

        def wide_group(j, carry):
            process(g_first + j * GROUP, GROUP)
            return carry

        lax.fori_loop(0, n_wide, wide_group, 0)
        g_rest = g_first + n_wide * GROUP
        for width in (2, 1):
            @pl.when((rest & width) != 0)
            def _narrow_group():
                process(g_rest + (rest & ~(2 * width - 1)), width)

    @pl.when(e == pl.num_programs(0) - 1)
    def _finish():
        for back in range(1, NY + 1):
            @pl.when(g_total - back >= 0)
            def _():
                y_copy(0, lax.rem(g_total - back, NY)).wait()

        for_tail_chunks(lambda g: tail_fill(0).wait())


def _max_chunks(n_assign):
    return n_assign // CH + N_EXPERTS


def _segments(counts, n_assign):
    n_chunks = (counts + CH - 1) // CH
    g_end = jnp.cumsum(n_chunks).astype(I32)
    g_first = g_end - n_chunks
    x_start = (jnp.cumsum(counts) - counts).astype(I32)
    chunk = jnp.arange(_max_chunks(n_assign), dtype=I32)
    eid = jnp.arange(N_EXPERTS, dtype=I32)
    owner = jnp.minimum(jnp.sum((g_end[None, :] <= chunk[:, None]).astype(I32), axis=1), N_EXPERTS - 1)
    onehot = owner[:, None] == eid[None, :]
    pick = lambda v: jnp.sum(jnp.where(onehot, v[None, :], 0), axis=1)
    src = jnp.minimum(pick(x_start) + (chunk - pick(g_first)) * CH, n_assign).astype(I32)
    has_rows = counts > 0
    seq = jnp.where(has_rows, jnp.cumsum(has_rows.astype(I32)) - 1, -1).astype(I32)
    nxt = jnp.min(jnp.where((eid[None, :] > eid[:, None]) & has_rows[None, :], eid[None, :], N_EXPERTS), axis=1)
    nxt = jnp.where(nxt == N_EXPERTS, -1, nxt).astype(I32)
    return dict(x_start=x_start, y_start=g_first * CH, g_first=g_first, g_total=g_end[-1:], src=src,
                seq=seq, nxt=nxt, counts=counts)


def _experts(seg, x_sorted, w_gate, w_up, w_down, n_assign):
    g_max = _max_chunks(n_assign)
    hbm = pl.BlockSpec(memory_space=pl.ANY)
    grid_spec = pltpu.PrefetchScalarGridSpec(
        num_scalar_prefetch=6,
        grid=(N_EXPERTS,),
        in_specs=[hbm, hbm, hbm, hbm],
        out_specs=hbm,
        scratch_shapes=[pltpu.VMEM((NX, CH * SUBLANES, LANES), F32),
                        pltpu.VMEM((NY, CH * SUBLANES, LANES), F32),
                        pltpu.VMEM((CH * SUBLANES, LANES), F32),
                        pltpu.VMEM((2, D_MODEL, EXPERT_HIDDEN), F32),
                        pltpu.VMEM((2, D_MODEL, EXPERT_HIDDEN), F32),
                        pltpu.VMEM((2, EXPERT_HIDDEN, D_MODEL), F32),
                        pltpu.VMEM((D_MODEL, EXPERT_HIDDEN), BF16),
                        pltpu.VMEM((D_MODEL, EXPERT_HIDDEN), BF16),
                        pltpu.VMEM((EXPERT_HIDDEN, D_MODEL), BF16),
                        pltpu.SemaphoreType.DMA((NX,)), pltpu.SemaphoreType.DMA((NY,)),
                        pltpu.SemaphoreType.DMA(()), pltpu.SemaphoreType.DMA((2,))],
    )
    return pl.pallas_call(
        functools.partial(_expert_kernel, g_max=g_max),
        grid_spec=grid_spec,
        out_shape=jax.ShapeDtypeStruct((g_max * CH * SUBLANES, LANES), F32),
        compiler_params=pltpu.CompilerParams(dimension_semantics=("arbitrary",), vmem_limit_bytes=VMEM_LIMIT,
                                             has_side_effects=True),
        name="experts",
    )(seg["g_first"], seg["counts"], seg["seq"], seg["nxt"], seg["g_total"], seg["src"],
      x_sorted, w_gate, w_up, w_down)


def _combine_kernel(pos_cur_ref, pos_nxt_ref, y_hbm, h_ref, w_ref, wsg_ref, wsu_ref, wsd_ref, l2g_ref, l2b_ref,
                    outp_ref, outs_ref, ybuf, sem, *, n_prompt_tiles):
    t = pl.program_id(0)
    n_t = pl.num_programs(0)
    slot = t % 2

    def gather_start(pos_ref, dst, dst_sem):
        def body(r, carry):
            for k in range(TOP_K):
                j = k * TN_COMBINE + r
                _tile_copy(y_hbm, pos_ref[0, 0, j], dst, j, dst_sem).start(priority=k % 2)
            return carry
        lax.fori_loop(0, TN_COMBINE, body, 0, unroll=2)

    @pl.when(t == 0)
    def _first():
        gather_start(pos_cur_ref, ybuf.at[0], sem.at[0])

    @pl.when(t + 1 < n_t)
    def _prefetch():
        gather_start(pos_nxt_ref, ybuf.at[1 - slot], sem.at[1 - slot])

    h = _tiles_to_rows(h_ref, TN_COMBINE)
    hb = h.astype(BF16)
    g = _dot(hb, wsg_ref[...])
    u = _dot(hb, wsu_ref[...])
    shared = _dot(((g * _sigmoid(g)) * u).astype(BF16), wsd_ref[...])

    yslot = ybuf.at[slot]
    pltpu.make_async_copy(y_hbm.at[pl.ds(0, TOP_K * TN_COMBINE * SUBLANES)], yslot, sem.at[slot]).wait()
    w = w_ref[...]
    wk = [jnp.broadcast_to(w[:, k:k + 1], (TN_COMBINE, LANES)) for k in range(TOP_K)]
    chunks = []
    for s in range(SUBLANES):
        acc = wk[0] * yslot[pl.ds(s, TN_COMBINE, stride=SUBLANES), :]
        for k in range(1, TOP_K):
            acc = acc + wk[k] * yslot[pl.ds(k * TN_COMBINE * SUBLANES + s, TN_COMBINE, stride=SUBLANES), :]
        chunks.append(acc)
    routed = jnp.concatenate(chunks, axis=1)
    out = _layer_norm(ALPHA * h + (routed + shared), l2g_ref[...], l2b_ref[...])

    @pl.when(t < n_prompt_tiles)
    def _prompt_rows():
        outp_ref[...] = out

    @pl.when(t >= n_prompt_tiles)
    def _sample_rows():
        outs_ref[...] = out


def _combine(pos_tiles, y_sorted, h1_tiles, w_tok, wsg, wsu, wsd, l2g, l2b, n_prompt):
    n = h1_tiles.shape[0] // SUBLANES
    n_t = n // TN_COMBINE
    npt = n_prompt // TN_COMBINE
    n_idx = TOP_K * TN_COMBINE
    cspec = lambda shape: pl.BlockSpec(shape, lambda t: (0,) * len(shape))
    return pl.pallas_call(
        functools.partial(_combine_kernel, n_prompt_tiles=npt),
        grid=(n_t,),
        in_specs=[pl.BlockSpec((1, 1, n_idx), lambda t: (t, 0, 0), memory_space=pltpu.SMEM),
                  pl.BlockSpec((1, 1, n_idx), lambda t: (jnp.minimum(t + 1, n_t - 1), 0, 0),
                               memory_space=pltpu.SMEM),
                  pl.BlockSpec(memory_space=pl.ANY),
                  pl.BlockSpec((TN_COMBINE * SUBLANES, LANES), lambda t: (t, 0)),
                  pl.BlockSpec((TN_COMBINE, TOP_K), lambda t: (t, 0)),
                  cspec((D_MODEL, SHARED_HIDDEN)), cspec((D_MODEL, SHARED_HIDDEN)), cspec((SHARED_HIDDEN, D_MODEL)),
                  cspec((1, D_MODEL)), cspec((1, D_MODEL))],
        out_specs=[pl.BlockSpec((TN_COMBINE, D_MODEL), lambda t: (jnp.minimum(t, npt - 1), 0)),
                   pl.BlockSpec((TN_COMBINE, D_MODEL), lambda t: (jnp.maximum(t - npt, 0), 0))],
        out_shape=[jax.ShapeDtypeStruct((n_prompt, D_MODEL), F32),
                   jax.ShapeDtypeStruct((n - n_prompt, D_MODEL), F32)],
        scratch_shapes=[pltpu.VMEM((2, n_idx * SUBLANES, LANES), F32), pltpu.SemaphoreType.DMA((2,))],
        compiler_params=pltpu.CompilerParams(dimension_semantics=("arbitrary",), vmem_limit_bytes=VMEM_LIMIT),
        name="combine",
    )(pos_tiles, pos_tiles, y_sorted, h1_tiles, w_tok, wsg, wsu, wsd, l2g, l2b)


def _pack_mixer_weights(ln_emb_g, ln_emb_b, w_in, w_gate_a2, b_gate, gla_norm_g, w_gla_out, conv_w,
                        w_conv_out, w_o, ln1_g, ln1_b):
    q, k, v, g, a, cb, cc, ch, ga, gb = jnp.split(
        w_in, [512, 1024, 2048, 3072, 3088, 4112, 5136, 6160, 7184], axis=-1)
    w_main = jnp.concatenate([q, k, v, g, cb, cc, ch, ga, gb], axis=-1).astype(BF16)
    w_a = jnp.pad(a, ((0, 0), (0, RANK_PAD - GATE_RANK))).astype(BF16)
    w_a2 = jnp.pad(w_gate_a2, ((0, RANK_PAD - GATE_RANK), (0, 0))).astype(BF16)
    row = lambda x: x.reshape(1, -1).astype(F32)
    return [row(ln_emb_g), row(ln_emb_b), w_main, w_a, w_a2, row(b_gate), row(gla_norm_g),
            w_gla_out.astype(BF16), conv_w.astype(F32), w_conv_out.astype(BF16), w_o.astype(BF16),
            row(ln1_g), row(ln1_b)]


def kernel(x_prompt, x_sample, state_gla, state_conv, meta_tokens, ln_emb_g, ln_emb_b, w_in, w_gate_a2, b_gate, gla_norm_g, w_gla_out, conv_w, w_conv_out, w_o, ln1_g, ln1_b, w_router, router_bias, w_exp_gate, w_exp_up, w_exp_down, w_sh_gate, w_sh_up, w_sh_down, ln2_g, ln2_b):
    batch, seq, _ = x_prompt.shape
    dec_batch, dec_seq, _ = x_sample.shape
    depth = w_in.shape[0]
    assert batch == 1 and depth == 1 and seq % TM == 0 and TM % dec_seq == 0 and dec_seq % SUB == 0
    n_prompt, n_sample = batch * seq, dec_batch * dec_seq
    n_total = n_prompt + n_sample
    assert n_sample % TM == 0 and n_total % TN_ROUTER == 0
    assert n_prompt % TN_COMBINE == 0 and n_sample % TN_COMBINE == 0 and n_total % TN_DISPATCH == 0
    n_assign = n_total * TOP_K

    weights = _pack_mixer_weights(ln_emb_g, ln_emb_b, w_in[0], w_gate_a2[0], b_gate[0], gla_norm_g[0],
                                  w_gla_out[0], conv_w[0], w_conv_out[0], w_o[0], ln1_g[0], ln1_b[0])
    h1, sg_p, sc_p, sg_s, sc_s = _mixer(meta_tokens.astype(F32), x_prompt.reshape(n_prompt, D_MODEL),
                                        x_sample.reshape(n_sample, D_MODEL), state_gla[0], state_conv[0], weights)

    wr_t = w_router[0].T
    wr_hi = wr_t.astype(BF16)
    wr_split = jnp.stack([wr_hi, (wr_t - wr_hi.astype(F32)).astype(BF16)])
    idx_t, w_t, rank_t, counts = _router(h1, wr_split, router_bias[0].reshape(N_EXPERTS, 1).astype(F32))

    seg = _segments(counts.reshape(N_EXPERTS).astype(I32), n_assign)
    table = lambda v: v.astype(F32).reshape(N_EXPERTS, 1)
    posx_t, posy_t = _positions(idx_t, rank_t, table(seg["x_start"]), table(seg["y_start"]))
    tiles = lambda p, tn: p.reshape(TOP_K, n_total // tn, tn).transpose(1, 0, 2).reshape(n_total // tn, 1, -1)

    x_sorted = _dispatch(tiles(posx_t, TN_DISPATCH), h1)
    y_sorted = _experts(seg, x_sorted, w_exp_gate[0], w_exp_up[0], w_exp_down[0], n_assign)
    y_prompt, y_sample = _combine(tiles(posy_t, TN_COMBINE), y_sorted, h1, w_t.T,
                                  w_sh_gate[0].astype(BF16), w_sh_up[0].astype(BF16), w_sh_down[0].astype(BF16),
                                  ln2_g[0].reshape(1, -1), ln2_b[0].reshape(1, -1), n_prompt)
    y_prompt = y_prompt.reshape(batch, seq, D_MODEL)
    y_sample = y_sample.reshape(dec_batch, dec_seq, D_MODEL)
    return (y_prompt, y_sample,
            sg_p.reshape(depth, batch, HEADS, DK, DV), sc_p.reshape(depth, batch, 2, D_MODEL),
            sg_s.reshape(depth, dec_batch, HEADS, DK, DV), sc_s.reshape(depth, dec_batch, 2, D_MODEL))
```

```python
import functools

import jax
import jax.numpy as jnp
from jax import lax
from jax.experimental import pallas as pl
from jax.experimental.pallas import tpu as pltpu

F32 = jnp.float32
BF16 = jnp.bfloat16
I32 = jnp.int32

D_MODEL = 1024
N_META = 16
HEADS = 4
DK = 128
DV = 256
GLA_DK = HEADS * DK
GLA_DV = HEADS * DV
GATE_RANK = 16
GATE_TAU = 16.0
N_EXPERTS = 256
TOP_K = 8
N_GROUPS = 8
GROUP_SIZE = N_EXPERTS // N_GROUPS
TOPK_GROUPS = 4
EXPERT_HIDDEN = 256
SHARED_HIDDEN = 256
ROUTED_SCALE = 2.5
LN_EPS = 1e-5
RMS_EPS = 1e-6
ALPHA = 2.0 ** 0.25
QSCALE = DK ** -0.5

Q0, K0, V0, G0, CB0, CC0, CH0, GA0, GB0 = 0, 512, 1024, 2048, 3072, 4096, 5120, 6144, 7168
W_MAIN_COLS = 8192
RANK_PAD = 128

SUB = 16
PROMPT_CHUNK = 64
TM = 256
TN_ROUTER = 512
TN_DISPATCH = 256
CH = 64
GROUP = 8
X_AHEAD = 12
assert GROUP & (GROUP - 1) == 0
NX = X_AHEAD + GROUP
NY = 2 * GROUP
TN_COMBINE = 256
VMEM_LIMIT = 60 * 1024 * 1024
SUBLANES = 8
LANES = 128
assert D_MODEL == SUBLANES * LANES

_DN_TB = (((1,), (1,)), ((), ()))
_DN_TA = (((0,), (0,)), ((), ()))


def _dot(a, b):
    return jnp.dot(a, b, preferred_element_type=F32)


def _tiles_to_rows(ref, n, base=0):
    return jnp.concatenate(
        [ref[pl.ds(SUBLANES * base + s, n, stride=SUBLANES), :] for s in range(SUBLANES)], axis=1)


def _rows_to_tiles(ref, val):
    n = val.shape[0]
    for s in range(SUBLANES):
        ref[pl.ds(s, n, stride=SUBLANES), :] = val[:, s * LANES:(s + 1) * LANES]


def _layer_norm(x, g, b):
    mu = jnp.mean(x, axis=-1, keepdims=True)
    xc = x - mu
    var = jnp.mean(xc * xc, axis=-1, keepdims=True)
    return xc * lax.rsqrt(var + LN_EPS) * g + b


def _sigmoid(x):
    return 1.0 / (1.0 + jnp.exp(-x))


def _log_sigmoid(x):
    return jnp.minimum(x, 0.0) - jnp.log1p(jnp.exp(-jnp.abs(x)))


def _chunk_cumsum(g, chunk):
    t = g.shape[0]
    shift = chunk.bit_length() - 1
    r = lax.broadcasted_iota(I32, (t, t), 0)
    c = lax.broadcasted_iota(I32, (t, t), 1)
    tri = jnp.where(((r >> shift) == (c >> shift)) & (c <= r), 1.0, 0.0).astype(BF16)
    g1 = g.astype(BF16)
    r1 = g - g1.astype(F32)
    g2 = r1.astype(BF16)
    g3 = (r1 - g2.astype(F32)).astype(BF16)
    return _dot(tri, g1) + _dot(tri, g2) + _dot(tri, g3)


def _gate_log_decay(hb, wa_ref, wa2_ref, bg_ref):
    a = _dot(hb, wa_ref[...])
    return _log_sigmoid(_dot(a.astype(BF16), wa2_ref[...]) + bg_ref[...]) * (1.0 / GATE_TAU)


def _gla_state_update(k, v, b, st):
    n = k.shape[0]
    b_last = b[n - 1:n, :]
    khat = (k * jnp.exp(b_last - b)).astype(BF16)
    return st * jnp.exp(b_last) + lax.dot_general(v.astype(BF16), khat, _DN_TA, preferred_element_type=F32)


def _gla_chunk(q, k, v, b, st):
    n = q.shape[0]
    vb = v.astype(BF16)
    q0 = (q * (jnp.exp(b) * QSCALE)).astype(BF16)
    o = lax.dot_general(q0, st.astype(BF16), _DN_TB, preferred_element_type=F32)
    col = lax.broadcasted_iota(I32, (SUB, n), 1)
    row = lax.broadcasted_iota(I32, (SUB, n), 0)
    blocks = []
    for i in range(n // SUB):
        lo, hi = SUB * i, SUB * (i + 1)
        bref = b[lo:lo + 1, :]
        qi = (q[lo:hi] * (jnp.exp(b[lo:hi] - bref) * QSCALE)).astype(BF16)
        kk = (k[:hi] * jnp.exp(bref - b[:hi])).astype(BF16)
        if hi < n:
            kk = jnp.concatenate([kk, jnp.zeros((n - hi, DK), BF16)], axis=0)
        a = lax.dot_general(qi, kk, _DN_TB, preferred_element_type=F32)
        blocks.append(jnp.where(col <= row + lo, a, 0.0))
    scores = blocks[0] if len(blocks) == 1 else jnp.concatenate(blocks, axis=0)
    o = o + _dot(scores.astype(BF16), vb)
    return o, _gla_state_update(k, v, b, st)


def _mixer_back(h, hb, o_ref, conv, wm_ref, ng_ref, wgo_ref, wco_ref, wo_ref, l1g_ref, l1b_ref):
    g = _dot(hb, wm_ref[:, G0:G0 + GLA_DV])
    parts = []
    for hd in range(HEADS):
        oh = o_ref[:, hd * DV:(hd + 1) * DV]
        ms = jnp.mean(oh * oh, axis=-1, keepdims=True)
        parts.append(oh * lax.rsqrt(ms + RMS_EPS) * ng_ref[...])
    on = jnp.concatenate(parts, axis=1) * (g * _sigmoid(g))
    branch_a = _dot(on.astype(BF16), wgo_ref[...])
    mix = _sigmoid(_dot(hb, wm_ref[:, GA0:GA0 + D_MODEL])) * branch_a
    yc = _dot(hb, wm_ref[:, CB0:CB0 + D_MODEL]) * conv
    branch_b = _dot(yc.astype(BF16), wco_ref[...])
    mix = mix + _sigmoid(_dot(hb, wm_ref[:, GB0:GB0 + D_MODEL])) * branch_b
    mixed = _dot(mix.astype(BF16), wo_ref[...])
    return _layer_norm(ALPHA * h + mixed, l1g_ref[...], l1b_ref[...])


def _conv_input(hb, wm_ref):
    return _dot(hb, wm_ref[:, CC0:CC0 + D_MODEL]) * _dot(hb, wm_ref[:, CH0:CH0 + D_MODEL])


def _gla_rows(qkv_ref, b_ref, rows, hd, st):
    kc = slice(hd * DK, (hd + 1) * DK)
    return _gla_chunk(qkv_ref[rows, Q0 + hd * DK:Q0 + (hd + 1) * DK],
                      qkv_ref[rows, K0 + hd * DK:K0 + (hd + 1) * DK],
                      qkv_ref[rows, V0 + hd * DV:V0 + (hd + 1) * DV],
                      b_ref[rows, kc], st)


def _mixer_kernel(meta_ref, xp_ref, xs_ref, sgin_ref, scin_ref,
                  lng_ref, lnb_ref, wm_ref, wa_ref, wa2_ref, bg_ref,
                  ng_ref, wgo_ref, cw_ref, wco_ref, wo_ref, l1g_ref, l1b_ref,
                  h1_ref, sgp_ref, scp_ref, sgs_ref, scs_ref,
                  st_ref, zbuf_ref, qkv_ref, b_ref, o_ref, *, n_prompt_tiles, seq, streams):
    i = pl.program_id(0)
    back = functools.partial(_mixer_back, wm_ref=wm_ref, ng_ref=ng_ref, wgo_ref=wgo_ref, wco_ref=wco_ref,
                             wo_ref=wo_ref, l1g_ref=l1g_ref, l1b_ref=l1b_ref)

    def front(x):
        h = _layer_norm(x, lng_ref[...], lnb_ref[...])
        hb = h.astype(BF16)
        return h, hb, _gate_log_decay(hb, wa_ref, wa2_ref, bg_ref)

    @pl.when(i == 0)
    def _meta():
        _, hb, gk = front(meta_ref[...])
        k = _dot(hb, wm_ref[:, K0:K0 + GLA_DK])
        v = _dot(hb, wm_ref[:, V0:V0 + GLA_DV])
        b = _chunk_cumsum(gk, N_META)
        zero_state = jnp.zeros((DV, DK), F32)
        for hd in range(HEADS):
            kc = slice(hd * DK, (hd + 1) * DK)
            st_ref[hd] = _gla_state_update(k[:, kc], v[:, hd * DV:(hd + 1) * DV], b[:, kc], zero_state)
        zbuf_ref[0:8, :] = _conv_input(hb, wm_ref)[N_META - 8:N_META, :]

    @pl.when((i >= 1) & (i <= n_prompt_tiles))
    def _prompt_tile():
        h, hb, gk = front(xp_ref[...])
        qkv_ref[...] = _dot(hb, wm_ref[:, Q0:G0])
        b_ref[...] = _chunk_cumsum(gk, PROMPT_CHUNK)
        for c in range(TM // PROMPT_CHUNK):
            rows = slice(c * PROMPT_CHUNK, (c + 1) * PROMPT_CHUNK)
            for hd in range(HEADS):
                o, st_new = _gla_rows(qkv_ref, b_ref, rows, hd, st_ref[hd])
                st_ref[hd] = st_new
                o_ref[rows, hd * DV:(hd + 1) * DV] = o
        z = _conv_input(hb, wm_ref)
        zbuf_ref[8:8 + TM, :] = z
        cw = cw_ref[...]
        conv = cw[0:1, :] * zbuf_ref[6:6 + TM, :] + cw[1:2, :] * zbuf_ref[7:7 + TM, :] + cw[2:3, :] * z
        zbuf_ref[0:8, :] = z[TM - 8:TM, :]
        _rows_to_tiles(h1_ref, back(h, hb, o_ref, conv))

        @pl.when(i == n_prompt_tiles)
        def _prompt_states():
            for hd in range(HEADS):
                sgp_ref[hd] = st_ref[hd].T
            scp_ref[...] = z[TM - 2:TM, :]

    @pl.when(i > n_prompt_tiles)
    def _sample_tile():
        h, hb, gk = front(xs_ref[...])
        qkv_ref[...] = _dot(hb, wm_ref[:, Q0:G0])
        b_ref[...] = _chunk_cumsum(gk, seq)
        z = _conv_input(hb, wm_ref)
        pitch = seq + 8
        cw = cw_ref[...]
        convs = []
        for s in range(streams):
            rows = slice(s * seq, (s + 1) * seq)
            for hd in range(HEADS):
                o, st_new = _gla_rows(qkv_ref, b_ref, rows, hd, sgin_ref[s, hd].T)
                sgs_ref[s, hd] = st_new.T
                o_ref[rows, hd * DV:(hd + 1) * DV] = o
            zs = z[rows, :]
            base = s * pitch
            zbuf_ref[base + 6:base + 8, :] = scin_ref[s]
            zbuf_ref[base + 8:base + 8 + seq, :] = zs
            convs.append(cw[0:1, :] * zbuf_ref[base + 6:base + 6 + seq, :]
                         + cw[1:2, :] * zbuf_ref[base + 7:base + 7 + seq, :] + cw[2:3, :] * zs)
            scs_ref[s] = zs[seq - 2:seq, :]
        _rows_to_tiles(h1_ref, back(h, hb, o_ref, jnp.concatenate(convs, axis=0)))


def _const_spec(shape):
    nd = len(shape)
    return pl.BlockSpec(shape, lambda i, _nd=nd: (0,) * _nd, pipeline_mode=pl.Buffered(1))


def _mixer_weight_specs():
    return [
        _const_spec((1, D_MODEL)), _const_spec((1, D_MODEL)),
        _const_spec((D_MODEL, W_MAIN_COLS)),
        _const_spec((D_MODEL, RANK_PAD)), _const_spec((RANK_PAD, GLA_DK)), _const_spec((1, GLA_DK)),
        _const_spec((1, DV)), _const_spec((GLA_DV, D_MODEL)),
        _const_spec((3, D_MODEL)), _const_spec((D_MODEL, D_MODEL)),
        _const_spec((D_MODEL, D_MODEL)),
        _const_spec((1, D_MODEL)), _const_spec((1, D_MODEL)),
    ]


def _mixer(meta, x_prompt, x_sample, state_gla, state_conv, weights):
    n_streams = state_conv.shape[0]
    seq = x_sample.shape[0] // n_streams
    streams = TM // seq
    npt = x_prompt.shape[0] // TM
    nst = n_streams // streams
    ptile = lambda i: (jnp.clip(i - 1, 0, npt - 1), 0)
    stile = lambda i: jnp.clip(i - 1 - npt, 0, nst - 1)
    kern = functools.partial(_mixer_kernel, n_prompt_tiles=npt, seq=seq, streams=streams)
    return pl.pallas_call(
        kern,
        grid=(1 + npt + nst,),
        in_specs=[_const_spec((N_META, D_MODEL)),
                  pl.BlockSpec((TM, D_MODEL), ptile),
                  pl.BlockSpec((TM, D_MODEL), lambda i: (stile(i), 0), pipeline_mode=pl.Buffered(1)),
                  pl.BlockSpec((streams, HEADS, DK, DV), lambda i: (stile(i), 0, 0, 0),
                               pipeline_mode=pl.Buffered(1)),
                  pl.BlockSpec((streams, 2, D_MODEL), lambda i: (stile(i), 0, 0))] + _mixer_weight_specs(),
        out_specs=[pl.BlockSpec((TM * SUBLANES, LANES), lambda i: (jnp.maximum(i - 1, 0), 0)),
                   pl.BlockSpec((HEADS, DK, DV), lambda i: (0, 0, 0)),
                   pl.BlockSpec((2, D_MODEL), lambda i: (0, 0)),
                   pl.BlockSpec((streams, HEADS, DK, DV), lambda i: (stile(i), 0, 0, 0)),
                   pl.BlockSpec((streams, 2, D_MODEL), lambda i: (stile(i), 0, 0))],
        out_shape=[jax.ShapeDtypeStruct(((x_prompt.shape[0] + x_sample.shape[0]) * SUBLANES, LANES), F32),
                   jax.ShapeDtypeStruct((HEADS, DK, DV), F32),
                   jax.ShapeDtypeStruct((2, D_MODEL), F32),
                   jax.ShapeDtypeStruct(state_gla.shape, F32),
                   jax.ShapeDtypeStruct(state_conv.shape, F32)],
        scratch_shapes=[pltpu.VMEM((HEADS, DV, DK), F32),
                        pltpu.VMEM((max(TM + 8, streams * (seq + 8)), D_MODEL), F32),
                        pltpu.VMEM((TM, G0), F32),
                        pltpu.VMEM((TM, GLA_DK), F32),
                        pltpu.VMEM((TM, GLA_DV), F32)],
        compiler_params=pltpu.CompilerParams(dimension_semantics=("arbitrary",), vmem_limit_bytes=VMEM_LIMIT),
        name="mixer",
    )(meta, x_prompt, x_sample, state_gla, state_conv, *weights)


def _first_index_of_max(vals, ids, sentinel):
    m = jnp.max(vals, axis=0, keepdims=True)
    first = jnp.min(jnp.where(vals == m, ids, sentinel), axis=0, keepdims=True)
    return m, first


def _router_kernel(h_ref, wr_ref, bias_ref, idx_ref, w_ref, rank_ref, counts_ref):
    @pl.when(pl.program_id(0) == 0)
    def _init():
        counts_ref[...] = jnp.zeros_like(counts_ref)

    h = _tiles_to_rows(h_ref, TN_ROUTER)
    hh = h.astype(BF16)
    hl = (h - hh.astype(F32)).astype(BF16)
    wh, wl = wr_ref[0], wr_ref[1]
    dg = functools.partial(lax.dot_general, dimension_numbers=_DN_TB, preferred_element_type=F32)
    logits = dg(wh, hh) + dg(wl, hh) + dg(wh, hl)
    scores = _sigmoid(logits)
    biased = scores + bias_ref[...]
    n_tok = biased.shape[1]
    neg_inf = jnp.float32(-jnp.inf)

    eid = lax.broadcasted_iota(I32, (N_EXPERTS, n_tok), 0).astype(F32)
    lid = lax.broadcasted_iota(I32, (GROUP_SIZE, n_tok), 0).astype(F32)
    group_scores = []
    for g in range(N_GROUPS):
        blk = biased[g * GROUP_SIZE:(g + 1) * GROUP_SIZE]
        m1, first = _first_index_of_max(blk, lid, float(GROUP_SIZE))
        m2 = jnp.max(jnp.where(lid == first, neg_inf, blk), axis=0, keepdims=True)
        group_scores.append(m1 + m2)
    gsc = jnp.concatenate(group_scores, axis=0)

    gid = lax.broadcasted_iota(I32, (N_GROUPS, n_tok), 0).astype(F32)
    keep = jnp.zeros((N_GROUPS, n_tok), F32)
    cur = gsc
    for _ in range(TOPK_GROUPS):
        _, first = _first_index_of_max(cur, gid, float(N_GROUPS))
        sel = gid == first
        keep = jnp.where(sel, 1.0, keep)
        cur = jnp.where(sel, neg_inf, cur)

    masked = jnp.concatenate(
        [jnp.where(keep[g:g + 1] > 0.5, biased[g * GROUP_SIZE:(g + 1) * GROUP_SIZE], neg_inf)
         for g in range(N_GROUPS)], axis=0)
    idxs, wts, sels = [], [], []
    cur = masked
    for _ in range(TOP_K):
        _, first = _first_index_of_max(cur, eid, float(N_EXPERTS))
        sel = eid == first
        idxs.append(first)
        sels.append(sel)
        wts.append(jnp.sum(jnp.where(sel, scores, 0.0), axis=0, keepdims=True))
        cur = jnp.where(sel, neg_inf, cur)
    w = jnp.concatenate(wts, axis=0)
    idx_ref[...] = jnp.concatenate(idxs, axis=0).astype(I32)
    w_ref[...] = w / jnp.sum(w, axis=0, keepdims=True) * ROUTED_SCALE

    chosen = jnp.where(sels[0], 1.0, 0.0)
    for sel in sels[1:]:
        chosen = chosen + jnp.where(sel, 1.0, 0.0)
    t_src = lax.broadcasted_iota(I32, (n_tok, n_tok), 0)
    t_dst = lax.broadcasted_iota(I32, (n_tok, n_tok), 1)
    earlier = jnp.where(t_src < t_dst, 1.0, 0.0).astype(BF16)
    arrivals = counts_ref[...] + _dot(chosen.astype(BF16), earlier)
    rank_ref[...] = jnp.concatenate(
        [jnp.sum(jnp.where(sel, arrivals, 0.0), axis=0, keepdims=True) for sel in sels], axis=0).astype(I32)
    counts_ref[...] += jnp.sum(chosen, axis=1, keepdims=True)


def _router(h1_tiles, wr_split, bias_col):
    n = h1_tiles.shape[0] // SUBLANES
    tok = pl.BlockSpec((TOP_K, TN_ROUTER), lambda i: (0, i))
    return pl.pallas_call(
        _router_kernel,
        grid=(n // TN_ROUTER,),
        in_specs=[pl.BlockSpec((TN_ROUTER * SUBLANES, LANES), lambda i: (i, 0)),
                  pl.BlockSpec((2, N_EXPERTS, D_MODEL), lambda i: (0, 0, 0)),
                  pl.BlockSpec((N_EXPERTS, 1), lambda i: (0, 0))],
        out_specs=[tok, tok, tok, pl.BlockSpec((N_EXPERTS, 1), lambda i: (0, 0))],
        out_shape=[jax.ShapeDtypeStruct((TOP_K, n), I32), jax.ShapeDtypeStruct((TOP_K, n), F32),
                   jax.ShapeDtypeStruct((TOP_K, n), I32), jax.ShapeDtypeStruct((N_EXPERTS, 1), F32)],
        compiler_params=pltpu.CompilerParams(dimension_semantics=("arbitrary",), vmem_limit_bytes=VMEM_LIMIT),
        name="router",
    )(h1_tiles, wr_split, bias_col)


def _positions_kernel(idx_ref, rank_ref, segx_ref, segy_ref, posx_ref, posy_ref):
    n_tok = idx_ref.shape[1]
    eid = lax.broadcasted_iota(I32, (N_EXPERTS, n_tok), 0)
    segx, segy = segx_ref[...], segy_ref[...]
    rows_x, rows_y = [], []
    for k in range(TOP_K):
        onehot = eid == idx_ref[k:k + 1, :]
        rows_x.append(jnp.sum(jnp.where(onehot, segx, 0.0), axis=0, keepdims=True))
        rows_y.append(jnp.sum(jnp.where(onehot, segy, 0.0), axis=0, keepdims=True))
    posx_ref[...] = jnp.concatenate(rows_x, axis=0).astype(I32) + rank_ref[...]
    posy_ref[...] = jnp.concatenate(rows_y, axis=0).astype(I32) + rank_ref[...]


def _positions(idx_t, rank_t, segx_col, segy_col):
    n = idx_t.shape[1]
    tok = pl.BlockSpec((TOP_K, TN_ROUTER), lambda i: (0, i))
    table = pl.BlockSpec((N_EXPERTS, 1), lambda i: (0, 0))
    return pl.pallas_call(
        _positions_kernel,
        grid=(n // TN_ROUTER,),
        in_specs=[tok, tok, table, table],
        out_specs=[tok, tok],
        out_shape=[jax.ShapeDtypeStruct((TOP_K, n), I32), jax.ShapeDtypeStruct((TOP_K, n), I32)],
        compiler_params=pltpu.CompilerParams(dimension_semantics=("arbitrary",)),
        name="positions",
    )(idx_t, rank_t, segx_col, segy_col)


def _tile_copy(src, src_tile, dst, dst_tile, sem):
    return pltpu.make_async_copy(src.at[pl.ds(pl.multiple_of(src_tile * SUBLANES, SUBLANES), SUBLANES)],
                                 dst.at[pl.ds(pl.multiple_of(dst_tile * SUBLANES, SUBLANES), SUBLANES)], sem)


def _chunk_rows(first_tile):
    return pl.ds(pl.multiple_of(first_tile * SUBLANES, SUBLANES), CH * SUBLANES)


def _dispatch_kernel(pos_ref, h_ref, x_hbm, zeros, sem, zsem, *, n_assign):
    @pl.when(pl.program_id(0) == 0)
    def _zero_slack():
        zeros[...] = jnp.zeros_like(zeros)
        fill = pltpu.make_async_copy(zeros, x_hbm.at[_chunk_rows(n_assign)], zsem)
        fill.start()
        fill.wait()

    def body(t, carry):
        for k in range(TOP_K):
            _tile_copy(h_ref, t, x_hbm, pos_ref[0, 0, k * TN_DISPATCH + t], sem.at[k]).start(priority=k % 2)
        return carry
    lax.fori_loop(0, TN_DISPATCH, body, 0, unroll=8)
    for k in range(TOP_K):
        pltpu.make_async_copy(h_ref, x_hbm.at[pl.ds(0, TN_DISPATCH * SUBLANES)], sem.at[k]).wait()


def _dispatch(pos_tiles, h1_tiles):
    n_t = pos_tiles.shape[0]
    n_assign = n_t * TN_DISPATCH * TOP_K
    return pl.pallas_call(
        functools.partial(_dispatch_kernel, n_assign=n_assign),
        grid=(n_t,),
        in_specs=[pl.BlockSpec((1, 1, TOP_K * TN_DISPATCH), lambda t: (t, 0, 0), memory_space=pltpu.SMEM),
                  pl.BlockSpec((TN_DISPATCH * SUBLANES, LANES), lambda t: (t, 0))],
        out_specs=pl.BlockSpec(memory_space=pl.ANY),
        out_shape=jax.ShapeDtypeStruct(((n_assign + CH) * SUBLANES, LANES), F32),
        scratch_shapes=[pltpu.VMEM((CH * SUBLANES, LANES), F32), pltpu.SemaphoreType.DMA((TOP_K,)),
                        pltpu.SemaphoreType.DMA(())],
        compiler_params=pltpu.CompilerParams(dimension_semantics=("arbitrary",), has_side_effects=True),
        name="dispatch",
    )(pos_tiles, h1_tiles)


def _expert_kernel(gfirst_ref, cnt_ref, seq_ref, nxt_ref, gtot_ref, src_ref, x_hbm, wg_hbm, wu_hbm, wd_hbm, y_hbm,
                   xbuf, ybuf, zbuf, wg32, wu32, wd32, wgb, wub, wdb, xsem, ysem, zsem, wsem, *, g_max):
    e = pl.program_id(0)
    cnt = cnt_ref[e]
    g_total = gtot_ref[0]

    def x_copy(g, slot):
        return pltpu.make_async_copy(x_hbm.at[_chunk_rows(src_ref[g])], xbuf.at[slot], xsem.at[slot])

    def y_copy(g, slot):
        return pltpu.make_async_copy(ybuf.at[slot], y_hbm.at[_chunk_rows(g * CH)], ysem.at[slot])

    def tail_fill(g):
        return pltpu.make_async_copy(zbuf, y_hbm.at[_chunk_rows(g * CH)], zsem)

    def for_tail_chunks(fn):
        def body(g, carry):
            fn(g)
            return carry
        lax.fori_loop(g_total, g_max, body, 0)

    def weight_copies(ex, slot):
        return (pltpu.make_async_copy(wg_hbm.at[ex], wg32.at[slot], wsem.at[slot]),
                pltpu.make_async_copy(wu_hbm.at[ex], wu32.at[slot], wsem.at[slot]),
                pltpu.make_async_copy(wd_hbm.at[ex], wd32.at[slot], wsem.at[slot]))

    @pl.when(e == 0)
    def _prime():
        for g in range(X_AHEAD):
            @pl.when(g < g_total)
            def _():
                x_copy(g, g).start()
        zbuf[...] = jnp.zeros_like(zbuf)
        for_tail_chunks(lambda g: tail_fill(g).start())

    @pl.when(cnt > 0)
    def _expert():
        seq, nxt = seq_ref[e], nxt_ref[e]
        wslot = lax.rem(seq, 2)

        @pl.when(seq == 0)
        def _fetch_own():
            for c in weight_copies(e, wslot):
                c.start(priority=1)

        @pl.when(nxt >= 0)
        def _prefetch_weights():
            for c in weight_copies(nxt, 1 - wslot):
                c.start(priority=1)

        for c in weight_copies(e, wslot):
            c.wait()
        wgb[...] = wg32[wslot].astype(BF16)
        wub[...] = wu32[wslot].astype(BF16)
        wdb[...] = wd32[wslot].astype(BF16)

        def process(g, width):
            for i in range(width):
                x_copy(0, lax.rem(g + i, NX)).wait()
            for i in range(width):
                ahead = g + i + X_AHEAD

                @pl.when(ahead < g_total)
                def _fetch_ahead():
                    x_copy(ahead, lax.rem(ahead, NX)).start()

            xb = jnp.concatenate([_tiles_to_rows(xbuf.at[lax.rem(g + i, NX)], CH) for i in range(width)],
                                 axis=0).astype(BF16)
            gate = _dot(xb, wgb[...])
            up = _dot(xb, wub[...])
            res = _dot(((gate * _sigmoid(gate)) * up).astype(BF16), wdb[...])
            for i in range(width):
                ys = lax.rem(g + i, NY)

                @pl.when(g + i >= NY)
                def _slot_written_back():
                    y_copy(0, ys).wait()

                _rows_to_tiles(ybuf.at[ys], res[i * CH:(i + 1) * CH])
                y_copy(g + i, ys).start()

        g_first = gfirst_ref[e]
        n_chunks = (cnt + CH - 1) // CH
        n_wide = n_chunks // GROUP
        rest = n_chunks - n_wide * GROUP

        def wide_group(j, carry):
            process(g_first + j * GROUP, GROUP)
            return carry

        lax.fori_loop(0, n_wide, wide_group, 0)
        g_rest = g_first + n_wide * GROUP
        width = GROUP // 2
        while width >= 1:
            @pl.when((rest & width) != 0)
            def _narrow_group(width=width):
                process(g_rest + (rest & ~(2 * width - 1)), width)
            width //= 2

    @pl.when(e == pl.num_programs(0) - 1)
    def _finish():
        for back in range(1, NY + 1):
            @pl.when(g_total - back >= 0)
            def _():
                y_copy(0, lax.rem(g_total - back, NY)).wait()

        for_tail_chunks(lambda g: tail_fill(0).wait())


def _max_chunks(n_assign):
    return n_assign // CH + N_EXPERTS


def _segments(counts, n_assign):
    n_chunks = (counts + CH - 1) // CH
    g_end = jnp.cumsum(n_chunks).astype(I32)
    g_first = g_end - n_chunks
    x_start = (jnp.cumsum(counts) - counts).astype(I32)
    chunk = jnp.arange(_max_chunks(n_assign), dtype=I32)
    eid = jnp.arange(N_EXPERTS, dtype=I32)
    owner = jnp.minimum(jnp.sum((g_end[None, :] <= chunk[:, None]).astype(I32), axis=1), N_EXPERTS - 1)
    onehot = owner[:, None] == eid[None, :]
    pick = lambda v: jnp.sum(jnp.where(onehot, v[None, :], 0), axis=1)
    src = jnp.minimum(pick(x_start) + (chunk - pick(g_first)) * CH, n_assign).astype(I32)
    has_rows = counts > 0
    seq = jnp.where(has_rows, jnp.cumsum(has_rows.astype(I32)) - 1, -1).astype(I32)
    nxt = jnp.min(jnp.where((eid[None, :] > eid[:, None]) & has_rows[None, :], eid[None, :], N_EXPERTS), axis=1)
    nxt = jnp.where(nxt == N_EXPERTS, -1, nxt).astype(I32)
    return dict(x_start=x_start, y_start=g_first * CH, g_first=g_first, g_total=g_end[-1:], src=src,
                seq=seq, nxt=nxt, counts=counts)


def _experts(seg, x_sorted, w_gate, w_up, w_down, n_assign):
    g_max = _max_chunks(n_assign)
    hbm = pl.BlockSpec(memory_space=pl.ANY)
    grid_spec = pltpu.PrefetchScalarGridSpec(
        num_scalar_prefetch=6,
        grid=(N_EXPERTS,),
        in_specs=[hbm, hbm, hbm, hbm],
        out_specs=hbm,
        scratch_shapes=[pltpu.VMEM((NX, CH * SUBLANES, LANES), F32),
                        pltpu.VMEM((NY, CH * SUBLANES, LANES), F32),
                        pltpu.VMEM((CH * SUBLANES, LANES), F32),
                        pltpu.VMEM((2, D_MODEL, EXPERT_HIDDEN), F32),
                        pltpu.VMEM((2, D_MODEL, EXPERT_HIDDEN), F32),
                        pltpu.VMEM((2, EXPERT_HIDDEN, D_MODEL), F32),
                        pltpu.VMEM((D_MODEL, EXPERT_HIDDEN), BF16),
                        pltpu.VMEM((D_MODEL, EXPERT_HIDDEN), BF16),
                        pltpu.VMEM((EXPERT_HIDDEN, D_MODEL), BF16),
                        pltpu.SemaphoreType.DMA((NX,)), pltpu.SemaphoreType.DMA((NY,)),
                        pltpu.SemaphoreType.DMA(()), pltpu.SemaphoreType.DMA((2,))],
    )
    return pl.pallas_call(
        functools.partial(_expert_kernel, g_max=g_max),
        grid_spec=grid_spec,
        out_shape=jax.ShapeDtypeStruct((g_max * CH * SUBLANES, LANES), F32),
        compiler_params=pltpu.CompilerParams(dimension_semantics=("arbitrary",), vmem_limit_bytes=VMEM_LIMIT,
                                             has_side_effects=True),
        name="experts",
    )(seg["g_first"], seg["counts"], seg["seq"], seg["nxt"], seg["g_total"], seg["src"],
      x_sorted, w_gate, w_up, w_down)


def _combine_kernel(pos_cur_ref, pos_nxt_ref, y_hbm, h_ref, w_ref, wsg_ref, wsu_ref, wsd_ref, l2g_ref, l2b_ref,
                    outp_ref, outs_ref, ybuf, sem, *, n_prompt_tiles):
    t = pl.program_id(0)
    n_t = pl.num_programs(0)
    slot = t % 2

    def gather_start(pos_ref, dst, dst_sem):
        def body(r, carry):
            for k in range(TOP_K):
                j = k * TN_COMBINE + r
                _tile_copy(y_hbm, pos_ref[0, 0, j], dst, j, dst_sem).start(priority=k % 2)
            return carry
        lax.fori_loop(0, TN_COMBINE, body, 0, unroll=8)

    @pl.when(t == 0)
    def _first():
        gather_start(pos_cur_ref, ybuf.at[0], sem.at[0])

    @pl.when(t + 1 < n_t)
    def _prefetch():
        gather_start(pos_nxt_ref, ybuf.at[1 - slot], sem.at[1 - slot])

    h = _tiles_to_rows(h_ref, TN_COMBINE)
    hb = h.astype(BF16)
    g = _dot(hb, wsg_ref[...])
    u = _dot(hb, wsu_ref[...])
    shared = _dot(((g * _sigmoid(g)) * u).astype(BF16), wsd_ref[...])

    yslot = ybuf.at[slot]
    pltpu.make_async_copy(y_hbm.at[pl.ds(0, TOP_K * TN_COMBINE * SUBLANES)], yslot, sem.at[slot]).wait()
    w = w_ref[...]
    wk = [jnp.broadcast_to(w[:, k:k + 1], (TN_COMBINE, LANES)) for k in range(TOP_K)]
    chunks = []
    for s in range(SUBLANES):
        acc = wk[0] * yslot[pl.ds(s, TN_COMBINE, stride=SUBLANES), :]
        for k in range(1, TOP_K):
            acc = acc + wk[k] * yslot[pl.ds(k * TN_COMBINE * SUBLANES + s, TN_COMBINE, stride=SUBLANES), :]
        chunks.append(acc)
    routed = jnp.concatenate(chunks, axis=1)
    out = _layer_norm(ALPHA * h + (routed + shared), l2g_ref[...], l2b_ref[...])

    @pl.when(t < n_prompt_tiles)
    def _prompt_rows():
        outp_ref[...] = out

    @pl.when(t >= n_prompt_tiles)
    def _sample_rows():
        outs_ref[...] = out


def _combine(pos_tiles, y_sorted, h1_tiles, w_tok, wsg, wsu, wsd, l2g, l2b, n_prompt):
    n = h1_tiles.shape[0] // SUBLANES
    n_t = n // TN_COMBINE
    npt = n_prompt // TN_COMBINE
    n_idx = TOP_K * TN_COMBINE
    cspec = lambda shape: pl.BlockSpec(shape, lambda t: (0,) * len(shape))
    return pl.pallas_call(
        functools.partial(_combine_kernel, n_prompt_tiles=npt),
        grid=(n_t,),
        in_specs=[pl.BlockSpec((1, 1, n_idx), lambda t: (t, 0, 0), memory_space=pltpu.SMEM),
                  pl.BlockSpec((1, 1, n_idx), lambda t: (jnp.minimum(t + 1, n_t - 1), 0, 0),
                               memory_space=pltpu.SMEM),
                  pl.BlockSpec(memory_space=pl.ANY),
                  pl.BlockSpec((TN_COMBINE * SUBLANES, LANES), lambda t: (t, 0)),
                  pl.BlockSpec((TN_COMBINE, TOP_K), lambda t: (t, 0)),
                  cspec((D_MODEL, SHARED_HIDDEN)), cspec((D_MODEL, SHARED_HIDDEN)), cspec((SHARED_HIDDEN, D_MODEL)),
                  cspec((1, D_MODEL)), cspec((1, D_MODEL))],
        out_specs=[pl.BlockSpec((TN_COMBINE, D_MODEL), lambda t: (jnp.minimum(t, npt - 1), 0)),
                   pl.BlockSpec((TN_COMBINE, D_MODEL), lambda t: (jnp.maximum(t - npt, 0), 0))],
        out_shape=[jax.ShapeDtypeStruct((n_prompt, D_MODEL), F32),
                   jax.ShapeDtypeStruct((n - n_prompt, D_MODEL), F32)],
        scratch_shapes=[pltpu.VMEM((2, n_idx * SUBLANES, LANES), F32), pltpu.SemaphoreType.DMA((2,))],
        compiler_params=pltpu.CompilerParams(dimension_semantics=("arbitrary",), vmem_limit_bytes=VMEM_LIMIT),
        name="combine",
    )(pos_tiles, pos_tiles, y_sorted, h1_tiles, w_tok, wsg, wsu, wsd, l2g, l2b)


def _pack_mixer_weights(ln_emb_g, ln_emb_b, w_in, w_gate_a2, b_gate, gla_norm_g, w_gla_out, conv_w,
                        w_conv_out, w_o, ln1_g, ln1_b):
    q, k, v, g, a, cb, cc, ch, ga, gb = jnp.split(
        w_in, [512, 1024, 2048, 3072, 3088, 4112, 5136, 6160, 7184], axis=-1)
    w_main = jnp.concatenate([q, k, v, g, cb, cc, ch, ga, gb], axis=-1).astype(BF16)
    w_a = jnp.pad(a, ((0, 0), (0, RANK_PAD - GATE_RANK))).astype(BF16)
    w_a2 = jnp.pad(w_gate_a2, ((0, RANK_PAD - GATE_RANK), (0, 0))).astype(BF16)
    row = lambda x: x.reshape(1, -1).astype(F32)
    return [row(ln_emb_g), row(ln_emb_b), w_main, w_a, w_a2, row(b_gate), row(gla_norm_g),
            w_gla_out.astype(BF16), conv_w.astype(F32), w_conv_out.astype(BF16), w_o.astype(BF16),
            row(ln1_g), row(ln1_b)]


def kernel(x_prompt, x_sample, state_gla, state_conv, meta_tokens, ln_emb_g, ln_emb_b, w_in, w_gate_a2, b_gate, gla_norm_g, w_gla_out, conv_w, w_conv_out, w_o, ln1_g, ln1_b, w_router, router_bias, w_exp_gate, w_exp_up, w_exp_down, w_sh_gate, w_sh_up, w_sh_down, ln2_g, ln2_b):
    batch, seq, _ = x_prompt.shape
    dec_batch, dec_seq, _ = x_sample.shape
    depth = w_in.shape[0]
    assert batch == 1 and depth == 1 and seq % TM == 0 and TM % dec_seq == 0 and dec_seq % SUB == 0
    n_prompt, n_sample = batch * seq, dec_batch * dec_seq
    n_total = n_prompt + n_sample
    assert n_sample % TM == 0 and n_total % TN_ROUTER == 0
    assert n_prompt % TN_COMBINE == 0 and n_sample % TN_COMBINE == 0 and n_total % TN_DISPATCH == 0
    n_assign = n_total * TOP_K

    weights = _pack_mixer_weights(ln_emb_g, ln_emb_b, w_in[0], w_gate_a2[0], b_gate[0], gla_norm_g[0],
                                  w_gla_out[0], conv_w[0], w_conv_out[0], w_o[0], ln1_g[0], ln1_b[0])
    h1, sg_p, sc_p, sg_s, sc_s = _mixer(meta_tokens.astype(F32), x_prompt.reshape(n_prompt, D_MODEL),
                                        x_sample.reshape(n_sample, D_MODEL), state_gla[0], state_conv[0], weights)

    wr_t = w_router[0].T
    wr_hi = wr_t.astype(BF16)
    wr_split = jnp.stack([wr_hi, (wr_t - wr_hi.astype(F32)).astype(BF16)])
    idx_t, w_t, rank_t, counts = _router(h1, wr_split, router_bias[0].reshape(N_EXPERTS, 1).astype(F32))

    seg = _segments(counts.reshape(N_EXPERTS).astype(I32), n_assign)
    table = lambda v: v.astype(F32).reshape(N_EXPERTS, 1)
    posx_t, posy_t = _positions(idx_t, rank_t, table(seg["x_start"]), table(seg["y_start"]))
    tiles = lambda p, tn: p.reshape(TOP_K, n_total // tn, tn).transpose(1, 0, 2).reshape(n_total // tn, 1, -1)

    x_sorted = _dispatch(tiles(posx_t, TN_DISPATCH), h1)
    y_sorted = _experts(seg, x_sorted, w_exp_gate[0], w_exp_up[0], w_exp_down[0], n_assign)
    y_prompt, y_sample = _combine(tiles(posy_t, TN_COMBINE), y_sorted, h1, w_t.T,
                                  w_sh_gate[0].astype(BF16), w_sh_up[0].astype(BF16), w_sh_down[0].astype(BF16),
                                  ln2_g[0].reshape(1, -1), ln2_b[0].reshape(1, -1), n_prompt)
    y_prompt = y_prompt.reshape(batch, seq, D_MODEL)
    y_sample = y_sample.reshape(dec_batch, dec_seq, D_MODEL)
    return (y_prompt, y_sample,
            sg_p.reshape(depth, batch, HEADS, DK, DV), sc_p.reshape(depth, batch, 2, D_MODEL),
            sg_s.reshape(depth, dec_batch, HEADS, DK, DV), sc_s.reshape(depth, dec_batch, 2, D_MODEL))
```

```python
import functools

import jax
import jax.numpy as jnp
from jax import lax
from jax.experimental import pallas as pl
from jax.experimental.pallas import tpu as pltpu

F32 = jnp.float32
BF16 = jnp.bfloat16
I32 = jnp.int32

D_MODEL = 1024
N_META = 16
HEADS = 4
DK = 128
DV = 256
GLA_DK = HEADS * DK
GLA_DV = HEADS * DV
GATE_RANK = 16
GATE_TAU = 16.0
N_EXPERTS = 256
TOP_K = 8
N_GROUPS = 8
GROUP_SIZE = N_EXPERTS // N_GROUPS
TOPK_GROUPS = 4
EXPERT_HIDDEN = 256
SHARED_HIDDEN = 256
ROUTED_SCALE = 2.5
LN_EPS = 1e-5
RMS_EPS = 1e-6
ALPHA = 2.0 ** 0.25
QSCALE = DK ** -0.5

Q0, K0, V0, G0, CB0, CC0, CH0, GA0, GB0 = 0, 512, 1024, 2048, 3072, 4096, 5120, 6144, 7168
W_MAIN_COLS = 8192
RANK_PAD = 128

SUB = 16
PROMPT_CHUNK = 64
TM = 256
TN_ROUTER = 512
TN_DISPATCH = 256
CH = 64
GROUP = 8
X_AHEAD = 12
assert GROUP & (GROUP - 1) == 0
NX = X_AHEAD + GROUP
NY = 2 * GROUP
TN_COMBINE = 256
VMEM_LIMIT = 60 * 1024 * 1024
SUBLANES = 8
LANES = 128
assert D_MODEL == SUBLANES * LANES

_DN_TB = (((1,), (1,)), ((), ()))
_DN_TA = (((0,), (0,)), ((), ()))


def _dot(a, b):
    return jnp.dot(a, b, preferred_element_type=F32)


def _tiles_to_rows(ref, n, base=0):
    return jnp.concatenate(
        [ref[pl.ds(SUBLANES * base + s, n, stride=SUBLANES), :] for s in range(SUBLANES)], axis=1)


def _rows_to_tiles(ref, val):
    n = val.shape[0]
    for s in range(SUBLANES):
        ref[pl.ds(s, n, stride=SUBLANES), :] = val[:, s * LANES:(s + 1) * LANES]


def _layer_norm(x, g, b):
    mu = jnp.mean(x, axis=-1, keepdims=True)
    xc = x - mu
    var = jnp.mean(xc * xc, axis=-1, keepdims=True)
    return xc * lax.rsqrt(var + LN_EPS) * g + b


def _sigmoid(x):
    return 1.0 / (1.0 + jnp.exp(-x))


def _log_sigmoid(x):
    return jnp.minimum(x, 0.0) - jnp.log1p(jnp.exp(-jnp.abs(x)))


def _chunk_cumsum(g, chunk):
    t = g.shape[0]
    shift = chunk.bit_length() - 1
    r = lax.broadcasted_iota(I32, (t, t), 0)
    c = lax.broadcasted_iota(I32, (t, t), 1)
    tri = jnp.where(((r >> shift) == (c >> shift)) & (c <= r), 1.0, 0.0).astype(BF16)
    g1 = g.astype(BF16)
    r1 = g - g1.astype(F32)
    g2 = r1.astype(BF16)
    g3 = (r1 - g2.astype(F32)).astype(BF16)
    return _dot(tri, g1) + _dot(tri, g2) + _dot(tri, g3)


def _gate_log_decay(hb, wa_ref, wa2_ref, bg_ref):
    a = _dot(hb, wa_ref[...])
    return _log_sigmoid(_dot(a.astype(BF16), wa2_ref[...]) + bg_ref[...]) * (1.0 / GATE_TAU)


def _gla_state_update(k, v, b, st):
    n = k.shape[0]
    b_last = b[n - 1:n, :]
    khat = (k * jnp.exp(b_last - b)).astype(BF16)
    return st * jnp.exp(b_last) + lax.dot_general(v.astype(BF16), khat, _DN_TA, preferred_element_type=F32)


def _gla_chunk(q, k, v, b, st):
    n = q.shape[0]
    vb = v.astype(BF16)
    q0 = (q * (jnp.exp(b) * QSCALE)).astype(BF16)
    o = lax.dot_general(q0, st.astype(BF16), _DN_TB, preferred_element_type=F32)
    col = lax.broadcasted_iota(I32, (SUB, n), 1)
    row = lax.broadcasted_iota(I32, (SUB, n), 0)
    blocks = []
    for i in range(n // SUB):
        lo, hi = SUB * i, SUB * (i + 1)
        bref = b[lo:lo + 1, :]
        qi = (q[lo:hi] * (jnp.exp(b[lo:hi] - bref) * QSCALE)).astype(BF16)
        kk = (k[:hi] * jnp.exp(bref - b[:hi])).astype(BF16)
        if hi < n:
            kk = jnp.concatenate([kk, jnp.zeros((n - hi, DK), BF16)], axis=0)
        a = lax.dot_general(qi, kk, _DN_TB, preferred_element_type=F32)
        blocks.append(jnp.where(col <= row + lo, a, 0.0))
    scores = blocks[0] if len(blocks) == 1 else jnp.concatenate(blocks, axis=0)
    o = o + _dot(scores.astype(BF16), vb)
    return o, _gla_state_update(k, v, b, st)


def _mixer_back(h, hb, o_ref, conv, wm_ref, ng_ref, wgo_ref, wco_ref, wo_ref, l1g_ref, l1b_ref):
    g = _dot(hb, wm_ref[:, G0:G0 + GLA_DV])
    parts = []
    for hd in range(HEADS):
        oh = o_ref[:, hd * DV:(hd + 1) * DV]
        ms = jnp.mean(oh * oh, axis=-1, keepdims=True)
        parts.append(oh * lax.rsqrt(ms + RMS_EPS) * ng_ref[...])
    on = jnp.concatenate(parts, axis=1) * (g * _sigmoid(g))
    branch_a = _dot(on.astype(BF16), wgo_ref[...])
    mix = _sigmoid(_dot(hb, wm_ref[:, GA0:GA0 + D_MODEL])) * branch_a
    yc = _dot(hb, wm_ref[:, CB0:CB0 + D_MODEL]) * conv
    branch_b = _dot(yc.astype(BF16), wco_ref[...])
    mix = mix + _sigmoid(_dot(hb, wm_ref[:, GB0:GB0 + D_MODEL])) * branch_b
    mixed = _dot(mix.astype(BF16), wo_ref[...])
    return _layer_norm(ALPHA * h + mixed, l1g_ref[...], l1b_ref[...])


def _conv_input(hb, wm_ref):
    return _dot(hb, wm_ref[:, CC0:CC0 + D_MODEL]) * _dot(hb, wm_ref[:, CH0:CH0 + D_MODEL])


def _gla_rows(qkv_ref, b_ref, rows, hd, st):
    kc = slice(hd * DK, (hd + 1) * DK)
    return _gla_chunk(qkv_ref[rows, Q0 + hd * DK:Q0 + (hd + 1) * DK],
                      qkv_ref[rows, K0 + hd * DK:K0 + (hd + 1) * DK],
                      qkv_ref[rows, V0 + hd * DV:V0 + (hd + 1) * DV],
                      b_ref[rows, kc], st)


def _mixer_kernel(meta_ref, xp_ref, xs_ref, sgin_ref, scin_ref,
                  lng_ref, lnb_ref, wm_ref, wa_ref, wa2_ref, bg_ref,
                  ng_ref, wgo_ref, cw_ref, wco_ref, wo_ref, l1g_ref, l1b_ref,
                  h1_ref, sgp_ref, scp_ref, sgs_ref, scs_ref,
                  st_ref, zbuf_ref, qkv_ref, b_ref, o_ref, *, n_prompt_tiles, seq, streams):
    i = pl.program_id(0)
    back = functools.partial(_mixer_back, wm_ref=wm_ref, ng_ref=ng_ref, wgo_ref=wgo_ref, wco_ref=wco_ref,
                             wo_ref=wo_ref, l1g_ref=l1g_ref, l1b_ref=l1b_ref)

    def front(x):
        h = _layer_norm(x, lng_ref[...], lnb_ref[...])
        hb = h.astype(BF16)
        return h, hb, _gate_log_decay(hb, wa_ref, wa2_ref, bg_ref)

    @pl.when(i == 0)
    def _meta():
        _, hb, gk = front(meta_ref[...])
        k = _dot(hb, wm_ref[:, K0:K0 + GLA_DK])
        v = _dot(hb, wm_ref[:, V0:V0 + GLA_DV])
        b = _chunk_cumsum(gk, N_META)
        zero_state = jnp.zeros((DV, DK), F32)
        for hd in range(HEADS):
            kc = slice(hd * DK, (hd + 1) * DK)
            st_ref[hd] = _gla_state_update(k[:, kc], v[:, hd * DV:(hd + 1) * DV], b[:, kc], zero_state)
        zbuf_ref[0:8, :] = _conv_input(hb, wm_ref)[N_META - 8:N_META, :]

    @pl.when((i >= 1) & (i <= n_prompt_tiles))
    def _prompt_tile():
        h, hb, gk = front(xp_ref[...])
        qkv_ref[...] = _dot(hb, wm_ref[:, Q0:G0])
        b_ref[...] = _chunk_cumsum(gk, PROMPT_CHUNK)
        for c in range(TM // PROMPT_CHUNK):
            rows = slice(c * PROMPT_CHUNK, (c + 1) * PROMPT_CHUNK)
            for hd in range(HEADS):
                o, st_new = _gla_rows(qkv_ref, b_ref, rows, hd, st_ref[hd])
                st_ref[hd] = st_new
                o_ref[rows, hd * DV:(hd + 1) * DV] = o
        z = _conv_input(hb, wm_ref)
        zbuf_ref[8:8 + TM, :] = z
        cw = cw_ref[...]
        conv = cw[0:1, :] * zbuf_ref[6:6 + TM, :] + cw[1:2, :] * zbuf_ref[7:7 + TM, :] + cw[2:3, :] * z
        zbuf_ref[0:8, :] = z[TM - 8:TM, :]
        _rows_to_tiles(h1_ref, back(h, hb, o_ref, conv))

        @pl.when(i == n_prompt_tiles)
        def _prompt_states():
            for hd in range(HEADS):
                sgp_ref[hd] = st_ref[hd].T
            scp_ref[...] = z[TM - 2:TM, :]

    @pl.when(i > n_prompt_tiles)
    def _sample_tile():
        h, hb, gk = front(xs_ref[...])
        qkv_ref[...] = _dot(hb, wm_ref[:, Q0:G0])
        b_ref[...] = _chunk_cumsum(gk, seq)
        z = _conv_input(hb, wm_ref)
        pitch = seq + 8
        cw = cw_ref[...]
        convs = []
        for s in range(streams):
            rows = slice(s * seq, (s + 1) * seq)
            for hd in range(HEADS):
                o, st_new = _gla_rows(qkv_ref, b_ref, rows, hd, sgin_ref[s, hd].T)
                sgs_ref[s, hd] = st_new.T
                o_ref[rows, hd * DV:(hd + 1) * DV] = o
            zs = z[rows, :]
            base = s * pitch
            zbuf_ref[base + 6:base + 8, :] = scin_ref[s]
            zbuf_ref[base + 8:base + 8 + seq, :] = zs
            convs.append(cw[0:1, :] * zbuf_ref[base + 6:base + 6 + seq, :]
                         + cw[1:2, :] * zbuf_ref[base + 7:base + 7 + seq, :] + cw[2:3, :] * zs)
            scs_ref[s] = zs[seq - 2:seq, :]
        _rows_to_tiles(h1_ref, back(h, hb, o_ref, jnp.concatenate(convs, axis=0)))


def _const_spec(shape):
    nd = len(shape)
    return pl.BlockSpec(shape, lambda i, _nd=nd: (0,) * _nd, pipeline_mode=pl.Buffered(1))


def _mixer_weight_specs():
    return [
        _const_spec((1, D_MODEL)), _const_spec((1, D_MODEL)),
        _const_spec((D_MODEL, W_MAIN_COLS)),
        _const_spec((D_MODEL, RANK_PAD)), _const_spec((RANK_PAD, GLA_DK)), _const_spec((1, GLA_DK)),
        _const_spec((1, DV)), _const_spec((GLA_DV, D_MODEL)),
        _const_spec((3, D_MODEL)), _const_spec((D_MODEL, D_MODEL)),
        _const_spec((D_MODEL, D_MODEL)),
        _const_spec((1, D_MODEL)), _const_spec((1, D_MODEL)),
    ]


def _mixer(meta, x_prompt, x_sample, state_gla, state_conv, weights):
    n_streams = state_conv.shape[0]
    seq = x_sample.shape[0] // n_streams
    streams = TM // seq
    npt = x_prompt.shape[0] // TM
    nst = n_streams // streams
    ptile = lambda i: (jnp.clip(i - 1, 0, npt - 1), 0)
    stile = lambda i: jnp.clip(i - 1 - npt, 0, nst - 1)
    kern = functools.partial(_mixer_kernel, n_prompt_tiles=npt, seq=seq, streams=streams)
    return pl.pallas_call(
        kern,
        grid=(1 + npt + nst,),
        in_specs=[_const_spec((N_META, D_MODEL)),
                  pl.BlockSpec((TM, D_MODEL), ptile),
                  pl.BlockSpec((TM, D_MODEL), lambda i: (stile(i), 0), pipeline_mode=pl.Buffered(1)),
                  pl.BlockSpec((streams, HEADS, DK, DV), lambda i: (stile(i), 0, 0, 0),
                               pipeline_mode=pl.Buffered(1)),
                  pl.BlockSpec((streams, 2, D_MODEL), lambda i: (stile(i), 0, 0))] + _mixer_weight_specs(),
        out_specs=[pl.BlockSpec((TM * SUBLANES, LANES), lambda i: (jnp.maximum(i - 1, 0), 0)),
                   pl.BlockSpec((HEADS, DK, DV), lambda i: (0, 0, 0)),
                   pl.BlockSpec((2, D_MODEL), lambda i: (0, 0)),
                   pl.BlockSpec((streams, HEADS, DK, DV), lambda i: (stile(i), 0, 0, 0)),
                   pl.BlockSpec((streams, 2, D_MODEL), lambda i: (stile(i), 0, 0))],
        out_shape=[jax.ShapeDtypeStruct(((x_prompt.shape[0] + x_sample.shape[0]) * SUBLANES, LANES), F32),
                   jax.ShapeDtypeStruct((HEADS, DK, DV), F32),
                   jax.ShapeDtypeStruct((2, D_MODEL), F32),
                   jax.ShapeDtypeStruct(state_gla.shape, F32),
                   jax.ShapeDtypeStruct(state_conv.shape, F32)],
        scratch_shapes=[pltpu.VMEM((HEADS, DV, DK), F32),
                        pltpu.VMEM((max(TM + 8, streams * (seq + 8)), D_MODEL), F32),
                        pltpu.VMEM((TM, G0), F32),
                        pltpu.VMEM((TM, GLA_DK), F32),
                        pltpu.VMEM((TM, GLA_DV), F32)],
        compiler_params=pltpu.CompilerParams(dimension_semantics=("arbitrary",), vmem_limit_bytes=VMEM_LIMIT),
        name="mixer",
    )(meta, x_prompt, x_sample, state_gla, state_conv, *weights)


def _first_index_of_max(vals, ids, sentinel):
    m = jnp.max(vals, axis=0, keepdims=True)
    first = jnp.min(jnp.where(vals == m, ids, sentinel), axis=0, keepdims=True)
    return m, first


def _router_kernel(h_ref, wr_ref, bias_ref, idx_ref, w_ref, rank_ref, counts_ref):
    @pl.when(pl.program_id(0) == 0)
    def _init():
        counts_ref[...] = jnp.zeros_like(counts_ref)

    h = _tiles_to_rows(h_ref, TN_ROUTER)
    hh = h.astype(BF16)
    hl = (h - hh.astype(F32)).astype(BF16)
    wh, wl = wr_ref[0], wr_ref[1]
    dg = functools.partial(lax.dot_general, dimension_numbers=_DN_TB, preferred_element_type=F32)
    logits = dg(wh, hh) + dg(wl, hh) + dg(wh, hl)
    scores = _sigmoid(logits)
    biased = scores + bias_ref[...]
    n_tok = biased.shape[1]
    neg_inf = jnp.float32(-jnp.inf)

    eid = lax.broadcasted_iota(I32, (N_EXPERTS, n_tok), 0).astype(F32)
    lid = lax.broadcasted_iota(I32, (GROUP_SIZE, n_tok), 0).astype(F32)
    group_scores = []
    for g in range(N_GROUPS):
        blk = biased[g * GROUP_SIZE:(g + 1) * GROUP_SIZE]
        m1, first = _first_index_of_max(blk, lid, float(GROUP_SIZE))
        m2 = jnp.max(jnp.where(lid == first, neg_inf, blk), axis=0, keepdims=True)
        group_scores.append(m1 + m2)
    gsc = jnp.concatenate(group_scores, axis=0)

    gid = lax.broadcasted_iota(I32, (N_GROUPS, n_tok), 0).astype(F32)
    keep = jnp.zeros((N_GROUPS, n_tok), F32)
    cur = gsc
    for _ in range(TOPK_GROUPS):
        _, first = _first_index_of_max(cur, gid, float(N_GROUPS))
        sel = gid == first
        keep = jnp.where(sel, 1.0, keep)
        cur = jnp.where(sel, neg_inf, cur)

    masked = jnp.concatenate(
        [jnp.where(keep[g:g + 1] > 0.5, biased[g * GROUP_SIZE:(g + 1) * GROUP_SIZE], neg_inf)
         for g in range(N_GROUPS)], axis=0)
    idxs, wts, sels = [], [], []
    cur = masked
    for _ in range(TOP_K):
        _, first = _first_index_of_max(cur, eid, float(N_EXPERTS))
        sel = eid == first
        idxs.append(first)
        sels.append(sel)
        wts.append(jnp.sum(jnp.where(sel, scores, 0.0), axis=0, keepdims=True))
        cur = jnp.where(sel, neg_inf, cur)
    w = jnp.concatenate(wts, axis=0)
    idx_ref[...] = jnp.concatenate(idxs, axis=0).astype(I32)
    w_ref[...] = w / jnp.sum(w, axis=0, keepdims=True) * ROUTED_SCALE

    chosen = jnp.where(sels[0], 1.0, 0.0)
    for sel in sels[1:]:
        chosen = chosen + jnp.where(sel, 1.0, 0.0)
    t_src = lax.broadcasted_iota(I32, (n_tok, n_tok), 0)
    t_dst = lax.broadcasted_iota(I32, (n_tok, n_tok), 1)
    earlier = jnp.where(t_src < t_dst, 1.0, 0.0).astype(BF16)
    arrivals = counts_ref[...] + _dot(chosen.astype(BF16), earlier)
    rank_ref[...] = jnp.concatenate(
        [jnp.sum(jnp.where(sel, arrivals, 0.0), axis=0, keepdims=True) for sel in sels], axis=0).astype(I32)
    counts_ref[...] += jnp.sum(chosen, axis=1, keepdims=True)


def _router(h1_tiles, wr_split, bias_col):
    n = h1_tiles.shape[0] // SUBLANES
    tok = pl.BlockSpec((TOP_K, TN_ROUTER), lambda i: (0, i))
    return pl.pallas_call(
        _router_kernel,
        grid=(n // TN_ROUTER,),
        in_specs=[pl.BlockSpec((TN_ROUTER * SUBLANES, LANES), lambda i: (i, 0)),
                  pl.BlockSpec((2, N_EXPERTS, D_MODEL), lambda i: (0, 0, 0)),
                  pl.BlockSpec((N_EXPERTS, 1), lambda i: (0, 0))],
        out_specs=[tok, tok, tok, pl.BlockSpec((N_EXPERTS, 1), lambda i: (0, 0))],
        out_shape=[jax.ShapeDtypeStruct((TOP_K, n), I32), jax.ShapeDtypeStruct((TOP_K, n), F32),
                   jax.ShapeDtypeStruct((TOP_K, n), I32), jax.ShapeDtypeStruct((N_EXPERTS, 1), F32)],
        compiler_params=pltpu.CompilerParams(dimension_semantics=("arbitrary",), vmem_limit_bytes=VMEM_LIMIT),
        name="router",
    )(h1_tiles, wr_split, bias_col)


def _positions_kernel(idx_ref, rank_ref, segx_ref, segy_ref, posx_ref, posy_ref):
    n_tok = idx_ref.shape[1]
    eid = lax.broadcasted_iota(I32, (N_EXPERTS, n_tok), 0)
    segx, segy = segx_ref[...], segy_ref[...]
    rows_x, rows_y = [], []
    for k in range(TOP_K):
        onehot = eid == idx_ref[k:k + 1, :]
        rows_x.append(jnp.sum(jnp.where(onehot, segx, 0.0), axis=0, keepdims=True))
        rows_y.append(jnp.sum(jnp.where(onehot, segy, 0.0), axis=0, keepdims=True))
    posx_ref[...] = jnp.concatenate(rows_x, axis=0).astype(I32) + rank_ref[...]
    posy_ref[...] = jnp.concatenate(rows_y, axis=0).astype(I32) + rank_ref[...]


def _positions(idx_t, rank_t, segx_col, segy_col):
    n = idx_t.shape[1]
    tok = pl.BlockSpec((TOP_K, TN_ROUTER), lambda i: (0, i))
    table = pl.BlockSpec((N_EXPERTS, 1), lambda i: (0, 0))
    return pl.pallas_call(
        _positions_kernel,
        grid=(n // TN_ROUTER,),
        in_specs=[tok, tok, table, table],
        out_specs=[tok, tok],
        out_shape=[jax.ShapeDtypeStruct((TOP_K, n), I32), jax.ShapeDtypeStruct((TOP_K, n), I32)],
        compiler_params=pltpu.CompilerParams(dimension_semantics=("arbitrary",)),
        name="positions",
    )(idx_t, rank_t, segx_col, segy_col)


def _tile_copy(src, src_tile, dst, dst_tile, sem):
    return pltpu.make_async_copy(src.at[pl.ds(pl.multiple_of(src_tile * SUBLANES, SUBLANES), SUBLANES)],
                                 dst.at[pl.ds(pl.multiple_of(dst_tile * SUBLANES, SUBLANES), SUBLANES)], sem)


def _chunk_rows(first_tile):
    return pl.ds(pl.multiple_of(first_tile * SUBLANES, SUBLANES), CH * SUBLANES)


def _dispatch_kernel(pos_ref, h_hbm, x_hbm, zeros, sem, zsem, *, n_assign):
    step = pl.program_id(0)

    @pl.when(step == 0)
    def _zero_slack():
        zeros[...] = jnp.zeros_like(zeros)
        fill = pltpu.make_async_copy(zeros, x_hbm.at[_chunk_rows(n_assign)], zsem)
        fill.start()
        fill.wait()

    def body(t, carry):
        for k in range(TOP_K):
            _tile_copy(h_hbm, step * TN_DISPATCH + t, x_hbm, pos_ref[0, 0, k * TN_DISPATCH + t],
                       sem.at[k]).start(priority=k % 2)
        return carry
    lax.fori_loop(0, TN_DISPATCH, body, 0, unroll=8)

    def account_one_step():
        for k in range(TOP_K):
            span = pl.ds(0, TN_DISPATCH * SUBLANES)
            pltpu.make_async_copy(h_hbm.at[span], x_hbm.at[span], sem.at[k]).wait()

    @pl.when(step > 0)
    def _previous_step():
        account_one_step()

    @pl.when(step == pl.num_programs(0) - 1)
    def _this_step():
        account_one_step()


def _dispatch(pos_tiles, h1_tiles):
    n_t = pos_tiles.shape[0]
    n_assign = n_t * TN_DISPATCH * TOP_K
    return pl.pallas_call(
        functools.partial(_dispatch_kernel, n_assign=n_assign),
        grid=(n_t,),
        in_specs=[pl.BlockSpec((1, 1, TOP_K * TN_DISPATCH), lambda t: (t, 0, 0), memory_space=pltpu.SMEM),
                  pl.BlockSpec(memory_space=pl.ANY)],
        out_specs=pl.BlockSpec(memory_space=pl.ANY),
        out_shape=jax.ShapeDtypeStruct(((n_assign + CH) * SUBLANES, LANES), F32),
        scratch_shapes=[pltpu.VMEM((CH * SUBLANES, LANES), F32), pltpu.SemaphoreType.DMA((TOP_K,)),
                        pltpu.SemaphoreType.DMA(())],
        compiler_params=pltpu.CompilerParams(dimension_semantics=("arbitrary",), has_side_effects=True),
        name="dispatch",
    )(pos_tiles, h1_tiles)


def _expert_kernel(gfirst_ref, cnt_ref, seq_ref, nxt_ref, gtot_ref, src_ref, x_hbm, wg_hbm, wu_hbm, wd_hbm, y_hbm,
                   xbuf, ybuf, zbuf, wg32, wu32, wd32, wgb, wub, wdb, xsem, ysem, zsem, wsem, *, g_max):
    e = pl.program_id(0)
    cnt = cnt_ref[e]
    g_total = gtot_ref[0]

    def x_copy(g, slot):
        return pltpu.make_async_copy(x_hbm.at[_chunk_rows(src_ref[g])], xbuf.at[slot], xsem.at[slot])

    def y_copy(g, slot):
        return pltpu.make_async_copy(ybuf.at[slot], y_hbm.at[_chunk_rows(g * CH)], ysem.at[slot])

    def tail_fill(g):
        return pltpu.make_async_copy(zbuf, y_hbm.at[_chunk_rows(g * CH)], zsem)

    def for_tail_chunks(fn):
        def body(g, carry):
            fn(g)
            return carry
        lax.fori_loop(g_total, g_max, body, 0)

    def weight_copies(ex, slot):
        return (pltpu.make_async_copy(wg_hbm.at[ex], wg32.at[slot], wsem.at[slot]),
                pltpu.make_async_copy(wu_hbm.at[ex], wu32.at[slot], wsem.at[slot]),
                pltpu.make_async_copy(wd_hbm.at[ex], wd32.at[slot], wsem.at[slot]))

    @pl.when(e == 0)
    def _prime():
        for g in range(X_AHEAD):
            @pl.when(g < g_total)
            def _():
                x_copy(g, g).start()
        zbuf[...] = jnp.zeros_like(zbuf)
        for_tail_chunks(lambda g: tail_fill(g).start())

    @pl.when(cnt > 0)
    def _expert():
        seq, nxt = seq_ref[e], nxt_ref[e]
        wslot = lax.rem(seq, 2)

        @pl.when(seq == 0)
        def _fetch_own():
            for c in weight_copies(e, wslot):
                c.start(priority=1)

        @pl.when(nxt >= 0)
        def _prefetch_weights():
            for c in weight_copies(nxt, 1 - wslot):
                c.start(priority=1)

        for c in weight_copies(e, wslot):
            c.wait()
        wgb[...] = wg32[wslot].astype(BF16)
        wub[...] = wu32[wslot].astype(BF16)
        wdb[...] = wd32[wslot].astype(BF16)

        def process(g, width):
            for i in range(width):
                x_copy(0, lax.rem(g + i, NX)).wait()
            for i in range(width):
                ahead = g + i + X_AHEAD

                @pl.when(ahead < g_total)
                def _fetch_ahead():
                    x_copy(ahead, lax.rem(ahead, NX)).start()

            xb = jnp.concatenate([_tiles_to_rows(xbuf.at[lax.rem(g + i, NX)], CH) for i in range(width)],
                                 axis=0).astype(BF16)
            gate = _dot(xb, wgb[...])
            up = _dot(xb, wub[...])
            res = _dot(((gate * _sigmoid(gate)) * up).astype(BF16), wdb[...])
            for i in range(width):
                ys = lax.rem(g + i, NY)

                @pl.when(g + i >= NY)
                def _slot_written_back():
                    y_copy(0, ys).wait()

                _rows_to_tiles(ybuf.at[ys], res[i * CH:(i + 1) * CH])
                y_copy(g + i, ys).start()

        g_first = gfirst_ref[e]
        n_chunks = (cnt + CH - 1) // CH
        n_wide = n_chunks // GROUP
        rest = n_chunks - n_wide * GROUP

        def wide_group(j, carry):
            process(g_first + j * GROUP, GROUP)
            return carry

        lax.fori_loop(0, n_wide, wide_group, 0)
        g_rest = g_first + n_wide * GROUP
        width = GROUP // 2
        while width >= 1:
            @pl.when((rest & width) != 0)
            def _narrow_group(width=width):
                process(g_rest + (rest & ~(2 * width - 1)), width)
            width //= 2

    @pl.when(e == pl.num_programs(0) - 1)
    def _finish():
        for back in range(1, NY + 1):
            @pl.when(g_total - back >= 0)
            def _():
                y_copy(0, lax.rem(g_total - back, NY)).wait()

        for_tail_chunks(lambda g: tail_fill(0).wait())


def _max_chunks(n_assign):
    return n_assign // CH + N_EXPERTS


def _segments(counts, n_assign):
    n_chunks = (counts + CH - 1) // CH
    g_end = jnp.cumsum(n_chunks).astype(I32)
    g_first = g_end - n_chunks
    x_start = (jnp.cumsum(counts) - counts).astype(I32)
    chunk = jnp.arange(_max_chunks(n_assign), dtype=I32)
    eid = jnp.arange(N_EXPERTS, dtype=I32)
    owner = jnp.minimum(jnp.sum((g_end[None, :] <= chunk[:, None]).astype(I32), axis=1), N_EXPERTS - 1)
    onehot = owner[:, None] == eid[None, :]
    pick = lambda v: jnp.sum(jnp.where(onehot, v[None, :], 0), axis=1)
    src = jnp.minimum(pick(x_start) + (chunk - pick(g_first)) * CH, n_assign).astype(I32)
    has_rows = counts > 0
    seq = jnp.where(has_rows, jnp.cumsum(has_rows.astype(I32)) - 1, -1).astype(I32)
    nxt = jnp.min(jnp.where((eid[None, :] > eid[:, None]) & has_rows[None, :], eid[None, :], N_EXPERTS), axis=1)
    nxt = jnp.where(nxt == N_EXPERTS, -1, nxt).astype(I32)
    return dict(x_start=x_start, y_start=g_first * CH, g_first=g_first, g_total=g_end[-1:], src=src,
                seq=seq, nxt=nxt, counts=counts)


def _experts(seg, x_sorted, w_gate, w_up, w_down, n_assign):
    g_max = _max_chunks(n_assign)
    hbm = pl.BlockSpec(memory_space=pl.ANY)
    grid_spec = pltpu.PrefetchScalarGridSpec(
        num_scalar_prefetch=6,
        grid=(N_EXPERTS,),
        in_specs=[hbm, hbm, hbm, hbm],
        out_specs=hbm,
        scratch_shapes=[pltpu.VMEM((NX, CH * SUBLANES, LANES), F32),
                        pltpu.VMEM((NY, CH * SUBLANES, LANES), F32),
                        pltpu.VMEM((CH * SUBLANES, LANES), F32),
                        pltpu.VMEM((2, D_MODEL, EXPERT_HIDDEN), F32),
                        pltpu.VMEM((2, D_MODEL, EXPERT_HIDDEN), F32),
                        pltpu.VMEM((2, EXPERT_HIDDEN, D_MODEL), F32),
                        pltpu.VMEM((D_MODEL, EXPERT_HIDDEN), BF16),
                        pltpu.VMEM((D_MODEL, EXPERT_HIDDEN), BF16),
                        pltpu.VMEM((EXPERT_HIDDEN, D_MODEL), BF16),
                        pltpu.SemaphoreType.DMA((NX,)), pltpu.SemaphoreType.DMA((NY,)),
                        pltpu.SemaphoreType.DMA(()), pltpu.SemaphoreType.DMA((2,))],
    )
    return pl.pallas_call(
        functools.partial(_expert_kernel, g_max=g_max),
        grid_spec=grid_spec,
        out_shape=jax.ShapeDtypeStruct((g_max * CH * SUBLANES, LANES), F32),
        compiler_params=pltpu.CompilerParams(dimension_semantics=("arbitrary",), vmem_limit_bytes=VMEM_LIMIT,
                                             has_side_effects=True),
        name="experts",
    )(seg["g_first"], seg["counts"], seg["seq"], seg["nxt"], seg["g_total"], seg["src"],
      x_sorted, w_gate, w_up, w_down)


def _combine_kernel(pos_cur_ref, pos_nxt_ref, y_hbm, h_ref, w_ref, wsg_ref, wsu_ref, wsd_ref, l2g_ref, l2b_ref,
                    outp_ref, outs_ref, ybuf, sem, *, n_prompt_tiles):
    t = pl.program_id(0)
    n_t = pl.num_programs(0)
    slot = t % 2

    def gather_start(pos_ref, dst, dst_sem):
        def body(r, carry):
            for k in range(TOP_K):
                j = k * TN_COMBINE + r
                _tile_copy(y_hbm, pos_ref[0, 0, j], dst, j, dst_sem).start(priority=k % 2)
            return carry
        lax.fori_loop(0, TN_COMBINE, body, 0, unroll=8)

    @pl.when(t == 0)
    def _first():
        gather_start(pos_cur_ref, ybuf.at[0], sem.at[0])

    @pl.when(t + 1 < n_t)
    def _prefetch():
        gather_start(pos_nxt_ref, ybuf.at[1 - slot], sem.at[1 - slot])

    h = _tiles_to_rows(h_ref, TN_COMBINE)
    hb = h.astype(BF16)
    g = _dot(hb, wsg_ref[...])
    u = _dot(hb, wsu_ref[...])
    shared = _dot(((g * _sigmoid(g)) * u).astype(BF16), wsd_ref[...])

    yslot = ybuf.at[slot]
    pltpu.make_async_copy(y_hbm.at[pl.ds(0, TOP_K * TN_COMBINE * SUBLANES)], yslot, sem.at[slot]).wait()
    w = w_ref[...]
    wk = [jnp.broadcast_to(w[:, k:k + 1], (TN_COMBINE, LANES)) for k in range(TOP_K)]
    chunks = []
    for s in range(SUBLANES):
        acc = wk[0] * yslot[pl.ds(s, TN_COMBINE, stride=SUBLANES), :]
        for k in range(1, TOP_K):
            acc = acc + wk[k] * yslot[pl.ds(k * TN_COMBINE * SUBLANES + s, TN_COMBINE, stride=SUBLANES), :]
        chunks.append(acc)
    routed = jnp.concatenate(chunks, axis=1)
    out = _layer_norm(ALPHA * h + (routed + shared), l2g_ref[...], l2b_ref[...])

    @pl.when(t < n_prompt_tiles)
    def _prompt_rows():
        outp_ref[...] = out

    @pl.when(t >= n_prompt_tiles)
    def _sample_rows():
        outs_ref[...] = out


def _combine(pos_tiles, y_sorted, h1_tiles, w_tok, wsg, wsu, wsd, l2g, l2b, n_prompt):
    n = h1_tiles.shape[0] // SUBLANES
    n_t = n // TN_COMBINE
    npt = n_prompt // TN_COMBINE
    n_idx = TOP_K * TN_COMBINE
    cspec = lambda shape: pl.BlockSpec(shape, lambda t: (0,) * len(shape))
    return pl.pallas_call(
        functools.partial(_combine_kernel, n_prompt_tiles=npt),
        grid=(n_t,),
        in_specs=[pl.BlockSpec((1, 1, n_idx), lambda t: (t, 0, 0), memory_space=pltpu.SMEM),
                  pl.BlockSpec((1, 1, n_idx), lambda t: (jnp.minimum(t + 1, n_t - 1), 0, 0),
                               memory_space=pltpu.SMEM),
                  pl.BlockSpec(memory_space=pl.ANY),
                  pl.BlockSpec((TN_COMBINE * SUBLANES, LANES), lambda t: (t, 0)),
                  pl.BlockSpec((TN_COMBINE, TOP_K), lambda t: (t, 0)),
                  cspec((D_MODEL, SHARED_HIDDEN)), cspec((D_MODEL, SHARED_HIDDEN)), cspec((SHARED_HIDDEN, D_MODEL)),
                  cspec((1, D_MODEL)), cspec((1, D_MODEL))],
        out_specs=[pl.BlockSpec((TN_COMBINE, D_MODEL), lambda t: (jnp.minimum(t, npt - 1), 0)),
                   pl.BlockSpec((TN_COMBINE, D_MODEL), lambda t: (jnp.maximum(t - npt, 0), 0))],
        out_shape=[jax.ShapeDtypeStruct((n_prompt, D_MODEL), F32),
                   jax.ShapeDtypeStruct((n - n_prompt, D_MODEL), F32)],
        scratch_shapes=[pltpu.VMEM((2, n_idx * SUBLANES, LANES), F32), pltpu.SemaphoreType.DMA((2,))],
        compiler_params=pltpu.CompilerParams(dimension_semantics=("arbitrary",), vmem_limit_bytes=VMEM_LIMIT),
        name="combine",
    )(pos_tiles, pos_tiles, y_sorted, h1_tiles, w_tok, wsg, wsu, wsd, l2g, l2b)


def _pack_mixer_weights(ln_emb_g, ln_emb_b, w_in, w_gate_a2, b_gate, gla_norm_g, w_gla_out, conv_w,
                        w_conv_out, w_o, ln1_g, ln1_b):
    q, k, v, g, a, cb, cc, ch, ga, gb = jnp.split(
        w_in, [512, 1024, 2048, 3072, 3088, 4112, 5136, 6160, 7184], axis=-1)
    w_main = jnp.concatenate([q, k, v, g, cb, cc, ch, ga, gb], axis=-1).astype(BF16)
    w_a = jnp.pad(a, ((0, 0), (0, RANK_PAD - GATE_RANK))).astype(BF16)
    w_a2 = jnp.pad(w_gate_a2, ((0, RANK_PAD - GATE_RANK), (0, 0))).astype(BF16)
    row = lambda x: x.reshape(1, -1).astype(F32)
    return [row(ln_emb_g), row(ln_emb_b), w_main, w_a, w_a2, row(b_gate), row(gla_norm_g),
            w_gla_out.astype(BF16), conv_w.astype(F32), w_conv_out.astype(BF16), w_o.astype(BF16),
            row(ln1_g), row(ln1_b)]


def kernel(x_prompt, x_sample, state_gla, state_conv, meta_tokens, ln_emb_g, ln_emb_b, w_in, w_gate_a2, b_gate, gla_norm_g, w_gla_out, conv_w, w_conv_out, w_o, ln1_g, ln1_b, w_router, router_bias, w_exp_gate, w_exp_up, w_exp_down, w_sh_gate, w_sh_up, w_sh_down, ln2_g, ln2_b):
    batch, seq, _ = x_prompt.shape
    dec_batch, dec_seq, _ = x_sample.shape
    depth = w_in.shape[0]
    assert batch == 1 and depth == 1 and seq % TM == 0 and TM % dec_seq == 0 and dec_seq % SUB == 0
    n_prompt, n_sample = batch * seq, dec_batch * dec_seq
    n_total = n_prompt + n_sample
    assert n_sample % TM == 0 and n_total % TN_ROUTER == 0
    assert n_prompt % TN_COMBINE == 0 and n_sample % TN_COMBINE == 0 and n_total % TN_DISPATCH == 0
    n_assign = n_total * TOP_K

    weights = _pack_mixer_weights(ln_emb_g, ln_emb_b, w_in[0], w_gate_a2[0], b_gate[0], gla_norm_g[0],
                                  w_gla_out[0], conv_w[0], w_conv_out[0], w_o[0], ln1_g[0], ln1_b[0])
    h1, sg_p, sc_p, sg_s, sc_s = _mixer(meta_tokens.astype(F32), x_prompt.reshape(n_prompt, D_MODEL),
                                        x_sample.reshape(n_sample, D_MODEL), state_gla[0], state_conv[0], weights)

    wr_t = w_router[0].T
    wr_hi = wr_t.astype(BF16)
    wr_split = jnp.stack([wr_hi, (wr_t - wr_hi.astype(F32)).astype(BF16)])
    idx_t, w_t, rank_t, counts = _router(h1, wr_split, router_bias[0].reshape(N_EXPERTS, 1).astype(F32))

    seg = _segments(counts.reshape(N_EXPERTS).astype(I32), n_assign)
    table = lambda v: v.astype(F32).reshape(N_EXPERTS, 1)
    posx_t, posy_t = _positions(idx_t, rank_t, table(seg["x_start"]), table(seg["y_start"]))
    tiles = lambda p, tn: p.reshape(TOP_K, n_total // tn, tn).transpose(1, 0, 2).reshape(n_total // tn, 1, -1)

    x_sorted = _dispatch(tiles(posx_t, TN_DISPATCH), h1)
    y_sorted = _experts(seg, x_sorted, w_exp_gate[0], w_exp_up[0], w_exp_down[0], n_assign)
    y_prompt, y_sample = _combine(tiles(posy_t, TN_COMBINE), y_sorted, h1, w_t.T,
                                  w_sh_gate[0].astype(BF16), w_sh_up[0].astype(BF16), w_sh_down[0].astype(BF16),
                                  ln2_g[0].reshape(1, -1), ln2_b[0].reshape(1, -1), n_prompt)
    y_prompt = y_prompt.reshape(batch, seq, D_MODEL)
    y_sample = y_sample.reshape(dec_batch, dec_seq, D_MODEL)
    return (y_prompt, y_sample,
            sg_p.reshape(depth, batch, HEADS, DK, DV), sc_p.reshape(depth, batch, 2, D_MODEL),
            sg_s.reshape(depth, dec_batch, HEADS, DK, DV), sc_s.reshape(depth, dec_batch, 2, D_MODEL))
```

```python
import functools

import jax
import jax.numpy as jnp
from jax import lax
from jax.experimental import pallas as pl
from jax.experimental.pallas import tpu as pltpu

F32 = jnp.float32
BF16 = jnp.bfloat16
I32 = jnp.int32

D_MODEL = 1024
N_META = 16
HEADS = 4
DK = 128
DV = 256
GLA_DK = HEADS * DK
GLA_DV = HEADS * DV
GATE_RANK = 16
GATE_TAU = 16.0
N_EXPERTS = 256
TOP_K = 8
N_GROUPS = 8
GROUP_SIZE = N_EXPERTS // N_GROUPS
TOPK_GROUPS = 4
EXPERT_HIDDEN = 256
SHARED_HIDDEN = 256
ROUTED_SCALE = 2.5
LN_EPS = 1e-5
RMS_EPS = 1e-6
ALPHA = 2.0 ** 0.25
QSCALE = DK ** -0.5

Q0, K0, V0, G0, CB0, CC0, CH0, GA0, GB0 = 0, 512, 1024, 2048, 3072, 4096, 5120, 6144, 7168
W_MAIN_COLS = 8192
RANK_PAD = 128

SUB = 16
PROMPT_CHUNK = 64
TM = 256
TN_ROUTER = 512
TN_POSITIONS = 1536
TN_DISPATCH = 512
CH = 64
GROUP = 8
X_AHEAD = 12
assert GROUP & (GROUP - 1) == 0
NX = X_AHEAD + GROUP
NY = 2 * GROUP
TN_COMBINE = 256
VMEM_LIMIT = 60 * 1024 * 1024
SUBLANES = 8
LANES = 128
assert D_MODEL == SUBLANES * LANES

_DN_TB = (((1,), (1,)), ((), ()))
_DN_TA = (((0,), (0,)), ((), ()))


def _dot(a, b):
    return jnp.dot(a, b, preferred_element_type=F32)


def _tiles_to_rows(ref, n, base=0):
    return jnp.concatenate(
        [ref[pl.ds(SUBLANES * base + s, n, stride=SUBLANES), :] for s in range(SUBLANES)], axis=1)


def _rows_to_tiles(ref, val):
    n = val.shape[0]
    for s in range(SUBLANES):
        ref[pl.ds(s, n, stride=SUBLANES), :] = val[:, s * LANES:(s + 1) * LANES]


def _layer_norm(x, g, b):
    mu = jnp.mean(x, axis=-1, keepdims=True)
    xc = x - mu
    var = jnp.mean(xc * xc, axis=-1, keepdims=True)
    return xc * lax.rsqrt(var + LN_EPS) * g + b


def _sigmoid(x):
    return 1.0 / (1.0 + jnp.exp(-x))


def _log_sigmoid(x):
    return jnp.minimum(x, 0.0) - jnp.log1p(jnp.exp(-jnp.abs(x)))


def _chunk_cumsum(g, chunk):
    t = g.shape[0]
    shift = chunk.bit_length() - 1
    r = lax.broadcasted_iota(I32, (t, t), 0)
    c = lax.broadcasted_iota(I32, (t, t), 1)
    tri = jnp.where(((r >> shift) == (c >> shift)) & (c <= r), 1.0, 0.0).astype(BF16)
    g1 = g.astype(BF16)
    r1 = g - g1.astype(F32)
    g2 = r1.astype(BF16)
    g3 = (r1 - g2.astype(F32)).astype(BF16)
    return _dot(tri, g1) + _dot(tri, g2) + _dot(tri, g3)


def _gate_log_decay(hb, wa_ref, wa2_ref, bg_ref):
    a = _dot(hb, wa_ref[...])
    return _log_sigmoid(_dot(a.astype(BF16), wa2_ref[...]) + bg_ref[...]) * (1.0 / GATE_TAU)


def _gla_state_update(k, v, b, st):
    n = k.shape[0]
    b_last = b[n - 1:n, :]
    khat = (k * jnp.exp(b_last - b)).astype(BF16)
    return st * jnp.exp(b_last) + lax.dot_general(v.astype(BF16), khat, _DN_TA, preferred_element_type=F32)


def _gla_chunk(q, k, v, b, st):
    n = q.shape[0]
    vb = v.astype(BF16)
    q0 = (q * (jnp.exp(b) * QSCALE)).astype(BF16)
    o = lax.dot_general(q0, st.astype(BF16), _DN_TB, preferred_element_type=F32)
    col = lax.broadcasted_iota(I32, (SUB, n), 1)
    row = lax.broadcasted_iota(I32, (SUB, n), 0)
    blocks = []
    for i in range(n // SUB):
        lo, hi = SUB * i, SUB * (i + 1)
        bref = b[lo:lo + 1, :]
        qi = (q[lo:hi] * (jnp.exp(b[lo:hi] - bref) * QSCALE)).astype(BF16)
        kk = (k[:hi] * jnp.exp(bref - b[:hi])).astype(BF16)
        if hi < n:
            kk = jnp.concatenate([kk, jnp.zeros((n - hi, DK), BF16)], axis=0)
        a = lax.dot_general(qi, kk, _DN_TB, preferred_element_type=F32)
        blocks.append(jnp.where(col <= row + lo, a, 0.0))
    scores = blocks[0] if len(blocks) == 1 else jnp.concatenate(blocks, axis=0)
    o = o + _dot(scores.astype(BF16), vb)
    return o, _gla_state_update(k, v, b, st)


def _mixer_back(h, hb, o_ref, conv, wm_ref, ng_ref, wgo_ref, wco_ref, wo_ref, l1g_ref, l1b_ref):
    g = _dot(hb, wm_ref[:, G0:G0 + GLA_DV])
    parts = []
    for hd in range(HEADS):
        oh = o_ref[:, hd * DV:(hd + 1) * DV]
        ms = jnp.mean(oh * oh, axis=-1, keepdims=True)
        parts.append(oh * lax.rsqrt(ms + RMS_EPS) * ng_ref[...])
    on = jnp.concatenate(parts, axis=1) * (g * _sigmoid(g))
    branch_a = _dot(on.astype(BF16), wgo_ref[...])
    mix = _sigmoid(_dot(hb, wm_ref[:, GA0:GA0 + D_MODEL])) * branch_a
    yc = _dot(hb, wm_ref[:, CB0:CB0 + D_MODEL]) * conv
    branch_b = _dot(yc.astype(BF16), wco_ref[...])
    mix = mix + _sigmoid(_dot(hb, wm_ref[:, GB0:GB0 + D_MODEL])) * branch_b
    mixed = _dot(mix.astype(BF16), wo_ref[...])
    return _layer_norm(ALPHA * h + mixed, l1g_ref[...], l1b_ref[...])


def _conv_input(hb, wm_ref):
    return _dot(hb, wm_ref[:, CC0:CC0 + D_MODEL]) * _dot(hb, wm_ref[:, CH0:CH0 + D_MODEL])


def _gla_rows(qkv_ref, b_ref, rows, hd, st):
    kc = slice(hd * DK, (hd + 1) * DK)
    return _gla_chunk(qkv_ref[rows, Q0 + hd * DK:Q0 + (hd + 1) * DK],
                      qkv_ref[rows, K0 + hd * DK:K0 + (hd + 1) * DK],
                      qkv_ref[rows, V0 + hd * DV:V0 + (hd + 1) * DV],
                      b_ref[rows, kc], st)


def _mixer_kernel(meta_ref, xp_ref, xs_ref, sgin_ref, scin_ref,
                  lng_ref, lnb_ref, wm_ref, wa_ref, wa2_ref, bg_ref,
                  ng_ref, wgo_ref, cw_ref, wco_ref, wo_ref, l1g_ref, l1b_ref,
                  h1_ref, sgp_ref, scp_ref, sgs_ref, scs_ref,
                  st_ref, zbuf_ref, qkv_ref, b_ref, o_ref, *, n_prompt_tiles, seq, streams):
    i = pl.program_id(0)
    back = functools.partial(_mixer_back, wm_ref=wm_ref, ng_ref=ng_ref, wgo_ref=wgo_ref, wco_ref=wco_ref,
                             wo_ref=wo_ref, l1g_ref=l1g_ref, l1b_ref=l1b_ref)

    def front(x):
        h = _layer_norm(x, lng_ref[...], lnb_ref[...])
        hb = h.astype(BF16)
        return h, hb, _gate_log_decay(hb, wa_ref, wa2_ref, bg_ref)

    @pl.when(i == 0)
    def _meta():
        _, hb, gk = front(meta_ref[...])
        k = _dot(hb, wm_ref[:, K0:K0 + GLA_DK])
        v = _dot(hb, wm_ref[:, V0:V0 + GLA_DV])
        b = _chunk_cumsum(gk, N_META)
        zero_state = jnp.zeros((DV, DK), F32)
        for hd in range(HEADS):
            kc = slice(hd * DK, (hd + 1) * DK)
            st_ref[hd] = _gla_state_update(k[:, kc], v[:, hd * DV:(hd + 1) * DV], b[:, kc], zero_state)
        zbuf_ref[0:8, :] = _conv_input(hb, wm_ref)[N_META - 8:N_META, :]

    @pl.when((i >= 1) & (i <= n_prompt_tiles))
    def _prompt_tile():
        h, hb, gk = front(xp_ref[...])
        qkv_ref[...] = _dot(hb, wm_ref[:, Q0:G0])
        b_ref[...] = _chunk_cumsum(gk, PROMPT_CHUNK)
        for c in range(TM // PROMPT_CHUNK):
            rows = slice(c * PROMPT_CHUNK, (c + 1) * PROMPT_CHUNK)
            for hd in range(HEADS):
                o, st_new = _gla_rows(qkv_ref, b_ref, rows, hd, st_ref[hd])
                st_ref[hd] = st_new
                o_ref[rows, hd * DV:(hd + 1) * DV] = o
        z = _conv_input(hb, wm_ref)
        zbuf_ref[8:8 + TM, :] = z
        cw = cw_ref[...]
        conv = cw[0:1, :] * zbuf_ref[6:6 + TM, :] + cw[1:2, :] * zbuf_ref[7:7 + TM, :] + cw[2:3, :] * z
        zbuf_ref[0:8, :] = z[TM - 8:TM, :]
        _rows_to_tiles(h1_ref, back(h, hb, o_ref, conv))

        @pl.when(i == n_prompt_tiles)
        def _prompt_states():
            for hd in range(HEADS):
                sgp_ref[hd] = st_ref[hd].T
            scp_ref[...] = z[TM - 2:TM, :]

    @pl.when(i > n_prompt_tiles)
    def _sample_tile():
        h, hb, gk = front(xs_ref[...])
        qkv_ref[...] = _dot(hb, wm_ref[:, Q0:G0])
        b_ref[...] = _chunk_cumsum(gk, seq)
        z = _conv_input(hb, wm_ref)
        pitch = seq + 8
        cw = cw_ref[...]
        convs = []
        for s in range(streams):
            rows = slice(s * seq, (s + 1) * seq)
            for hd in range(HEADS):
                o, st_new = _gla_rows(qkv_ref, b_ref, rows, hd, sgin_ref[s, hd].T)
                sgs_ref[s, hd] = st_new.T
                o_ref[rows, hd * DV:(hd + 1) * DV] = o
            zs = z[rows, :]
            base = s * pitch
            zbuf_ref[base + 6:base + 8, :] = scin_ref[s]
            zbuf_ref[base + 8:base + 8 + seq, :] = zs
            convs.append(cw[0:1, :] * zbuf_ref[base + 6:base + 6 + seq, :]
                         + cw[1:2, :] * zbuf_ref[base + 7:base + 7 + seq, :] + cw[2:3, :] * zs)
            scs_ref[s] = zs[seq - 2:seq, :]
        _rows_to_tiles(h1_ref, back(h, hb, o_ref, jnp.concatenate(convs, axis=0)))


def _const_spec(shape):
    nd = len(shape)
    return pl.BlockSpec(shape, lambda i, _nd=nd: (0,) * _nd, pipeline_mode=pl.Buffered(1))


def _mixer_weight_specs():
    return [
        _const_spec((1, D_MODEL)), _const_spec((1, D_MODEL)),
        _const_spec((D_MODEL, W_MAIN_COLS)),
        _const_spec((D_MODEL, RANK_PAD)), _const_spec((RANK_PAD, GLA_DK)), _const_spec((1, GLA_DK)),
        _const_spec((1, DV)), _const_spec((GLA_DV, D_MODEL)),
        _const_spec((3, D_MODEL)), _const_spec((D_MODEL, D_MODEL)),
        _const_spec((D_MODEL, D_MODEL)),
        _const_spec((1, D_MODEL)), _const_spec((1, D_MODEL)),
    ]


def _mixer(meta, x_prompt, x_sample, state_gla, state_conv, weights):
    n_streams = state_conv.shape[0]
    seq = x_sample.shape[0] // n_streams
    streams = TM // seq
    npt = x_prompt.shape[0] // TM
    nst = n_streams // streams
    ptile = lambda i: (jnp.clip(i - 1, 0, npt - 1), 0)
    stile = lambda i: jnp.clip(i - 1 - npt, 0, nst - 1)
    kern = functools.partial(_mixer_kernel, n_prompt_tiles=npt, seq=seq, streams=streams)
    return pl.pallas_call(
        kern,
        grid=(1 + npt + nst,),
        in_specs=[_const_spec((N_META, D_MODEL)),
                  pl.BlockSpec((TM, D_MODEL), ptile),
                  pl.BlockSpec((TM, D_MODEL), lambda i: (stile(i), 0), pipeline_mode=pl.Buffered(1)),
                  pl.BlockSpec((streams, HEADS, DK, DV), lambda i: (stile(i), 0, 0, 0),
                               pipeline_mode=pl.Buffered(1)),
                  pl.BlockSpec((streams, 2, D_MODEL), lambda i: (stile(i), 0, 0))] + _mixer_weight_specs(),
        out_specs=[pl.BlockSpec((TM * SUBLANES, LANES), lambda i: (jnp.maximum(i - 1, 0), 0)),
                   pl.BlockSpec((HEADS, DK, DV), lambda i: (0, 0, 0)),
                   pl.BlockSpec((2, D_MODEL), lambda i: (0, 0)),
                   pl.BlockSpec((streams, HEADS, DK, DV), lambda i: (stile(i), 0, 0, 0)),
                   pl.BlockSpec((streams, 2, D_MODEL), lambda i: (stile(i), 0, 0))],
        out_shape=[jax.ShapeDtypeStruct(((x_prompt.shape[0] + x_sample.shape[0]) * SUBLANES, LANES), F32),
                   jax.ShapeDtypeStruct((HEADS, DK, DV), F32),
                   jax.ShapeDtypeStruct((2, D_MODEL), F32),
                   jax.ShapeDtypeStruct(state_gla.shape, F32),
                   jax.ShapeDtypeStruct(state_conv.shape, F32)],
        scratch_shapes=[pltpu.VMEM((HEADS, DV, DK), F32),
                        pltpu.VMEM((max(TM + 8, streams * (seq + 8)), D_MODEL), F32),
                        pltpu.VMEM((TM, G0), F32),
                        pltpu.VMEM((TM, GLA_DK), F32),
                        pltpu.VMEM((TM, GLA_DV), F32)],
        compiler_params=pltpu.CompilerParams(dimension_semantics=("arbitrary",), vmem_limit_bytes=VMEM_LIMIT),
        name="mixer",
    )(meta, x_prompt, x_sample, state_gla, state_conv, *weights)


def _first_index_of_max(vals, ids, sentinel):
    m = jnp.max(vals, axis=0, keepdims=True)
    first = jnp.min(jnp.where(vals == m, ids, sentinel), axis=0, keepdims=True)
    return m, first


def _router_kernel(h_ref, wr_ref, bias_ref, idx_ref, w_ref, rank_ref, counts_ref):
    @pl.when(pl.program_id(0) == 0)
    def _init():
        counts_ref[...] = jnp.zeros_like(counts_ref)

    h = _tiles_to_rows(h_ref, TN_ROUTER)
    hh = h.astype(BF16)
    hl = (h - hh.astype(F32)).astype(BF16)
    wh, wl = wr_ref[0], wr_ref[1]
    dg = functools.partial(lax.dot_general, dimension_numbers=_DN_TB, preferred_element_type=F32)
    logits = dg(wh, hh) + dg(wl, hh) + dg(wh, hl)
    scores = _sigmoid(logits)
    biased = scores + bias_ref[...]
    n_tok = biased.shape[1]
    neg_inf = jnp.float32(-jnp.inf)

    eid = lax.broadcasted_iota(I32, (N_EXPERTS, n_tok), 0).astype(F32)
    lid = lax.broadcasted_iota(I32, (GROUP_SIZE, n_tok), 0).astype(F32)
    group_scores = []
    for g in range(N_GROUPS):
        blk = biased[g * GROUP_SIZE:(g + 1) * GROUP_SIZE]
        m1, first = _first_index_of_max(blk, lid, float(GROUP_SIZE))
        m2 = jnp.max(jnp.where(lid == first, neg_inf, blk), axis=0, keepdims=True)
        group_scores.append(m1 + m2)
    gsc = jnp.concatenate(group_scores, axis=0)

    gid = lax.broadcasted_iota(I32, (N_GROUPS, n_tok), 0).astype(F32)
    keep = jnp.zeros((N_GROUPS, n_tok), F32)
    cur = gsc
    for _ in range(TOPK_GROUPS):
        _, first = _first_index_of_max(cur, gid, float(N_GROUPS))
        sel = gid == first
        keep = jnp.where(sel, 1.0, keep)
        cur = jnp.where(sel, neg_inf, cur)

    masked = jnp.concatenate(
        [jnp.where(keep[g:g + 1] > 0.5, biased[g * GROUP_SIZE:(g + 1) * GROUP_SIZE], neg_inf)
         for g in range(N_GROUPS)], axis=0)
    idxs, wts, sels = [], [], []
    cur = masked
    for _ in range(TOP_K):
        _, first = _first_index_of_max(cur, eid, float(N_EXPERTS))
        sel = eid == first
        idxs.append(first)
        sels.append(sel)
        wts.append(jnp.sum(jnp.where(sel, scores, 0.0), axis=0, keepdims=True))
        cur = jnp.where(sel, neg_inf, cur)
    w = jnp.concatenate(wts, axis=0)
    idx_ref[...] = jnp.concatenate(idxs, axis=0).astype(I32)
    w_ref[...] = w / jnp.sum(w, axis=0, keepdims=True) * ROUTED_SCALE

    chosen = jnp.where(sels[0], 1.0, 0.0)
    for sel in sels[1:]:
        chosen = chosen + jnp.where(sel, 1.0, 0.0)
    t_src = lax.broadcasted_iota(I32, (n_tok, n_tok), 0)
    t_dst = lax.broadcasted_iota(I32, (n_tok, n_tok), 1)
    earlier = jnp.where(t_src < t_dst, 1.0, 0.0).astype(BF16)
    arrivals = counts_ref[...] + _dot(chosen.astype(BF16), earlier)
    rank_ref[...] = jnp.concatenate(
        [jnp.sum(jnp.where(sel, arrivals, 0.0), axis=0, keepdims=True) for sel in sels], axis=0).astype(I32)
    counts_ref[...] += jnp.sum(chosen, axis=1, keepdims=True)


def _router(h1_tiles, wr_split, bias_col):
    n = h1_tiles.shape[0] // SUBLANES
    tok = pl.BlockSpec((TOP_K, TN_ROUTER), lambda i: (0, i))
    return pl.pallas_call(
        _router_kernel,
        grid=(n // TN_ROUTER,),
        in_specs=[pl.BlockSpec((TN_ROUTER * SUBLANES, LANES), lambda i: (i, 0)),
                  pl.BlockSpec((2, N_EXPERTS, D_MODEL), lambda i: (0, 0, 0)),
                  pl.BlockSpec((N_EXPERTS, 1), lambda i: (0, 0))],
        out_specs=[tok, tok, tok, pl.BlockSpec((N_EXPERTS, 1), lambda i: (0, 0))],
        out_shape=[jax.ShapeDtypeStruct((TOP_K, n), I32), jax.ShapeDtypeStruct((TOP_K, n), F32),
                   jax.ShapeDtypeStruct((TOP_K, n), I32), jax.ShapeDtypeStruct((N_EXPERTS, 1), F32)],
        compiler_params=pltpu.CompilerParams(dimension_semantics=("arbitrary",), vmem_limit_bytes=VMEM_LIMIT),
        name="router",
    )(h1_tiles, wr_split, bias_col)


def _positions_kernel(idx_ref, rank_ref, segx_ref, segy_ref, posx_ref, posy_ref):
    n_tok = idx_ref.shape[1]
    eid = lax.broadcasted_iota(I32, (N_EXPERTS, n_tok), 0)
    segx, segy = segx_ref[...], segy_ref[...]
    rows_x, rows_y = [], []
    for k in range(TOP_K):
        onehot = eid == idx_ref[k:k + 1, :]
        rows_x.append(jnp.sum(jnp.where(onehot, segx, 0.0), axis=0, keepdims=True))
        rows_y.append(jnp.sum(jnp.where(onehot, segy, 0.0), axis=0, keepdims=True))
    posx_ref[...] = jnp.concatenate(rows_x, axis=0).astype(I32) + rank_ref[...]
    posy_ref[...] = jnp.concatenate(rows_y, axis=0).astype(I32) + rank_ref[...]


def _positions(idx_t, rank_t, segx_col, segy_col):
    n = idx_t.shape[1]
    tok = pl.BlockSpec((TOP_K, TN_POSITIONS), lambda i: (0, i))
    table = pl.BlockSpec((N_EXPERTS, 1), lambda i: (0, 0))
    return pl.pallas_call(
        _positions_kernel,
        grid=(n // TN_POSITIONS,),
        in_specs=[tok, tok, table, table],
        out_specs=[tok, tok],
        out_shape=[jax.ShapeDtypeStruct((TOP_K, n), I32), jax.ShapeDtypeStruct((TOP_K, n), I32)],
        compiler_params=pltpu.CompilerParams(dimension_semantics=("arbitrary",)),
        name="positions",
    )(idx_t, rank_t, segx_col, segy_col)


def _tile_copy(src, src_tile, dst, dst_tile, sem):
    return pltpu.make_async_copy(src.at[pl.ds(pl.multiple_of(src_tile * SUBLANES, SUBLANES), SUBLANES)],
                                 dst.at[pl.ds(pl.multiple_of(dst_tile * SUBLANES, SUBLANES), SUBLANES)], sem)


def _chunk_rows(first_tile):
    return pl.ds(pl.multiple_of(first_tile * SUBLANES, SUBLANES), CH * SUBLANES)


def _dispatch_kernel(pos_ref, h_ref, x_hbm, zeros, sem, zsem, *, n_assign):
    @pl.when(pl.program_id(0) == 0)
    def _zero_slack():
        zeros[...] = jnp.zeros_like(zeros)
        fill = pltpu.make_async_copy(zeros, x_hbm.at[_chunk_rows(n_assign)], zsem)
        fill.start()
        fill.wait()

    def body(t, carry):
        for k in range(TOP_K):
            _tile_copy(h_ref, t, x_hbm, pos_ref[0, 0, k * TN_DISPATCH + t], sem.at[k]).start(priority=k % 2)
        return carry
    lax.fori_loop(0, TN_DISPATCH, body, 0, unroll=8)
    for k in range(TOP_K):
        pltpu.make_async_copy(h_ref, x_hbm.at[pl.ds(0, TN_DISPATCH * SUBLANES)], sem.at[k]).wait()


def _dispatch(pos_tiles, h1_tiles):
    n_t = pos_tiles.shape[0]
    n_assign = n_t * TN_DISPATCH * TOP_K
    return pl.pallas_call(
        functools.partial(_dispatch_kernel, n_assign=n_assign),
        grid=(n_t,),
        in_specs=[pl.BlockSpec((1, 1, TOP_K * TN_DISPATCH), lambda t: (t, 0, 0), memory_space=pltpu.SMEM),
                  pl.BlockSpec((TN_DISPATCH * SUBLANES, LANES), lambda t: (t, 0))],
        out_specs=pl.BlockSpec(memory_space=pl.ANY),
        out_shape=jax.ShapeDtypeStruct(((n_assign + CH) * SUBLANES, LANES), F32),
        scratch_shapes=[pltpu.VMEM((CH * SUBLANES, LANES), F32), pltpu.SemaphoreType.DMA((TOP_K,)),
                        pltpu.SemaphoreType.DMA(())],
        compiler_params=pltpu.CompilerParams(dimension_semantics=("arbitrary",), has_side_effects=True),
        name="dispatch",
    )(pos_tiles, h1_tiles)


def _expert_kernel(gfirst_ref, cnt_ref, seq_ref, nxt_ref, gtot_ref, src_ref, x_hbm, wg_hbm, wu_hbm, wd_hbm, y_hbm,
                   xbuf, ybuf, zbuf, wg32, wu32, wd32, wgb, wub, wdb, xsem, ysem, zsem, wsem, *, g_max):
    e = pl.program_id(0)
    cnt = cnt_ref[e]
    g_total = gtot_ref[0]

    def x_copy(g, slot):
        return pltpu.make_async_copy(x_hbm.at[_chunk_rows(src_ref[g])], xbuf.at[slot], xsem.at[slot])

    def y_copy(g, slot):
        return pltpu.make_async_copy(ybuf.at[slot], y_hbm.at[_chunk_rows(g * CH)], ysem.at[slot])

    def tail_fill(g):
        return pltpu.make_async_copy(zbuf, y_hbm.at[_chunk_rows(g * CH)], zsem)

    def for_tail_chunks(fn):
        def body(g, carry):
            fn(g)
            return carry
        lax.fori_loop(g_total, g_max, body, 0)

    def weight_copies(ex, slot):
        return (pltpu.make_async_copy(wg_hbm.at[ex], wg32.at[slot], wsem.at[slot]),
                pltpu.make_async_copy(wu_hbm.at[ex], wu32.at[slot], wsem.at[slot]),
                pltpu.make_async_copy(wd_hbm.at[ex], wd32.at[slot], wsem.at[slot]))

    @pl.when(e == 0)
    def _prime():
        for g in range(X_AHEAD):
            @pl.when(g < g_total)
            def _():
                x_copy(g, g).start()
        zbuf[...] = jnp.zeros_like(zbuf)
        for_tail_chunks(lambda g: tail_fill(g).start())

    @pl.when(cnt > 0)
    def _expert():
        seq, nxt = seq_ref[e], nxt_ref[e]
        wslot = lax.rem(seq, 2)

        @pl.when(seq == 0)
        def _fetch_own():
            for c in weight_copies(e, wslot):
                c.start(priority=1)

        @pl.when(nxt >= 0)
        def _prefetch_weights():
            for c in weight_copies(nxt, 1 - wslot):
                c.start(priority=1)

        for c in weight_copies(e, wslot):
            c.wait()
        wgb[...] = wg32[wslot].astype(BF16)
        wub[...] = wu32[wslot].astype(BF16)
        wdb[...] = wd32[wslot].astype(BF16)

        def process(g, width):
            for i in range(width):
                x_copy(0, lax.rem(g + i, NX)).wait()
            for i in range(width):
                ahead = g + i + X_AHEAD

                @pl.when(ahead < g_total)
                def _fetch_ahead():
                    x_copy(ahead, lax.rem(ahead, NX)).start()

            xb = jnp.concatenate([_tiles_to_rows(xbuf.at[lax.rem(g + i, NX)], CH) for i in range(width)],
                                 axis=0).astype(BF16)
            gate = _dot(xb, wgb[...])
            up = _dot(xb, wub[...])
            res = _dot(((gate * _sigmoid(gate)) * up).astype(BF16), wdb[...])
            for i in range(width):
                ys = lax.rem(g + i, NY)

                @pl.when(g + i >= NY)
                def _slot_written_back():
                    y_copy(0, ys).wait()

                _rows_to_tiles(ybuf.at[ys], res[i * CH:(i + 1) * CH])
                y_copy(g + i, ys).start()

        g_first = gfirst_ref[e]
        n_chunks = (cnt + CH - 1) // CH
        n_wide = n_chunks // GROUP
        rest = n_chunks - n_wide * GROUP

        def wide_group(j, carry):
            process(g_first + j * GROUP, GROUP)
            return carry

        lax.fori_loop(0, n_wide, wide_group, 0)
        g_rest = g_first + n_wide * GROUP
        width = GROUP // 2
        while width >= 1:
            @pl.when((rest & width) != 0)
            def _narrow_group(width=width):
                process(g_rest + (rest & ~(2 * width - 1)), width)
            width //= 2

    @pl.when(e == pl.num_programs(0) - 1)
    def _finish():
        for back in range(1, NY + 1):
            @pl.when(g_total - back >= 0)
            def _():
                y_copy(0, lax.rem(g_total - back, NY)).wait()

        for_tail_chunks(lambda g: tail_fill(0).wait())


def _max_chunks(n_assign):
    return n_assign // CH + N_EXPERTS


def _segments(counts, n_assign):
    n_chunks = (counts + CH - 1) // CH
    g_end = jnp.cumsum(n_chunks).astype(I32)
    g_first = g_end - n_chunks
    x_start = (jnp.cumsum(counts) - counts).astype(I32)
    chunk = jnp.arange(_max_chunks(n_assign), dtype=I32)
    eid = jnp.arange(N_EXPERTS, dtype=I32)
    owner = jnp.minimum(jnp.sum((g_end[None, :] <= chunk[:, None]).astype(I32), axis=1), N_EXPERTS - 1)
    onehot = owner[:, None] == eid[None, :]
    pick = lambda v: jnp.sum(jnp.where(onehot, v[None, :], 0), axis=1)
    src = jnp.minimum(pick(x_start) + (chunk - pick(g_first)) * CH, n_assign).astype(I32)
    has_rows = counts > 0
    seq = jnp.where(has_rows, jnp.cumsum(has_rows.astype(I32)) - 1, -1).astype(I32)
    nxt = jnp.min(jnp.where((eid[None, :] > eid[:, None]) & has_rows[None, :], eid[None, :], N_EXPERTS), axis=1)
    nxt = jnp.where(nxt == N_EXPERTS, -1, nxt).astype(I32)
    return dict(x_start=x_start, y_start=g_first * CH, g_first=g_first, g_total=g_end[-1:], src=src,
                seq=seq, nxt=nxt, counts=counts)


def _experts(seg, x_sorted, w_gate, w_up, w_down, n_assign):
    g_max = _max_chunks(n_assign)
    hbm = pl.BlockSpec(memory_space=pl.ANY)
    grid_spec = pltpu.PrefetchScalarGridSpec(
        num_scalar_prefetch=6,
        grid=(N_EXPERTS,),
        in_specs=[hbm, hbm, hbm, hbm],
        out_specs=hbm,
        scratch_shapes=[pltpu.VMEM((NX, CH * SUBLANES, LANES), F32),
                        pltpu.VMEM((NY, CH * SUBLANES, LANES), F32),
                        pltpu.VMEM((CH * SUBLANES, LANES), F32),
                        pltpu.VMEM((2, D_MODEL, EXPERT_HIDDEN), F32),
                        pltpu.VMEM((2, D_MODEL, EXPERT_HIDDEN), F32),
                        pltpu.VMEM((2, EXPERT_HIDDEN, D_MODEL), F32),
                        pltpu.VMEM((D_MODEL, EXPERT_HIDDEN), BF16),
                        pltpu.VMEM((D_MODEL, EXPERT_HIDDEN), BF16),
                        pltpu.VMEM((EXPERT_HIDDEN, D_MODEL), BF16),
                        pltpu.SemaphoreType.DMA((NX,)), pltpu.SemaphoreType.DMA((NY,)),
                        pltpu.SemaphoreType.DMA(()), pltpu.SemaphoreType.DMA((2,))],
    )
    return pl.pallas_call(
        functools.partial(_expert_kernel, g_max=g_max),
        grid_spec=grid_spec,
        out_shape=jax.ShapeDtypeStruct((g_max * CH * SUBLANES, LANES), F32),
        compiler_params=pltpu.CompilerParams(dimension_semantics=("arbitrary",), vmem_limit_bytes=VMEM_LIMIT,
                                             has_side_effects=True),
        name="experts",
    )(seg["g_first"], seg["counts"], seg["seq"], seg["nxt"], seg["g_total"], seg["src"],
      x_sorted, w_gate, w_up, w_down)


def _combine_kernel(pos_cur_ref, pos_nxt_ref, y_hbm, h_ref, w_ref, wsg_ref, wsu_ref, wsd_ref, l2g_ref, l2b_ref,
                    outp_ref, outs_ref, ybuf, sem, *, n_prompt_tiles):
    t = pl.program_id(0)
    n_t = pl.num_programs(0)
    slot = t % 2

    def gather_start(pos_ref, dst, dst_sem):
        def body(r, carry):
            for k in range(TOP_K):
                j = k * TN_COMBINE + r
                _tile_copy(y_hbm, pos_ref[0, 0, j], dst, j, dst_sem).start(priority=k % 2)
            return carry
        lax.fori_loop(0, TN_COMBINE, body, 0, unroll=8)

    @pl.when(t == 0)
    def _first():
        gather_start(pos_cur_ref, ybuf.at[0], sem.at[0])

    @pl.when(t + 1 < n_t)
    def _prefetch():
        gather_start(pos_nxt_ref, ybuf.at[1 - slot], sem.at[1 - slot])

    h = _tiles_to_rows(h_ref, TN_COMBINE)
    hb = h.astype(BF16)
    g = _dot(hb, wsg_ref[...])
    u = _dot(hb, wsu_ref[...])
    shared = _dot(((g * _sigmoid(g)) * u).astype(BF16), wsd_ref[...])

    yslot = ybuf.at[slot]
    pltpu.make_async_copy(y_hbm.at[pl.ds(0, TOP_K * TN_COMBINE * SUBLANES)], yslot, sem.at[slot]).wait()
    w = w_ref[...]
    wk = [jnp.broadcast_to(w[:, k:k + 1], (TN_COMBINE, LANES)) for k in range(TOP_K)]
    chunks = []
    for s in range(SUBLANES):
        acc = wk[0] * yslot[pl.ds(s, TN_COMBINE, stride=SUBLANES), :]
        for k in range(1, TOP_K):
            acc = acc + wk[k] * yslot[pl.ds(k * TN_COMBINE * SUBLANES + s, TN_COMBINE, stride=SUBLANES), :]
        chunks.append(acc)
    routed = jnp.concatenate(chunks, axis=1)
    out = _layer_norm(ALPHA * h + (routed + shared), l2g_ref[...], l2b_ref[...])

    @pl.when(t < n_prompt_tiles)
    def _prompt_rows():
        outp_ref[...] = out

    @pl.when(t >= n_prompt_tiles)
    def _sample_rows():
        outs_ref[...] = out


def _combine(pos_tiles, y_sorted, h1_tiles, w_tok, wsg, wsu, wsd, l2g, l2b, n_prompt):
    n = h1_tiles.shape[0] // SUBLANES
    n_t = n // TN_COMBINE
    npt = n_prompt // TN_COMBINE
    n_idx = TOP_K * TN_COMBINE
    cspec = lambda shape: pl.BlockSpec(shape, lambda t: (0,) * len(shape))
    return pl.pallas_call(
        functools.partial(_combine_kernel, n_prompt_tiles=npt),
        grid=(n_t,),
        in_specs=[pl.BlockSpec((1, 1, n_idx), lambda t: (t, 0, 0), memory_space=pltpu.SMEM),
                  pl.BlockSpec((1, 1, n_idx), lambda t: (jnp.minimum(t + 1, n_t - 1), 0, 0),
                               memory_space=pltpu.SMEM),
                  pl.BlockSpec(memory_space=pl.ANY),
                  pl.BlockSpec((TN_COMBINE * SUBLANES, LANES), lambda t: (t, 0)),
                  pl.BlockSpec((TN_COMBINE, TOP_K), lambda t: (t, 0)),
                  cspec((D_MODEL, SHARED_HIDDEN)), cspec((D_MODEL, SHARED_HIDDEN)), cspec((SHARED_HIDDEN, D_MODEL)),
                  cspec((1, D_MODEL)), cspec((1, D_MODEL))],
        out_specs=[pl.BlockSpec((TN_COMBINE, D_MODEL), lambda t: (jnp.minimum(t, npt - 1), 0)),
                   pl.BlockSpec((TN_COMBINE, D_MODEL), lambda t: (jnp.maximum(t - npt, 0), 0))],
        out_shape=[jax.ShapeDtypeStruct((n_prompt, D_MODEL), F32),
                   jax.ShapeDtypeStruct((n - n_prompt, D_MODEL), F32)],
        scratch_shapes=[pltpu.VMEM((2, n_idx * SUBLANES, LANES), F32), pltpu.SemaphoreType.DMA((2,))],
        compiler_params=pltpu.CompilerParams(dimension_semantics=("arbitrary",), vmem_limit_bytes=VMEM_LIMIT),
        name="combine",
    )(pos_tiles, pos_tiles, y_sorted, h1_tiles, w_tok, wsg, wsu, wsd, l2g, l2b)


def _pack_mixer_weights(ln_emb_g, ln_emb_b, w_in, w_gate_a2, b_gate, gla_norm_g, w_gla_out, conv_w,
                        w_conv_out, w_o, ln1_g, ln1_b):
    q, k, v, g, a, cb, cc, ch, ga, gb = jnp.split(
        w_in, [512, 1024, 2048, 3072, 3088, 4112, 5136, 6160, 7184], axis=-1)
    w_main = jnp.concatenate([q, k, v, g, cb, cc, ch, ga, gb], axis=-1).astype(BF16)
    w_a = jnp.pad(a, ((0, 0), (0, RANK_PAD - GATE_RANK))).astype(BF16)
    w_a2 = jnp.pad(w_gate_a2, ((0, RANK_PAD - GATE_RANK), (0, 0))).astype(BF16)
    row = lambda x: x.reshape(1, -1).astype(F32)
    return [row(ln_emb_g), row(ln_emb_b), w_main, w_a, w_a2, row(b_gate), row(gla_norm_g),
            w_gla_out.astype(BF16), conv_w.astype(F32), w_conv_out.astype(BF16), w_o.astype(BF16),
            row(ln1_g), row(ln1_b)]


def kernel(x_prompt, x_sample, state_gla, state_conv, meta_tokens, ln_emb_g, ln_emb_b, w_in, w_gate_a2, b_gate, gla_norm_g, w_gla_out, conv_w, w_conv_out, w_o, ln1_g, ln1_b, w_router, router_bias, w_exp_gate, w_exp_up, w_exp_down, w_sh_gate, w_sh_up, w_sh_down, ln2_g, ln2_b):
    batch, seq, _ = x_prompt.shape
    dec_batch, dec_seq, _ = x_sample.shape
    depth = w_in.shape[0]
    assert batch == 1 and depth == 1 and seq % TM == 0 and TM % dec_seq == 0 and dec_seq % SUB == 0
    n_prompt, n_sample = batch * seq, dec_batch * dec_seq
    n_total = n_prompt + n_sample
    assert n_sample % TM == 0 and n_total % TN_ROUTER == 0 and n_total % TN_POSITIONS == 0
    assert n_prompt % TN_COMBINE == 0 and n_sample % TN_COMBINE == 0 and n_total % TN_DISPATCH == 0
    n_assign = n_total * TOP_K

    weights = _pack_mixer_weights(ln_emb_g, ln_emb_b, w_in[0], w_gate_a2[0], b_gate[0], gla_norm_g[0],
                                  w_gla_out[0], conv_w[0], w_conv_out[0], w_o[0], ln1_g[0], ln1_b[0])
    h1, sg_p, sc_p, sg_s, sc_s = _mixer(meta_tokens.astype(F32), x_prompt.reshape(n_prompt, D_MODEL),
                                        x_sample.reshape(n_sample, D_MODEL), state_gla[0], state_conv[0], weights)

    wr_t = w_router[0].T
    wr_hi = wr_t.astype(BF16)
    wr_split = jnp.stack([wr_hi, (wr_t - wr_hi.astype(F32)).astype(BF16)])
    idx_t, w_t, rank_t, counts = _router(h1, wr_split, router_bias[0].reshape(N_EXPERTS, 1).astype(F32))

    seg = _segments(counts.reshape(N_EXPERTS).astype(I32), n_assign)
    table = lambda v: v.astype(F32).reshape(N_EXPERTS, 1)
    posx_t, posy_t = _positions(idx_t, rank_t, table(seg["x_start"]), table(seg["y_start"]))
    tiles = lambda p, tn: p.reshape(TOP_K, n_total // tn, tn).transpose(1, 0, 2).reshape(n_total // tn, 1, -1)

    x_sorted = _dispatch(tiles(posx_t, TN_DISPATCH), h1)
    y_sorted = _experts(seg, x_sorted, w_exp_gate[0], w_exp_up[0], w_exp_down[0], n_assign)
    y_prompt, y_sample = _combine(tiles(posy_t, TN_COMBINE), y_sorted, h1, w_t.T,
                                  w_sh_gate[0].astype(BF16), w_sh_up[0].astype(BF16), w_sh_down[0].astype(BF16),
                                  ln2_g[0].reshape(1, -1), ln2_b[0].reshape(1, -1), n_prompt)
    y_prompt = y_prompt.reshape(batch, seq, D_MODEL)
    y_sample = y_sample.reshape(dec_batch, dec_seq, D_MODEL)
    return (y_prompt, y_sample,
            sg_p.reshape(depth, batch, HEADS, DK, DV), sc_p.reshape(depth, batch, 2, D_MODEL),
            sg_s.reshape(depth, dec_batch, HEADS, DK, DV), sc_s.reshape(depth, dec_batch, 2, D_MODEL))
```

```python
import functools

import jax
import jax.numpy as jnp
from jax import lax
from jax.experimental import pallas as pl
from jax.experimental.pallas import tpu as pltpu

F32 = jnp.float32
BF16 = jnp.bfloat16
I32 = jnp.int32

D_MODEL = 1024
N_META = 16
HEADS = 4
DK = 128
DV = 256
GLA_DK = HEADS * DK
GLA_DV = HEADS * DV
GATE_RANK = 16
GATE_TAU = 16.0
N_EXPERTS = 256
TOP_K = 8
N_GROUPS = 8
GROUP_SIZE = N_EXPERTS // N_GROUPS
TOPK_GROUPS = 4
EXPERT_HIDDEN = 256
SHARED_HIDDEN = 256
ROUTED_SCALE = 2.5
LN_EPS = 1e-5
RMS_EPS = 1e-6
ALPHA = 2.0 ** 0.25
QSCALE = DK ** -0.5

Q0, K0, V0, G0, CB0, CC0, CH0, GA0, GB0 = 0, 512, 1024, 2048, 3072, 4096, 5120, 6144, 7168
W_MAIN_COLS = 8192
RANK_PAD = 128

SUB = 16
PROMPT_CHUNK = 64
TM = 256
TN_ROUTER = 512
TN_POSITIONS = 1536
TN_DISPATCH = 1536
CH = 64
GROUP = 8
X_AHEAD = 12
assert GROUP & (GROUP - 1) == 0
NX = X_AHEAD + GROUP
NY = 2 * GROUP
TN_COMBINE = 512
VMEM_LIMIT = 60 * 1024 * 1024
SUBLANES = 8
LANES = 128
assert D_MODEL == SUBLANES * LANES

_DN_TB = (((1,), (1,)), ((), ()))
_DN_TA = (((0,), (0,)), ((), ()))


def _dot(a, b):
    return jnp.dot(a, b, preferred_element_type=F32)


def _tiles_to_rows(ref, n, base=0):
    return jnp.concatenate(
        [ref[pl.ds(SUBLANES * base + s, n, stride=SUBLANES), :] for s in range(SUBLANES)], axis=1)


def _rows_to_tiles(ref, val):
    n = val.shape[0]
    for s in range(SUBLANES):
        ref[pl.ds(s, n, stride=SUBLANES), :] = val[:, s * LANES:(s + 1) * LANES]


def _layer_norm(x, g, b):
    mu = jnp.mean(x, axis=-1, keepdims=True)
    xc = x - mu
    var = jnp.mean(xc * xc, axis=-1, keepdims=True)
    return xc * lax.rsqrt(var + LN_EPS) * g + b


def _sigmoid(x):
    return 1.0 / (1.0 + jnp.exp(-x))


def _log_sigmoid(x):
    return jnp.minimum(x, 0.0) - jnp.log1p(jnp.exp(-jnp.abs(x)))


def _chunk_cumsum(g, chunk):
    t = g.shape[0]
    shift = chunk.bit_length() - 1
    r = lax.broadcasted_iota(I32, (t, t), 0)
    c = lax.broadcasted_iota(I32, (t, t), 1)
    tri = jnp.where(((r >> shift) == (c >> shift)) & (c <= r), 1.0, 0.0).astype(BF16)
    g1 = g.astype(BF16)
    r1 = g - g1.astype(F32)
    g2 = r1.astype(BF16)
    g3 = (r1 - g2.astype(F32)).astype(BF16)
    return _dot(tri, g1) + _dot(tri, g2) + _dot(tri, g3)


def _gate_log_decay(hb, wa_ref, wa2_ref, bg_ref):
    a = _dot(hb, wa_ref[...])
    return _log_sigmoid(_dot(a.astype(BF16), wa2_ref[...]) + bg_ref[...]) * (1.0 / GATE_TAU)


def _gla_state_update(k, v, b, st):
    n = k.shape[0]
    b_last = b[n - 1:n, :]
    khat = (k * jnp.exp(b_last - b)).astype(BF16)
    return st * jnp.exp(b_last) + lax.dot_general(v.astype(BF16), khat, _DN_TA, preferred_element_type=F32)


def _gla_chunk(q, k, v, b, st):
    n = q.shape[0]
    vb = v.astype(BF16)
    q0 = (q * (jnp.exp(b) * QSCALE)).astype(BF16)
    o = lax.dot_general(q0, st.astype(BF16), _DN_TB, preferred_element_type=F32)
    col = lax.broadcasted_iota(I32, (SUB, n), 1)
    row = lax.broadcasted_iota(I32, (SUB, n), 0)
    blocks = []
    for i in range(n // SUB):
        lo, hi = SUB * i, SUB * (i + 1)
        bref = b[lo:lo + 1, :]
        qi = (q[lo:hi] * (jnp.exp(b[lo:hi] - bref) * QSCALE)).astype(BF16)
        kk = (k[:hi] * jnp.exp(bref - b[:hi])).astype(BF16)
        if hi < n:
            kk = jnp.concatenate([kk, jnp.zeros((n - hi, DK), BF16)], axis=0)
        a = lax.dot_general(qi, kk, _DN_TB, preferred_element_type=F32)
        blocks.append(jnp.where(col <= row + lo, a, 0.0))
    scores = blocks[0] if len(blocks) == 1 else jnp.concatenate(blocks, axis=0)
    o = o + _dot(scores.astype(BF16), vb)
    return o, _gla_state_update(k, v, b, st)


def _mixer_back(h, hb, o_ref, conv, wm_ref, ng_ref, wgo_ref, wco_ref, wo_ref, l1g_ref, l1b_ref):
    g = _dot(hb, wm_ref[:, G0:G0 + GLA_DV])
    parts = []
    for hd in range(HEADS):
        oh = o_ref[:, hd * DV:(hd + 1) * DV]
        ms = jnp.mean(oh * oh, axis=-1, keepdims=True)
        parts.append(oh * lax.rsqrt(ms + RMS_EPS) * ng_ref[...])
    on = jnp.concatenate(parts, axis=1) * (g * _sigmoid(g))
    branch_a = _dot(on.astype(BF16), wgo_ref[...])
    mix = _sigmoid(_dot(hb, wm_ref[:, GA0:GA0 + D_MODEL])) * branch_a
    yc = _dot(hb, wm_ref[:, CB0:CB0 + D_MODEL]) * conv
    branch_b = _dot(yc.astype(BF16), wco_ref[...])
    mix = mix + _sigmoid(_dot(hb, wm_ref[:, GB0:GB0 + D_MODEL])) * branch_b
    mixed = _dot(mix.astype(BF16), wo_ref[...])
    return _layer_norm(ALPHA * h + mixed, l1g_ref[...], l1b_ref[...])


def _conv_input(hb, wm_ref):
    return _dot(hb, wm_ref[:, CC0:CC0 + D_MODEL]) * _dot(hb, wm_ref[:, CH0:CH0 + D_MODEL])


def _gla_rows(qkv_ref, b_ref, rows, hd, st):
    kc = slice(hd * DK, (hd + 1) * DK)
    return _gla_chunk(qkv_ref[rows, Q0 + hd * DK:Q0 + (hd + 1) * DK],
                      qkv_ref[rows, K0 + hd * DK:K0 + (hd + 1) * DK],
                      qkv_ref[rows, V0 + hd * DV:V0 + (hd + 1) * DV],
                      b_ref[rows, kc], st)


def _mixer_kernel(meta_ref, xp_ref, xs_ref, sgin_ref, scin_ref,
                  lng_ref, lnb_ref, wm_ref, wa_ref, wa2_ref, bg_ref,
                  ng_ref, wgo_ref, cw_ref, wco_ref, wo_ref, l1g_ref, l1b_ref,
                  h1_ref, sgp_ref, scp_ref, sgs_ref, scs_ref,
                  st_ref, zbuf_ref, qkv_ref, b_ref, o_ref, *, n_prompt_tiles, seq, streams):
    i = pl.program_id(0)
    back = functools.partial(_mixer_back, wm_ref=wm_ref, ng_ref=ng_ref, wgo_ref=wgo_ref, wco_ref=wco_ref,
                             wo_ref=wo_ref, l1g_ref=l1g_ref, l1b_ref=l1b_ref)

    def front(x):
        h = _layer_norm(x, lng_ref[...], lnb_ref[...])
        hb = h.astype(BF16)
        return h, hb, _gate_log_decay(hb, wa_ref, wa2_ref, bg_ref)

    @pl.when(i == 0)
    def _meta():
        _, hb, gk = front(meta_ref[...])
        k = _dot(hb, wm_ref[:, K0:K0 + GLA_DK])
        v = _dot(hb, wm_ref[:, V0:V0 + GLA_DV])
        b = _chunk_cumsum(gk, N_META)
        zero_state = jnp.zeros((DV, DK), F32)
        for hd in range(HEADS):
            kc = slice(hd * DK, (hd + 1) * DK)
            st_ref[hd] = _gla_state_update(k[:, kc], v[:, hd * DV:(hd + 1) * DV], b[:, kc], zero_state)
        zbuf_ref[0:8, :] = _conv_input(hb, wm_ref)[N_META - 8:N_META, :]

    @pl.when((i >= 1) & (i <= n_prompt_tiles))
    def _prompt_tile():
        h, hb, gk = front(xp_ref[...])
        qkv_ref[...] = _dot(hb, wm_ref[:, Q0:G0])
        b_ref[...] = _chunk_cumsum(gk, PROMPT_CHUNK)
        for c in range(TM // PROMPT_CHUNK):
            rows = slice(c * PROMPT_CHUNK, (c + 1) * PROMPT_CHUNK)
            for hd in range(HEADS):
                o, st_new = _gla_rows(qkv_ref, b_ref, rows, hd, st_ref[hd])
                st_ref[hd] = st_new
                o_ref[rows, hd * DV:(hd + 1) * DV] = o
        z = _conv_input(hb, wm_ref)
        zbuf_ref[8:8 + TM, :] = z
        cw = cw_ref[...]
        conv = cw[0:1, :] * zbuf_ref[6:6 + TM, :] + cw[1:2, :] * zbuf_ref[7:7 + TM, :] + cw[2:3, :] * z
        zbuf_ref[0:8, :] = z[TM - 8:TM, :]
        _rows_to_tiles(h1_ref, back(h, hb, o_ref, conv))

        @pl.when(i == n_prompt_tiles)
        def _prompt_states():
            for hd in range(HEADS):
                sgp_ref[hd] = st_ref[hd].T
            scp_ref[...] = z[TM - 2:TM, :]

    @pl.when(i > n_prompt_tiles)
    def _sample_tile():
        h, hb, gk = front(xs_ref[...])
        qkv_ref[...] = _dot(hb, wm_ref[:, Q0:G0])
        b_ref[...] = _chunk_cumsum(gk, seq)
        z = _conv_input(hb, wm_ref)
        pitch = seq + 8
        cw = cw_ref[...]
        convs = []
        for s in range(streams):
            rows = slice(s * seq, (s + 1) * seq)
            for hd in range(HEADS):
                o, st_new = _gla_rows(qkv_ref, b_ref, rows, hd, sgin_ref[s, hd].T)
                sgs_ref[s, hd] = st_new.T
                o_ref[rows, hd * DV:(hd + 1) * DV] = o
            zs = z[rows, :]
            base = s * pitch
            zbuf_ref[base + 6:base + 8, :] = scin_ref[s]
            zbuf_ref[base + 8:base + 8 + seq, :] = zs
            convs.append(cw[0:1, :] * zbuf_ref[base + 6:base + 6 + seq, :]
                         + cw[1:2, :] * zbuf_ref[base + 7:base + 7 + seq, :] + cw[2:3, :] * zs)
            scs_ref[s] = zs[seq - 2:seq, :]
        _rows_to_tiles(h1_ref, back(h, hb, o_ref, jnp.concatenate(convs, axis=0)))


def _const_spec(shape):
    nd = len(shape)
    return pl.BlockSpec(shape, lambda i, _nd=nd: (0,) * _nd, pipeline_mode=pl.Buffered(1))


def _mixer_weight_specs():
    return [
        _const_spec((1, D_MODEL)), _const_spec((1, D_MODEL)),
        _const_spec((D_MODEL, W_MAIN_COLS)),
        _const_spec((D_MODEL, RANK_PAD)), _const_spec((RANK_PAD, GLA_DK)), _const_spec((1, GLA_DK)),
        _const_spec((1, DV)), _const_spec((GLA_DV, D_MODEL)),
        _const_spec((3, D_MODEL)), _const_spec((D_MODEL, D_MODEL)),
        _const_spec((D_MODEL, D_MODEL)),
        _const_spec((1, D_MODEL)), _const_spec((1, D_MODEL)),
    ]


def _mixer(meta, x_prompt, x_sample, state_gla, state_conv, weights):
    n_streams = state_conv.shape[0]
    seq = x_sample.shape[0] // n_streams
    streams = TM // seq
    npt = x_prompt.shape[0] // TM
    nst = n_streams // streams
    ptile = lambda i: (jnp.clip(i - 1, 0, npt - 1), 0)
    stile = lambda i: jnp.clip(i - 1 - npt, 0, nst - 1)
    kern = functools.partial(_mixer_kernel, n_prompt_tiles=npt, seq=seq, streams=streams)
    return pl.pallas_call(
        kern,
        grid=(1 + npt + nst,),
        in_specs=[_const_spec((N_META, D_MODEL)),
                  pl.BlockSpec((TM, D_MODEL), ptile),
                  pl.BlockSpec((TM, D_MODEL), lambda i: (stile(i), 0), pipeline_mode=pl.Buffered(1)),
                  pl.BlockSpec((streams, HEADS, DK, DV), lambda i: (stile(i), 0, 0, 0),
                               pipeline_mode=pl.Buffered(1)),
                  pl.BlockSpec((streams, 2, D_MODEL), lambda i: (stile(i), 0, 0))] + _mixer_weight_specs(),
        out_specs=[pl.BlockSpec((TM * SUBLANES, LANES), lambda i: (jnp.maximum(i - 1, 0), 0)),
                   pl.BlockSpec((HEADS, DK, DV), lambda i: (0, 0, 0)),
                   pl.BlockSpec((2, D_MODEL), lambda i: (0, 0)),
                   pl.BlockSpec((streams, HEADS, DK, DV), lambda i: (stile(i), 0, 0, 0)),
                   pl.BlockSpec((streams, 2, D_MODEL), lambda i: (stile(i), 0, 0))],
        out_shape=[jax.ShapeDtypeStruct(((x_prompt.shape[0] + x_sample.shape[0]) * SUBLANES, LANES), F32),
                   jax.ShapeDtypeStruct((HEADS, DK, DV), F32),
                   jax.ShapeDtypeStruct((2, D_MODEL), F32),
                   jax.ShapeDtypeStruct(state_gla.shape, F32),
                   jax.ShapeDtypeStruct(state_conv.shape, F32)],
        scratch_shapes=[pltpu.VMEM((HEADS, DV, DK), F32),
                        pltpu.VMEM((max(TM + 8, streams * (seq + 8)), D_MODEL), F32),
                        pltpu.VMEM((TM, G0), F32),
                        pltpu.VMEM((TM, GLA_DK), F32),
                        pltpu.VMEM((TM, GLA_DV), F32)],
        compiler_params=pltpu.CompilerParams(dimension_semantics=("arbitrary",), vmem_limit_bytes=VMEM_LIMIT),
        name="mixer",
    )(meta, x_prompt, x_sample, state_gla, state_conv, *weights)


def _first_index_of_max(vals, ids, sentinel):
    m = jnp.max(vals, axis=0, keepdims=True)
    first = jnp.min(jnp.where(vals == m, ids, sentinel), axis=0, keepdims=True)
    return m, first


def _router_kernel(h_ref, wr_ref, bias_ref, idx_ref, w_ref, rank_ref, counts_ref):
    @pl.when(pl.program_id(0) == 0)
    def _init():
        counts_ref[...] = jnp.zeros_like(counts_ref)

    h = _tiles_to_rows(h_ref, TN_ROUTER)
    hh = h.astype(BF16)
    hl = (h - hh.astype(F32)).astype(BF16)
    wh, wl = wr_ref[0], wr_ref[1]
    dg = functools.partial(lax.dot_general, dimension_numbers=_DN_TB, preferred_element_type=F32)
    logits = dg(wh, hh) + dg(wl, hh) + dg(wh, hl)
    scores = _sigmoid(logits)
    biased = scores + bias_ref[...]
    n_tok = biased.shape[1]
    neg_inf = jnp.float32(-jnp.inf)

    eid = lax.broadcasted_iota(I32, (N_EXPERTS, n_tok), 0).astype(F32)
    lid = lax.broadcasted_iota(I32, (GROUP_SIZE, n_tok), 0).astype(F32)
    group_scores = []
    for g in range(N_GROUPS):
        blk = biased[g * GROUP_SIZE:(g + 1) * GROUP_SIZE]
        m1, first = _first_index_of_max(blk, lid, float(GROUP_SIZE))
        m2 = jnp.max(jnp.where(lid == first, neg_inf, blk), axis=0, keepdims=True)
        group_scores.append(m1 + m2)
    gsc = jnp.concatenate(group_scores, axis=0)

    gid = lax.broadcasted_iota(I32, (N_GROUPS, n_tok), 0).astype(F32)
    keep = jnp.zeros((N_GROUPS, n_tok), F32)
    cur = gsc
    for _ in range(TOPK_GROUPS):
        _, first = _first_index_of_max(cur, gid, float(N_GROUPS))
        sel = gid == first
        keep = jnp.where(sel, 1.0, keep)
        cur = jnp.where(sel, neg_inf, cur)

    masked = jnp.concatenate(
        [jnp.where(keep[g:g + 1] > 0.5, biased[g * GROUP_SIZE:(g + 1) * GROUP_SIZE], neg_inf)
         for g in range(N_GROUPS)], axis=0)
    idxs, wts, sels = [], [], []
    cur = masked
    for _ in range(TOP_K):
        _, first = _first_index_of_max(cur, eid, float(N_EXPERTS))
        sel = eid == first
        idxs.append(first)
        sels.append(sel)
        wts.append(jnp.sum(jnp.where(sel, scores, 0.0), axis=0, keepdims=True))
        cur = jnp.where(sel, neg_inf, cur)
    w = jnp.concatenate(wts, axis=0)
    idx_ref[...] = jnp.concatenate(idxs, axis=0).astype(I32)
    w_ref[...] = w / jnp.sum(w, axis=0, keepdims=True) * ROUTED_SCALE

    chosen = jnp.where(sels[0], 1.0, 0.0)
    for sel in sels[1:]:
        chosen = chosen + jnp.where(sel, 1.0, 0.0)
    t_src = lax.broadcasted_iota(I32, (n_tok, n_tok), 0)
    t_dst = lax.broadcasted_iota(I32, (n_tok, n_tok), 1)
    earlier = jnp.where(t_src < t_dst, 1.0, 0.0).astype(BF16)
    arrivals = counts_ref[...] + _dot(chosen.astype(BF16), earlier)
    rank_ref[...] = jnp.concatenate(
        [jnp.sum(jnp.where(sel, arrivals, 0.0), axis=0, keepdims=True) for sel in sels], axis=0).astype(I32)
    counts_ref[...] += jnp.sum(chosen, axis=1, keepdims=True)


def _router(h1_tiles, wr_split, bias_col):
    n = h1_tiles.shape[0] // SUBLANES
    tok = pl.BlockSpec((TOP_K, TN_ROUTER), lambda i: (0, i))
    return pl.pallas_call(
        _router_kernel,
        grid=(n // TN_ROUTER,),
        in_specs=[pl.BlockSpec((TN_ROUTER * SUBLANES, LANES), lambda i: (i, 0)),
                  pl.BlockSpec((2, N_EXPERTS, D_MODEL), lambda i: (0, 0, 0)),
                  pl.BlockSpec((N_EXPERTS, 1), lambda i: (0, 0))],
        out_specs=[tok, tok, tok, pl.BlockSpec((N_EXPERTS, 1), lambda i: (0, 0))],
        out_shape=[jax.ShapeDtypeStruct((TOP_K, n), I32), jax.ShapeDtypeStruct((TOP_K, n), F32),
                   jax.ShapeDtypeStruct((TOP_K, n), I32), jax.ShapeDtypeStruct((N_EXPERTS, 1), F32)],
        compiler_params=pltpu.CompilerParams(dimension_semantics=("arbitrary",), vmem_limit_bytes=VMEM_LIMIT),
        name="router",
    )(h1_tiles, wr_split, bias_col)


def _positions_kernel(idx_ref, rank_ref, segx_ref, segy_ref, posx_ref, posy_ref):
    n_tok = idx_ref.shape[1]
    eid = lax.broadcasted_iota(I32, (N_EXPERTS, n_tok), 0)
    segx, segy = segx_ref[...], segy_ref[...]
    rows_x, rows_y = [], []
    for k in range(TOP_K):
        onehot = eid == idx_ref[k:k + 1, :]
        rows_x.append(jnp.sum(jnp.where(onehot, segx, 0.0), axis=0, keepdims=True))
        rows_y.append(jnp.sum(jnp.where(onehot, segy, 0.0), axis=0, keepdims=True))
    posx_ref[...] = jnp.concatenate(rows_x, axis=0).astype(I32) + rank_ref[...]
    posy_ref[...] = jnp.concatenate(rows_y, axis=0).astype(I32) + rank_ref[...]


def _positions(idx_t, rank_t, segx_col, segy_col):
    n = idx_t.shape[1]
    tok = pl.BlockSpec((TOP_K, TN_POSITIONS), lambda i: (0, i))
    table = pl.BlockSpec((N_EXPERTS, 1), lambda i: (0, 0))
    return pl.pallas_call(
        _positions_kernel,
        grid=(n // TN_POSITIONS,),
        in_specs=[tok, tok, table, table],
        out_specs=[tok, tok],
        out_shape=[jax.ShapeDtypeStruct((TOP_K, n), I32), jax.ShapeDtypeStruct((TOP_K, n), I32)],
        compiler_params=pltpu.CompilerParams(dimension_semantics=("arbitrary",)),
        name="positions",
    )(idx_t, rank_t, segx_col, segy_col)


def _tile_copy(src, src_tile, dst, dst_tile, sem):
    return pltpu.make_async_copy(src.at[pl.ds(pl.multiple_of(src_tile * SUBLANES, SUBLANES), SUBLANES)],
                                 dst.at[pl.ds(pl.multiple_of(dst_tile * SUBLANES, SUBLANES), SUBLANES)], sem)


def _chunk_rows(first_tile):
    return pl.ds(pl.multiple_of(first_tile * SUBLANES, SUBLANES), CH * SUBLANES)


def _dispatch_kernel(pos_ref, h_ref, x_hbm, zeros, sem, zsem, *, n_assign):
    @pl.when(pl.program_id(0) == 0)
    def _zero_slack():
        zeros[...] = jnp.zeros_like(zeros)
        fill = pltpu.make_async_copy(zeros, x_hbm.at[_chunk_rows(n_assign)], zsem)
        fill.start()
        fill.wait()

    def body(t, carry):
        for k in range(TOP_K):
            _tile_copy(h_ref, t, x_hbm, pos_ref[0, 0, k * TN_DISPATCH + t], sem.at[k]).start(priority=k % 2)
        return carry
    lax.fori_loop(0, TN_DISPATCH, body, 0, unroll=8)
    for k in range(TOP_K):
        pltpu.make_async_copy(h_ref, x_hbm.at[pl.ds(0, TN_DISPATCH * SUBLANES)], sem.at[k]).wait()


def _dispatch(pos_tiles, h1_tiles):
    n_t = pos_tiles.shape[0]
    n_assign = n_t * TN_DISPATCH * TOP_K
    return pl.pallas_call(
        functools.partial(_dispatch_kernel, n_assign=n_assign),
        grid=(n_t,),
        in_specs=[pl.BlockSpec((1, 1, TOP_K * TN_DISPATCH), lambda t: (t, 0, 0), memory_space=pltpu.SMEM),
                  pl.BlockSpec((TN_DISPATCH * SUBLANES, LANES), lambda t: (t, 0))],
        out_specs=pl.BlockSpec(memory_space=pl.ANY),
        out_shape=jax.ShapeDtypeStruct(((n_assign + CH) * SUBLANES, LANES), F32),
        scratch_shapes=[pltpu.VMEM((CH * SUBLANES, LANES), F32), pltpu.SemaphoreType.DMA((TOP_K,)),
                        pltpu.SemaphoreType.DMA(())],
        compiler_params=pltpu.CompilerParams(dimension_semantics=("arbitrary",), has_side_effects=True),
        name="dispatch",
    )(pos_tiles, h1_tiles)


def _expert_kernel(gfirst_ref, cnt_ref, seq_ref, nxt_ref, gtot_ref, src_ref, x_hbm, wg_hbm, wu_hbm, wd_hbm, y_hbm,
                   xbuf, ybuf, zbuf, wg32, wu32, wd32, wgb, wub, wdb, xsem, ysem, zsem, wsem, *, g_max):
    e = pl.program_id(0)
    cnt = cnt_ref[e]
    g_total = gtot_ref[0]

    def x_copy(g, slot):
        return pltpu.make_async_copy(x_hbm.at[_chunk_rows(src_ref[g])], xbuf.at[slot], xsem.at[slot])

    def y_copy(g, slot):
        return pltpu.make_async_copy(ybuf.at[slot], y_hbm.at[_chunk_rows(g * CH)], ysem.at[slot])

    def tail_fill(g):
        return pltpu.make_async_copy(zbuf, y_hbm.at[_chunk_rows(g * CH)], zsem)

    def for_tail_chunks(fn):
        def body(g, carry):
            fn(g)
            return carry
        lax.fori_loop(g_total, g_max, body, 0)

    def weight_copies(ex, slot):
        return (pltpu.make_async_copy(wg_hbm.at[ex], wg32.at[slot], wsem.at[slot]),
                pltpu.make_async_copy(wu_hbm.at[ex], wu32.at[slot], wsem.at[slot]),
                pltpu.make_async_copy(wd_hbm.at[ex], wd32.at[slot], wsem.at[slot]))

    @pl.when(e == 0)
    def _prime():
        for g in range(X_AHEAD):
            @pl.when(g < g_total)
            def _():
                x_copy(g, g).start()
        zbuf[...] = jnp.zeros_like(zbuf)
        for_tail_chunks(lambda g: tail_fill(g).start())

    @pl.when(cnt > 0)
    def _expert():
        seq, nxt = seq_ref[e], nxt_ref[e]
        wslot = lax.rem(seq, 2)

        @pl.when(seq == 0)
        def _fetch_own():
            for c in weight_copies(e, wslot):
                c.start(priority=1)

        @pl.when(nxt >= 0)
        def _prefetch_weights():
            for c in weight_copies(nxt, 1 - wslot):
                c.start(priority=1)

        for c in weight_copies(e, wslot):
            c.wait()
        wgb[...] = wg32[wslot].astype(BF16)
        wub[...] = wu32[wslot].astype(BF16)
        wdb[...] = wd32[wslot].astype(BF16)

        def process(g, width):
            for i in range(width):
                x_copy(0, lax.rem(g + i, NX)).wait()
            for i in range(width):
                ahead = g + i + X_AHEAD

                @pl.when(ahead < g_total)
                def _fetch_ahead():
                    x_copy(ahead, lax.rem(ahead, NX)).start()

            xb = jnp.concatenate([_tiles_to_rows(xbuf.at[lax.rem(g + i, NX)], CH) for i in range(width)],
                                 axis=0).astype(BF16)
            gate = _dot(xb, wgb[...])
            up = _dot(xb, wub[...])
            res = _dot(((gate * _sigmoid(gate)) * up).astype(BF16), wdb[...])
            for i in range(width):
                ys = lax.rem(g + i, NY)

                @pl.when(g + i >= NY)
                def _slot_written_back():
                    y_copy(0, ys).wait()

                _rows_to_tiles(ybuf.at[ys], res[i * CH:(i + 1) * CH])
                y_copy(g + i, ys).start()

        g_first = gfirst_ref[e]
        n_chunks = (cnt + CH - 1) // CH
        n_wide = n_chunks // GROUP
        rest = n_chunks - n_wide * GROUP

        def wide_group(j, carry):
            process(g_first + j * GROUP, GROUP)
            return carry

        lax.fori_loop(0, n_wide, wide_group, 0)
        g_rest = g_first + n_wide * GROUP
        width = GROUP // 2
        while width >= 1:
            @pl.when((rest & width) != 0)
            def _narrow_group(width=width):
                process(g_rest + (rest & ~(2 * width - 1)), width)
            width //= 2

    @pl.when(e == pl.num_programs(0) - 1)
    def _finish():
        for back in range(1, NY + 1):
            @pl.when(g_total - back >= 0)
            def _():
                y_copy(0, lax.rem(g_total - back, NY)).wait()

        for_tail_chunks(lambda g: tail_fill(0).wait())


def _max_chunks(n_assign):
    return n_assign // CH + N_EXPERTS


def _segments(counts, n_assign):
    n_chunks = (counts + CH - 1) // CH
    g_end = jnp.cumsum(n_chunks).astype(I32)
    g_first = g_end - n_chunks
    x_start = (jnp.cumsum(counts) - counts).astype(I32)
    chunk = jnp.arange(_max_chunks(n_assign), dtype=I32)
    eid = jnp.arange(N_EXPERTS, dtype=I32)
    owner = jnp.minimum(jnp.sum((g_end[None, :] <= chunk[:, None]).astype(I32), axis=1), N_EXPERTS - 1)
    onehot = owner[:, None] == eid[None, :]
    pick = lambda v: jnp.sum(jnp.where(onehot, v[None, :], 0), axis=1)
    src = jnp.minimum(pick(x_start) + (chunk - pick(g_first)) * CH, n_assign).astype(I32)
    has_rows = counts > 0
    seq = jnp.where(has_rows, jnp.cumsum(has_rows.astype(I32)) - 1, -1).astype(I32)
    nxt = jnp.min(jnp.where((eid[None, :] > eid[:, None]) & has_rows[None, :], eid[None, :], N_EXPERTS), axis=1)
    nxt = jnp.where(nxt == N_EXPERTS, -1, nxt).astype(I32)
    return dict(x_start=x_start, y_start=g_first * CH, g_first=g_first, g_total=g_end[-1:], src=src,
                seq=seq, nxt=nxt, counts=counts)


def _experts(seg, x_sorted, w_gate, w_up, w_down, n_assign):
    g_max = _max_chunks(n_assign)
    hbm = pl.BlockSpec(memory_space=pl.ANY)
    grid_spec = pltpu.PrefetchScalarGridSpec(
        num_scalar_prefetch=6,
        grid=(N_EXPERTS,),
        in_specs=[hbm, hbm, hbm, hbm],
        out_specs=hbm,
        scratch_shapes=[pltpu.VMEM((NX, CH * SUBLANES, LANES), F32),
                        pltpu.VMEM((NY, CH * SUBLANES, LANES), F32),
                        pltpu.VMEM((CH * SUBLANES, LANES), F32),
                        pltpu.VMEM((2, D_MODEL, EXPERT_HIDDEN), F32),
                        pltpu.VMEM((2, D_MODEL, EXPERT_HIDDEN), F32),
                        pltpu.VMEM((2, EXPERT_HIDDEN, D_MODEL), F32),
                        pltpu.VMEM((D_MODEL, EXPERT_HIDDEN), BF16),
                        pltpu.VMEM((D_MODEL, EXPERT_HIDDEN), BF16),
                        pltpu.VMEM((EXPERT_HIDDEN, D_MODEL), BF16),
                        pltpu.SemaphoreType.DMA((NX,)), pltpu.SemaphoreType.DMA((NY,)),
                        pltpu.SemaphoreType.DMA(()), pltpu.SemaphoreType.DMA((2,))],
    )
    return pl.pallas_call(
        functools.partial(_expert_kernel, g_max=g_max),
        grid_spec=grid_spec,
        out_shape=jax.ShapeDtypeStruct((g_max * CH * SUBLANES, LANES), F32),
        compiler_params=pltpu.CompilerParams(dimension_semantics=("arbitrary",), vmem_limit_bytes=VMEM_LIMIT,
                                             has_side_effects=True),
        name="experts",
    )(seg["g_first"], seg["counts"], seg["seq"], seg["nxt"], seg["g_total"], seg["src"],
      x_sorted, w_gate, w_up, w_down)


def _combine_kernel(pos_cur_ref, pos_nxt_ref, y_hbm, h_ref, w_ref, wsg_ref, wsu_ref, wsd_ref, l2g_ref, l2b_ref,
                    outp_ref, outs_ref, ybuf, sem, *, n_prompt_tiles):
    t = pl.program_id(0)
    n_t = pl.num_programs(0)
    slot = t % 2

    def gather_start(pos_ref, dst, dst_sem):
        def body(r, carry):
            for k in range(TOP_K):
                j = k * TN_COMBINE + r
                _tile_copy(y_hbm, pos_ref[0, 0, j], dst, j, dst_sem).start(priority=k % 2)
            return carry
        lax.fori_loop(0, TN_COMBINE, body, 0, unroll=8)

    @pl.when(t == 0)
    def _first():
        gather_start(pos_cur_ref, ybuf.at[0], sem.at[0])

    @pl.when(t + 1 < n_t)
    def _prefetch():
        gather_start(pos_nxt_ref, ybuf.at[1 - slot], sem.at[1 - slot])

    h = _tiles_to_rows(h_ref, TN_COMBINE)
    hb = h.astype(BF16)
    g = _dot(hb, wsg_ref[...])
    u = _dot(hb, wsu_ref[...])
    shared = _dot(((g * _sigmoid(g)) * u).astype(BF16), wsd_ref[...])

    yslot = ybuf.at[slot]
    pltpu.make_async_copy(y_hbm.at[pl.ds(0, TOP_K * TN_COMBINE * SUBLANES)], yslot, sem.at[slot]).wait()
    w = w_ref[...]
    wk = [jnp.broadcast_to(w[:, k:k + 1], (TN_COMBINE, LANES)) for k in range(TOP_K)]
    chunks = []
    for s in range(SUBLANES):
        acc = wk[0] * yslot[pl.ds(s, TN_COMBINE, stride=SUBLANES), :]
        for k in range(1, TOP_K):
            acc = acc + wk[k] * yslot[pl.ds(k * TN_COMBINE * SUBLANES + s, TN_COMBINE, stride=SUBLANES), :]
        chunks.append(acc)
    routed = jnp.concatenate(chunks, axis=1)
    out = _layer_norm(ALPHA * h + (routed + shared), l2g_ref[...], l2b_ref[...])

    @pl.when(t < n_prompt_tiles)
    def _prompt_rows():
        outp_ref[...] = out

    @pl.when(t >= n_prompt_tiles)
    def _sample_rows():
        outs_ref[...] = out


def _combine(pos_tiles, y_sorted, h1_tiles, w_tok, wsg, wsu, wsd, l2g, l2b, n_prompt):
    n = h1_tiles.shape[0] // SUBLANES
    n_t = n // TN_COMBINE
    npt = n_prompt // TN_COMBINE
    n_idx = TOP_K * TN_COMBINE
    cspec = lambda shape: pl.BlockSpec(shape, lambda t: (0,) * len(shape))
    return pl.pallas_call(
        functools.partial(_combine_kernel, n_prompt_tiles=npt),
        grid=(n_t,),
        in_specs=[pl.BlockSpec((1, 1, n_idx), lambda t: (t, 0, 0), memory_space=pltpu.SMEM),
                  pl.BlockSpec((1, 1, n_idx), lambda t: (jnp.minimum(t + 1, n_t - 1), 0, 0),
                               memory_space=pltpu.SMEM),
                  pl.BlockSpec(memory_space=pl.ANY),
                  pl.BlockSpec((TN_COMBINE * SUBLANES, LANES), lambda t: (t, 0)),
                  pl.BlockSpec((TN_COMBINE, TOP_K), lambda t: (t, 0)),
                  cspec((D_MODEL, SHARED_HIDDEN)), cspec((D_MODEL, SHARED_HIDDEN)), cspec((SHARED_HIDDEN, D_MODEL)),
                  cspec((1, D_MODEL)), cspec((1, D_MODEL))],
        out_specs=[pl.BlockSpec((TN_COMBINE, D_MODEL), lambda t: (jnp.minimum(t, npt - 1), 0)),
                   pl.BlockSpec((TN_COMBINE, D_MODEL), lambda t: (jnp.maximum(t - npt, 0), 0))],
        out_shape=[jax.ShapeDtypeStruct((n_prompt, D_MODEL), F32),
                   jax.ShapeDtypeStruct((n - n_prompt, D_MODEL), F32)],
        scratch_shapes=[pltpu.VMEM((2, n_idx * SUBLANES, LANES), F32), pltpu.SemaphoreType.DMA((2,))],
        compiler_params=pltpu.CompilerParams(dimension_semantics=("arbitrary",), vmem_limit_bytes=VMEM_LIMIT),
        name="combine",
    )(pos_tiles, pos_tiles, y_sorted, h1_tiles, w_tok, wsg, wsu, wsd, l2g, l2b)


def _pack_mixer_weights(ln_emb_g, ln_emb_b, w_in, w_gate_a2, b_gate, gla_norm_g, w_gla_out, conv_w,
                        w_conv_out, w_o, ln1_g, ln1_b):
    q, k, v, g, a, cb, cc, ch, ga, gb = jnp.split(
        w_in, [512, 1024, 2048, 3072, 3088, 4112, 5136, 6160, 7184], axis=-1)
    w_main = jnp.concatenate([q, k, v, g, cb, cc, ch, ga, gb], axis=-1).astype(BF16)
    w_a = jnp.pad(a, ((0, 0), (0, RANK_PAD - GATE_RANK))).astype(BF16)
    w_a2 = jnp.pad(w_gate_a2, ((0, RANK_PAD - GATE_RANK), (0, 0))).astype(BF16)
    row = lambda x: x.reshape(1, -1).astype(F32)
    return [row(ln_emb_g), row(ln_emb_b), w_main, w_a, w_a2, row(b_gate), row(gla_norm_g),
            w_gla_out.astype(BF16), conv_w.astype(F32), w_conv_out.astype(BF16), w_o.astype(BF16),
            row(ln1_g), row(ln1_b)]


def kernel(x_prompt, x_sample, state_gla, state_conv, meta_tokens, ln_emb_g, ln_emb_b, w_in, w_gate_a2, b_gate, gla_norm_g, w_gla_out, conv_w, w_conv_out, w_o, ln1_g, ln1_b, w_router, router_bias, w_exp_gate, w_exp_up, w_exp_down, w_sh_gate, w_sh_up, w_sh_down, ln2_g, ln2_b):
    batch, seq, _ = x_prompt.shape
    dec_batch, dec_seq, _ = x_sample.shape
    depth = w_in.shape[0]
    assert batch == 1 and depth == 1 and seq % TM == 0 and TM % dec_seq == 0 and dec_seq % SUB == 0
    n_prompt, n_sample = batch * seq, dec_batch * dec_seq
    n_total = n_prompt + n_sample
    assert n_sample % TM == 0 and n_total % TN_ROUTER == 0 and n_total % TN_POSITIONS == 0
    assert n_prompt % TN_COMBINE == 0 and n_sample % TN_COMBINE == 0 and n_total % TN_DISPATCH == 0
    n_assign = n_total * TOP_K

    weights = _pack_mixer_weights(ln_emb_g, ln_emb_b, w_in[0], w_gate_a2[0], b_gate[0], gla_norm_g[0],
                                  w_gla_out[0], conv_w[0], w_conv_out[0], w_o[0], ln1_g[0], ln1_b[0])
    h1, sg_p, sc_p, sg_s, sc_s = _mixer(meta_tokens.astype(F32), x_prompt.reshape(n_prompt, D_MODEL),
                                        x_sample.reshape(n_sample, D_MODEL), state_gla[0], state_conv[0], weights)

    wr_t = w_router[0].T
    wr_hi = wr_t.astype(BF16)
    wr_split = jnp.stack([wr_hi, (wr_t - wr_hi.astype(F32)).astype(BF16)])
    idx_t, w_t, rank_t, counts = _router(h1, wr_split, router_bias[0].reshape(N_EXPERTS, 1).astype(F32))

    seg = _segments(counts.reshape(N_EXPERTS).astype(I32), n_assign)
    table = lambda v: v.astype(F32).reshape(N_EXPERTS, 1)
    posx_t, posy_t = _positions(idx_t, rank_t, table(seg["x_start"]), table(seg["y_start"]))
    tiles = lambda p, tn: p.reshape(TOP_K, n_total // tn, tn).transpose(1, 0, 2).reshape(n_total // tn, 1, -1)

    x_sorted = _dispatch(tiles(posx_t, TN_DISPATCH), h1)
    y_sorted = _experts(seg, x_sorted, w_exp_gate[0], w_exp_up[0], w_exp_down[0], n_assign)
    y_prompt, y_sample = _combine(tiles(posy_t, TN_COMBINE), y_sorted, h1, w_t.T,
                                  w_sh_gate[0].astype(BF16), w_sh_up[0].astype(BF16), w_sh_down[0].astype(BF16),
                                  ln2_g[0].reshape(1, -1), ln2_b[0].reshape(1, -1), n_prompt)
    y_prompt = y_prompt.reshape(batch, seq, D_MODEL)
    y_sample = y_sample.reshape(dec_batch, dec_seq, D_MODEL)
    return (y_prompt, y_sample,
            sg_p.reshape(depth, batch, HEADS, DK, DV), sc_p.reshape(depth, batch, 2, D_MODEL),
            sg_s.reshape(depth, dec_batch, HEADS, DK, DV), sc_s.reshape(depth, dec_batch, 2, D_MODEL))
```

```python
import functools

import jax
import jax.numpy as jnp
from jax import lax
from jax.experimental import pallas as pl
from jax.experimental.pallas import tpu as pltpu

F32 = jnp.float32
BF16 = jnp.bfloat16
I32 = jnp.int32

D_MODEL = 1024
N_META = 16
HEADS = 4
DK = 128
DV = 256
GLA_DK = HEADS * DK
GLA_DV = HEADS * DV
GATE_RANK = 16
GATE_TAU = 16.0
N_EXPERTS = 256
TOP_K = 8
N_GROUPS = 8
GROUP_SIZE = N_EXPERTS // N_GROUPS
TOPK_GROUPS = 4
EXPERT_HIDDEN = 256
SHARED_HIDDEN = 256
ROUTED_SCALE = 2.5
LN_EPS = 1e-5
RMS_EPS = 1e-6
ALPHA = 2.0 ** 0.25
QSCALE = DK ** -0.5

Q0, K0, V0, G0, CB0, CC0, CH0, GA0, GB0 = 0, 512, 1024, 2048, 3072, 4096, 5120, 6144, 7168
W_MAIN_COLS = 8192
RANK_PAD = 128

SUB = 16
PROMPT_CHUNK = 64
TM = 256
TN_ROUTER = 512
TN_POSITIONS = 1536
TN_DISPATCH = 1536
CH = 64
GROUP = 8
X_AHEAD = 12
assert GROUP & (GROUP - 1) == 0
NX = X_AHEAD + GROUP
NY = 2 * GROUP
TN_COMBINE = 256
VMEM_LIMIT = 60 * 1024 * 1024
SUBLANES = 8
LANES = 128
assert D_MODEL == SUBLANES * LANES

_DN_TB = (((1,), (1,)), ((), ()))
_DN_TA = (((0,), (0,)), ((), ()))


def _dot(a, b):
    return jnp.dot(a, b, preferred_element_type=F32)


def _tiles_to_rows(ref, n, base=0):
    return jnp.concatenate(
        [ref[pl.ds(SUBLANES * base + s, n, stride=SUBLANES), :] for s in range(SUBLANES)], axis=1)


def _rows_to_tiles(ref, val):
    n = val.shape[0]
    for s in range(SUBLANES):
        ref[pl.ds(s, n, stride=SUBLANES), :] = val[:, s * LANES:(s + 1) * LANES]


def _layer_norm(x, g, b):
    mu = jnp.mean(x, axis=-1, keepdims=True)
    xc = x - mu
    var = jnp.mean(xc * xc, axis=-1, keepdims=True)
    return xc * lax.rsqrt(var + LN_EPS) * g + b


def _sigmoid(x):
    return 1.0 / (1.0 + jnp.exp(-x))


def _log_sigmoid(x):
    return jnp.minimum(x, 0.0) - jnp.log1p(jnp.exp(-jnp.abs(x)))


def _chunk_cumsum(g, chunk):
    t = g.shape[0]
    shift = chunk.bit_length() - 1
    r = lax.broadcasted_iota(I32, (t, t), 0)
    c = lax.broadcasted_iota(I32, (t, t), 1)
    tri = jnp.where(((r >> shift) == (c >> shift)) & (c <= r), 1.0, 0.0).astype(BF16)
    g1 = g.astype(BF16)
    r1 = g - g1.astype(F32)
    g2 = r1.astype(BF16)
    g3 = (r1 - g2.astype(F32)).astype(BF16)
    return _dot(tri, g1) + _dot(tri, g2) + _dot(tri, g3)


def _gate_log_decay(hb, wa_ref, wa2_ref, bg_ref):
    a = _dot(hb, wa_ref[...])
    return _log_sigmoid(_dot(a.astype(BF16), wa2_ref[...]) + bg_ref[...]) * (1.0 / GATE_TAU)


def _gla_state_update(k, v, b, st):
    n = k.shape[0]
    b_last = b[n - 1:n, :]
    khat = (k * jnp.exp(b_last - b)).astype(BF16)
    return st * jnp.exp(b_last) + lax.dot_general(v.astype(BF16), khat, _DN_TA, preferred_element_type=F32)


def _gla_chunk(q, k, v, b, st):
    n = q.shape[0]
    vb = v.astype(BF16)
    q0 = (q * (jnp.exp(b) * QSCALE)).astype(BF16)
    o = lax.dot_general(q0, st.astype(BF16), _DN_TB, preferred_element_type=F32)
    col = lax.broadcasted_iota(I32, (SUB, n), 1)
    row = lax.broadcasted_iota(I32, (SUB, n), 0)
    blocks = []
    for i in range(n // SUB):
        lo, hi = SUB * i, SUB * (i + 1)
        bref = b[lo:lo + 1, :]
        qi = (q[lo:hi] * (jnp.exp(b[lo:hi] - bref) * QSCALE)).astype(BF16)
        kk = (k[:hi] * jnp.exp(bref - b[:hi])).astype(BF16)
        if hi < n:
            kk = jnp.concatenate([kk, jnp.zeros((n - hi, DK), BF16)], axis=0)
        a = lax.dot_general(qi, kk, _DN_TB, preferred_element_type=F32)
        blocks.append(jnp.where(col <= row + lo, a, 0.0))
    scores = blocks[0] if len(blocks) == 1 else jnp.concatenate(blocks, axis=0)
    o = o + _dot(scores.astype(BF16), vb)
    return o, _gla_state_update(k, v, b, st)


def _mixer_back(h, hb, o_ref, conv, wm_ref, ng_ref, wgo_ref, wco_ref, wo_ref, l1g_ref, l1b_ref):
    g = _dot(hb, wm_ref[:, G0:G0 + GLA_DV])
    parts = []
    for hd in range(HEADS):
        oh = o_ref[:, hd * DV:(hd + 1) * DV]
        ms = jnp.mean(oh * oh, axis=-1, keepdims=True)
        parts.append(oh * lax.rsqrt(ms + RMS_EPS) * ng_ref[...])
    on = jnp.concatenate(parts, axis=1) * (g * _sigmoid(g))
    branch_a = _dot(on.astype(BF16), wgo_ref[...])
    mix = _sigmoid(_dot(hb, wm_ref[:, GA0:GA0 + D_MODEL])) * branch_a
    yc = _dot(hb, wm_ref[:, CB0:CB0 + D_MODEL]) * conv
    branch_b = _dot(yc.astype(BF16), wco_ref[...])
    mix = mix + _sigmoid(_dot(hb, wm_ref[:, GB0:GB0 + D_MODEL])) * branch_b
    mixed = _dot(mix.astype(BF16), wo_ref[...])
    return _layer_norm(ALPHA * h + mixed, l1g_ref[...], l1b_ref[...])


def _conv_input(hb, wm_ref):
    return _dot(hb, wm_ref[:, CC0:CC0 + D_MODEL]) * _dot(hb, wm_ref[:, CH0:CH0 + D_MODEL])


def _gla_rows(qkv_ref, b_ref, rows, hd, st):
    kc = slice(hd * DK, (hd + 1) * DK)
    return _gla_chunk(qkv_ref[rows, Q0 + hd * DK:Q0 + (hd + 1) * DK],
                      qkv_ref[rows, K0 + hd * DK:K0 + (hd + 1) * DK],
                      qkv_ref[rows, V0 + hd * DV:V0 + (hd + 1) * DV],
                      b_ref[rows, kc], st)


def _mixer_kernel(meta_ref, xp_ref, xs_ref, sgin_ref, scin_ref,
                  lng_ref, lnb_ref, wm_ref, wa_ref, wa2_ref, bg_ref,
                  ng_ref, wgo_ref, cw_ref, wco_ref, wo_ref, l1g_ref, l1b_ref,
                  h1_ref, sgp_ref, scp_ref, sgs_ref, scs_ref,
                  st_ref, zbuf_ref, qkv_ref, b_ref, o_ref, *, n_prompt_tiles, seq, streams):
    i = pl.program_id(0)
    back = functools.partial(_mixer_back, wm_ref=wm_ref, ng_ref=ng_ref, wgo_ref=wgo_ref, wco_ref=wco_ref,
                             wo_ref=wo_ref, l1g_ref=l1g_ref, l1b_ref=l1b_ref)

    def front(x):
        h = _layer_norm(x, lng_ref[...], lnb_ref[...])
        hb = h.astype(BF16)
        return h, hb, _gate_log_decay(hb, wa_ref, wa2_ref, bg_ref)

    @pl.when(i == 0)
    def _meta():
        _, hb, gk = front(meta_ref[...])
        k = _dot(hb, wm_ref[:, K0:K0 + GLA_DK])
        v = _dot(hb, wm_ref[:, V0:V0 + GLA_DV])
        b = _chunk_cumsum(gk, N_META)
        zero_state = jnp.zeros((DV, DK), F32)
        for hd in range(HEADS):
            kc = slice(hd * DK, (hd + 1) * DK)
            st_ref[hd] = _gla_state_update(k[:, kc], v[:, hd * DV:(hd + 1) * DV], b[:, kc], zero_state)
        zbuf_ref[0:8, :] = _conv_input(hb, wm_ref)[N_META - 8:N_META, :]

    @pl.when((i >= 1) & (i <= n_prompt_tiles))
    def _prompt_tile():
        h, hb, gk = front(xp_ref[...])
        qkv_ref[...] = _dot(hb, wm_ref[:, Q0:G0])
        b_ref[...] = _chunk_cumsum(gk, PROMPT_CHUNK)
        for c in range(TM // PROMPT_CHUNK):
            rows = slice(c * PROMPT_CHUNK, (c + 1) * PROMPT_CHUNK)
            for hd in range(HEADS):
                o, st_new = _gla_rows(qkv_ref, b_ref, rows, hd, st_ref[hd])
                st_ref[hd] = st_new
                o_ref[rows, hd * DV:(hd + 1) * DV] = o
        z = _conv_input(hb, wm_ref)
        zbuf_ref[8:8 + TM, :] = z
        cw = cw_ref[...]
        conv = cw[0:1, :] * zbuf_ref[6:6 + TM, :] + cw[1:2, :] * zbuf_ref[7:7 + TM, :] + cw[2:3, :] * z
        zbuf_ref[0:8, :] = z[TM - 8:TM, :]
        _rows_to_tiles(h1_ref, back(h, hb, o_ref, conv))

        @pl.when(i == n_prompt_tiles)
        def _prompt_states():
            for hd in range(HEADS):
                sgp_ref[hd] = st_ref[hd].T
            scp_ref[...] = z[TM - 2:TM, :]

    @pl.when(i > n_prompt_tiles)
    def _sample_tile():
        h, hb, gk = front(xs_ref[...])
        qkv_ref[...] = _dot(hb, wm_ref[:, Q0:G0])
        b_ref[...] = _chunk_cumsum(gk, seq)
        z = _conv_input(hb, wm_ref)
        pitch = seq + 8
        cw = cw_ref[...]
        convs = []
        for s in range(streams):
            rows = slice(s * seq, (s + 1) * seq)
            for hd in range(HEADS):
                o, st_new = _gla_rows(qkv_ref, b_ref, rows, hd, sgin_ref[s, hd].T)
                sgs_ref[s, hd] = st_new.T
                o_ref[rows, hd * DV:(hd + 1) * DV] = o
            zs = z[rows, :]
            base = s * pitch
            zbuf_ref[base + 6:base + 8, :] = scin_ref[s]
            zbuf_ref[base + 8:base + 8 + seq, :] = zs
            convs.append(cw[0:1, :] * zbuf_ref[base + 6:base + 6 + seq, :]
                         + cw[1:2, :] * zbuf_ref[base + 7:base + 7 + seq, :] + cw[2:3, :] * zs)
            scs_ref[s] = zs[seq - 2:seq, :]
        _rows_to_tiles(h1_ref, back(h, hb, o_ref, jnp.concatenate(convs, axis=0)))


def _const_spec(shape):
    nd = len(shape)
    return pl.BlockSpec(shape, lambda i, _nd=nd: (0,) * _nd, pipeline_mode=pl.Buffered(1))


def _mixer_weight_specs():
    return [
        _const_spec((1, D_MODEL)), _const_spec((1, D_MODEL)),
        _const_spec((D_MODEL, W_MAIN_COLS)),
        _const_spec((D_MODEL, RANK_PAD)), _const_spec((RANK_PAD, GLA_DK)), _const_spec((1, GLA_DK)),
        _const_spec((1, DV)), _const_spec((GLA_DV, D_MODEL)),
        _const_spec((3, D_MODEL)), _const_spec((D_MODEL, D_MODEL)),
        _const_spec((D_MODEL, D_MODEL)),
        _const_spec((1, D_MODEL)), _const_spec((1, D_MODEL)),
    ]


def _mixer(meta, x_prompt, x_sample, state_gla, state_conv, weights):
    n_streams = state_conv.shape[0]
    seq = x_sample.shape[0] // n_streams
    streams = TM // seq
    npt = x_prompt.shape[0] // TM
    nst = n_streams // streams
    ptile = lambda i: (jnp.clip(i - 1, 0, npt - 1), 0)
    stile = lambda i: jnp.clip(i - 1 - npt, 0, nst - 1)
    kern = functools.partial(_mixer_kernel, n_prompt_tiles=npt, seq=seq, streams=streams)
    return pl.pallas_call(
        kern,
        grid=(1 + npt + nst,),
        in_specs=[_const_spec((N_META, D_MODEL)),
                  pl.BlockSpec((TM, D_MODEL), ptile),
                  pl.BlockSpec((TM, D_MODEL), lambda i: (stile(i), 0), pipeline_mode=pl.Buffered(1)),
                  pl.BlockSpec((streams, HEADS, DK, DV), lambda i: (stile(i), 0, 0, 0),
                               pipeline_mode=pl.Buffered(1)),
                  pl.BlockSpec((streams, 2, D_MODEL), lambda i: (stile(i), 0, 0))] + _mixer_weight_specs(),
        out_specs=[pl.BlockSpec((TM * SUBLANES, LANES), lambda i: (jnp.maximum(i - 1, 0), 0)),
                   pl.BlockSpec((HEADS, DK, DV), lambda i: (0, 0, 0)),
                   pl.BlockSpec((2, D_MODEL), lambda i: (0, 0)),
                   pl.BlockSpec((streams, HEADS, DK, DV), lambda i: (stile(i), 0, 0, 0)),
                   pl.BlockSpec((streams, 2, D_MODEL), lambda i: (stile(i), 0, 0))],
        out_shape=[jax.ShapeDtypeStruct(((x_prompt.shape[0] + x_sample.shape[0]) * SUBLANES, LANES), F32),
                   jax.ShapeDtypeStruct((HEADS, DK, DV), F32),
                   jax.ShapeDtypeStruct((2, D_MODEL), F32),
                   jax.ShapeDtypeStruct(state_gla.shape, F32),
                   jax.ShapeDtypeStruct(state_conv.shape, F32)],
        scratch_shapes=[pltpu.VMEM((HEADS, DV, DK), F32),
                        pltpu.VMEM((max(TM + 8, streams * (seq + 8)), D_MODEL), F32),
                        pltpu.VMEM((TM, G0), F32),
                        pltpu.VMEM((TM, GLA_DK), F32),
                        pltpu.VMEM((TM, GLA_DV), F32)],
        compiler_params=pltpu.CompilerParams(dimension_semantics=("arbitrary",), vmem_limit_bytes=VMEM_LIMIT),
        name="mixer",
    )(meta, x_prompt, x_sample, state_gla, state_conv, *weights)


def _first_index_of_max(vals, ids, sentinel):
    m = jnp.max(vals, axis=0, keepdims=True)
    first = jnp.min(jnp.where(vals == m, ids, sentinel), axis=0, keepdims=True)
    return m, first


def _router_kernel(h_ref, wr_ref, bias_ref, idx_ref, w_ref, rank_ref, counts_ref):
    @pl.when(pl.program_id(0) == 0)
    def _init():
        counts_ref[...] = jnp.zeros_like(counts_ref)

    h = _tiles_to_rows(h_ref, TN_ROUTER)
    hh = h.astype(BF16)
    hl = (h - hh.astype(F32)).astype(BF16)
    wh, wl = wr_ref[0], wr_ref[1]
    dg = functools.partial(lax.dot_general, dimension_numbers=_DN_TB, preferred_element_type=F32)
    logits = dg(wh, hh) + dg(wl, hh) + dg(wh, hl)
    scores = _sigmoid(logits)
    biased = scores + bias_ref[...]
    n_tok = biased.shape[1]
    neg_inf = jnp.float32(-jnp.inf)

    eid = lax.broadcasted_iota(I32, (N_EXPERTS, n_tok), 0).astype(F32)
    lid = lax.broadcasted_iota(I32, (GROUP_SIZE, n_tok), 0).astype(F32)
    group_scores = []
    for g in range(N_GROUPS):
        blk = biased[g * GROUP_SIZE:(g + 1) * GROUP_SIZE]
        m1, first = _first_index_of_max(blk, lid, float(GROUP_SIZE))
        m2 = jnp.max(jnp.where(lid == first, neg_inf, blk), axis=0, keepdims=True)
        group_scores.append(m1 + m2)
    gsc = jnp.concatenate(group_scores, axis=0)

    gid = lax.broadcasted_iota(I32, (N_GROUPS, n_tok), 0).astype(F32)
    keep = jnp.zeros((N_GROUPS, n_tok), F32)
    cur = gsc
    for _ in range(TOPK_GROUPS):
        _, first = _first_index_of_max(cur, gid, float(N_GROUPS))
        sel = gid == first
        keep = jnp.where(sel, 1.0, keep)
        cur = jnp.where(sel, neg_inf, cur)

    masked = jnp.concatenate(
        [jnp.where(keep[g:g + 1] > 0.5, biased[g * GROUP_SIZE:(g + 1) * GROUP_SIZE], neg_inf)
         for g in range(N_GROUPS)], axis=0)
    idxs, wts, sels = [], [], []
    cur = masked
    for _ in range(TOP_K):
        _, first = _first_index_of_max(cur, eid, float(N_EXPERTS))
        sel = eid == first
        idxs.append(first)
        sels.append(sel)
        wts.append(jnp.sum(jnp.where(sel, scores, 0.0), axis=0, keepdims=True))
        cur = jnp.where(sel, neg_inf, cur)
    w = jnp.concatenate(wts, axis=0)
    idx_ref[...] = jnp.concatenate(idxs, axis=0).astype(I32)
    w_ref[...] = w / jnp.sum(w, axis=0, keepdims=True) * ROUTED_SCALE

    chosen = jnp.where(sels[0], 1.0, 0.0)
    for sel in sels[1:]:
        chosen = chosen + jnp.where(sel, 1.0, 0.0)
    t_src = lax.broadcasted_iota(I32, (n_tok, n_tok), 0)
    t_dst = lax.broadcasted_iota(I32, (n_tok, n_tok), 1)
    earlier = jnp.where(t_src < t_dst, 1.0, 0.0).astype(BF16)
    arrivals = counts_ref[...] + _dot(chosen.astype(BF16), earlier)
    rank_ref[...] = jnp.concatenate(
        [jnp.sum(jnp.where(sel, arrivals, 0.0), axis=0, keepdims=True) for sel in sels], axis=0).astype(I32)
    counts_ref[...] += jnp.sum(chosen, axis=1, keepdims=True)


def _router(h1_tiles, wr_split, bias_col):
    n = h1_tiles.shape[0] // SUBLANES
    tok = pl.BlockSpec((TOP_K, TN_ROUTER), lambda i: (0, i))
    return pl.pallas_call(
        _router_kernel,
        grid=(n // TN_ROUTER,),
        in_specs=[pl.BlockSpec((TN_ROUTER * SUBLANES, LANES), lambda i: (i, 0)),
                  pl.BlockSpec((2, N_EXPERTS, D_MODEL), lambda i: (0, 0, 0)),
                  pl.BlockSpec((N_EXPERTS, 1), lambda i: (0, 0))],
        out_specs=[tok, tok, tok, pl.BlockSpec((N_EXPERTS, 1), lambda i: (0, 0))],
        out_shape=[jax.ShapeDtypeStruct((TOP_K, n), I32), jax.ShapeDtypeStruct((TOP_K, n), F32),
                   jax.ShapeDtypeStruct((TOP_K, n), I32), jax.ShapeDtypeStruct((N_EXPERTS, 1), F32)],
        compiler_params=pltpu.CompilerParams(dimension_semantics=("arbitrary",), vmem_limit_bytes=VMEM_LIMIT),
        name="router",
    )(h1_tiles, wr_split, bias_col)


def _positions_kernel(idx_ref, rank_ref, segx_ref, segy_ref, posx_ref, posy_ref):
    n_tok = idx_ref.shape[1]
    eid = lax.broadcasted_iota(I32, (N_EXPERTS, n_tok), 0)
    segx, segy = segx_ref[...], segy_ref[...]
    rows_x, rows_y = [], []
    for k in range(TOP_K):
        onehot = eid == idx_ref[k:k + 1, :]
        rows_x.append(jnp.sum(jnp.where(onehot, segx, 0.0), axis=0, keepdims=True))
        rows_y.append(jnp.sum(jnp.where(onehot, segy, 0.0), axis=0, keepdims=True))
    posx_ref[...] = jnp.concatenate(rows_x, axis=0).astype(I32) + rank_ref[...]
    posy_ref[...] = jnp.concatenate(rows_y, axis=0).astype(I32) + rank_ref[...]


def _positions(idx_t, rank_t, segx_col, segy_col):
    n = idx_t.shape[1]
    tok = pl.BlockSpec((TOP_K, TN_POSITIONS), lambda i: (0, i))
    table = pl.BlockSpec((N_EXPERTS, 1), lambda i: (0, 0))
    return pl.pallas_call(
        _positions_kernel,
        grid=(n // TN_POSITIONS,),
        in_specs=[tok, tok, table, table],
        out_specs=[tok, tok],
        out_shape=[jax.ShapeDtypeStruct((TOP_K, n), I32), jax.ShapeDtypeStruct((TOP_K, n), I32)],
        compiler_params=pltpu.CompilerParams(dimension_semantics=("arbitrary",)),
        name="positions",
    )(idx_t, rank_t, segx_col, segy_col)


def _tile_copy(src, src_tile, dst, dst_tile, sem):
    return pltpu.make_async_copy(src.at[pl.ds(pl.multiple_of(src_tile * SUBLANES, SUBLANES), SUBLANES)],
                                 dst.at[pl.ds(pl.multiple_of(dst_tile * SUBLANES, SUBLANES), SUBLANES)], sem)


def _chunk_rows(first_tile):
    return pl.ds(pl.multiple_of(first_tile * SUBLANES, SUBLANES), CH * SUBLANES)


def _dispatch_kernel(pos_ref, h_ref, x_hbm, zeros, sem, zsem, *, n_assign):
    @pl.when(pl.program_id(0) == 0)
    def _zero_slack():
        zeros[...] = jnp.zeros_like(zeros)
        fill = pltpu.make_async_copy(zeros, x_hbm.at[_chunk_rows(n_assign)], zsem)
        fill.start()
        fill.wait()

    def body(t, carry):
        for k in range(TOP_K):
            _tile_copy(h_ref, t, x_hbm, pos_ref[0, 0, k * TN_DISPATCH + t], sem.at[k]).start(priority=k % 2)
        return carry
    lax.fori_loop(0, TN_DISPATCH, body, 0, unroll=8)
    for k in range(TOP_K):
        pltpu.make_async_copy(h_ref, x_hbm.at[pl.ds(0, TN_DISPATCH * SUBLANES)], sem.at[k]).wait()


def _dispatch(pos_tiles, h1_tiles):
    n_t = pos_tiles.shape[0]
    n_assign = n_t * TN_DISPATCH * TOP_K
    return pl.pallas_call(
        functools.partial(_dispatch_kernel, n_assign=n_assign),
        grid=(n_t,),
        in_specs=[pl.BlockSpec((1, 1, TOP_K * TN_DISPATCH), lambda t: (t, 0, 0), memory_space=pltpu.SMEM),
                  pl.BlockSpec((TN_DISPATCH * SUBLANES, LANES), lambda t: (t, 0))],
        out_specs=pl.BlockSpec(memory_space=pl.ANY),
        out_shape=jax.ShapeDtypeStruct(((n_assign + CH) * SUBLANES, LANES), F32),
        scratch_shapes=[pltpu.VMEM((CH * SUBLANES, LANES), F32), pltpu.SemaphoreType.DMA((TOP_K,)),
                        pltpu.SemaphoreType.DMA(())],
        compiler_params=pltpu.CompilerParams(dimension_semantics=("arbitrary",), has_side_effects=True),
        name="dispatch",
    )(pos_tiles, h1_tiles)


def _expert_kernel(gfirst_ref, cnt_ref, seq_ref, nxt_ref, gtot_ref, src_ref, x_hbm, wg_hbm, wu_hbm, wd_hbm, y_hbm,
                   xbuf, ybuf, zbuf, wg32, wu32, wd32, wgb, wub, wdb, xsem, ysem, zsem, wsem, *, g_max):
    e = pl.program_id(0)
    cnt = cnt_ref[e]
    g_total = gtot_ref[0]

    def x_copy(g, slot):
        return pltpu.make_async_copy(x_hbm.at[_chunk_rows(src_ref[g])], xbuf.at[slot], xsem.at[slot])

    def y_copy(g, slot):
        return pltpu.make_async_copy(ybuf.at[slot], y_hbm.at[_chunk_rows(g * CH)], ysem.at[slot])

    def tail_fill(g):
        return pltpu.make_async_copy(zbuf, y_hbm.at[_chunk_rows(g * CH)], zsem)

    def for_tail_chunks(fn):
        def body(g, carry):
            fn(g)
            return carry
        lax.fori_loop(g_total, g_max, body, 0)

    def weight_copies(ex, slot):
        return (pltpu.make_async_copy(wg_hbm.at[ex], wg32.at[slot], wsem.at[slot]),
                pltpu.make_async_copy(wu_hbm.at[ex], wu32.at[slot], wsem.at[slot]),
                pltpu.make_async_copy(wd_hbm.at[ex], wd32.at[slot], wsem.at[slot]))

    @pl.when(e == 0)
    def _prime():
        for g in range(X_AHEAD):
            @pl.when(g < g_total)
            def _():
                x_copy(g, g).start()
        zbuf[...] = jnp.zeros_like(zbuf)
        for_tail_chunks(lambda g: tail_fill(g).start())

    @pl.when(cnt > 0)
    def _expert():
        seq, nxt = seq_ref[e], nxt_ref[e]
        wslot = lax.rem(seq, 2)

        @pl.when(seq == 0)
        def _fetch_own():
            for c in weight_copies(e, wslot):
                c.start(priority=1)

        @pl.when(nxt >= 0)
        def _prefetch_weights():
            for c in weight_copies(nxt, 1 - wslot):
                c.start(priority=1)

        for c in weight_copies(e, wslot):
            c.wait()
        wgb[...] = wg32[wslot].astype(BF16)
        wub[...] = wu32[wslot].astype(BF16)
        wdb[...] = wd32[wslot].astype(BF16)

        def process(g, width):
            for i in range(width):
                x_copy(0, lax.rem(g + i, NX)).wait()
            for i in range(width):
                ahead = g + i + X_AHEAD

                @pl.when(ahead < g_total)
                def _fetch_ahead():
                    x_copy(ahead, lax.rem(ahead, NX)).start()

            xb = jnp.concatenate([_tiles_to_rows(xbuf.at[lax.rem(g + i, NX)], CH) for i in range(width)],
                                 axis=0).astype(BF16)
            gate = _dot(xb, wgb[...])
            up = _dot(xb, wub[...])
            res = _dot(((gate * _sigmoid(gate)) * up).astype(BF16), wdb[...])
            for i in range(width):
                ys = lax.rem(g + i, NY)

                @pl.when(g + i >= NY)
                def _slot_written_back():
                    y_copy(0, ys).wait()

                _rows_to_tiles(ybuf.at[ys], res[i * CH:(i + 1) * CH])
                y_copy(g + i, ys).start()

        g_first = gfirst_ref[e]
        n_chunks = (cnt + CH - 1) // CH
        n_wide = n_chunks // GROUP
        rest = n_chunks - n_wide * GROUP

        def wide_group(j, carry):
            process(g_first + j * GROUP, GROUP)
            return carry

        lax.fori_loop(0, n_wide, wide_group, 0)
        g_rest = g_first + n_wide * GROUP
        width = GROUP // 2
        while width >= 1:
            @pl.when((rest & width) != 0)
            def _narrow_group(width=width):
                process(g_rest + (rest & ~(2 * width - 1)), width)
            width //= 2

    @pl.when(e == pl.num_programs(0) - 1)
    def _finish():
        for back in range(1, NY + 1):
            @pl.when(g_total - back >= 0)
            def _():
                y_copy(0, lax.rem(g_total - back, NY)).wait()

        for_tail_chunks(lambda g: tail_fill(0).wait())


def _max_chunks(n_assign):
    return n_assign // CH + N_EXPERTS


def _segments(counts, n_assign):
    n_chunks = (counts + CH - 1) // CH
    g_end = jnp.cumsum(n_chunks).astype(I32)
    g_first = g_end - n_chunks
    x_start = (jnp.cumsum(counts) - counts).astype(I32)
    chunk = jnp.arange(_max_chunks(n_assign), dtype=I32)
    eid = jnp.arange(N_EXPERTS, dtype=I32)
    owner = jnp.minimum(jnp.sum((g_end[None, :] <= chunk[:, None]).astype(I32), axis=1), N_EXPERTS - 1)
    onehot = owner[:, None] == eid[None, :]
    pick = lambda v: jnp.sum(jnp.where(onehot, v[None, :], 0), axis=1)
    src = jnp.minimum(pick(x_start) + (chunk - pick(g_first)) * CH, n_assign).astype(I32)
    has_rows = counts > 0
    seq = jnp.where(has_rows, jnp.cumsum(has_rows.astype(I32)) - 1, -1).astype(I32)
    nxt = jnp.min(jnp.where((eid[None, :] > eid[:, None]) & has_rows[None, :], eid[None, :], N_EXPERTS), axis=1)
    nxt = jnp.where(nxt == N_EXPERTS, -1, nxt).astype(I32)
    return dict(x_start=x_start, y_start=g_first * CH, g_first=g_first, g_total=g_end[-1:], src=src,
                seq=seq, nxt=nxt, counts=counts)


def _experts(seg, x_sorted, w_gate, w_up, w_down, n_assign):
    g_max = _max_chunks(n_assign)
    hbm = pl.BlockSpec(memory_space=pl.ANY)
    grid_spec = pltpu.PrefetchScalarGridSpec(
        num_scalar_prefetch=6,
        grid=(N_EXPERTS,),
        in_specs=[hbm, hbm, hbm, hbm],
        out_specs=hbm,
        scratch_shapes=[pltpu.VMEM((NX, CH * SUBLANES, LANES), F32),
                        pltpu.VMEM((NY, CH * SUBLANES, LANES), F32),
                        pltpu.VMEM((CH * SUBLANES, LANES), F32),
                        pltpu.VMEM((2, D_MODEL, EXPERT_HIDDEN), F32),
                        pltpu.VMEM((2, D_MODEL, EXPERT_HIDDEN), F32),
                        pltpu.VMEM((2, EXPERT_HIDDEN, D_MODEL), F32),
                        pltpu.VMEM((D_MODEL, EXPERT_HIDDEN), BF16),
                        pltpu.VMEM((D_MODEL, EXPERT_HIDDEN), BF16),
                        pltpu.VMEM((EXPERT_HIDDEN, D_MODEL), BF16),
                        pltpu.SemaphoreType.DMA((NX,)), pltpu.SemaphoreType.DMA((NY,)),
                        pltpu.SemaphoreType.DMA(()), pltpu.SemaphoreType.DMA((2,))],
    )
    return pl.pallas_call(
        functools.partial(_expert_kernel, g_max=g_max),
        grid_spec=grid_spec,
        out_shape=jax.ShapeDtypeStruct((g_max * CH * SUBLANES, LANES), F32),
        compiler_params=pltpu.CompilerParams(dimension_semantics=("arbitrary",), vmem_limit_bytes=VMEM_LIMIT,
                                             has_side_effects=True),
        name="experts",
    )(seg["g_first"], seg["counts"], seg["seq"], seg["nxt"], seg["g_total"], seg["src"],
      x_sorted, w_gate, w_up, w_down)


def _combine_kernel(pos_cur_ref, pos_nxt_ref, y_hbm, h_ref, w_ref, wsg_ref, wsu_ref, wsd_ref, l2g_ref, l2b_ref,
                    outp_ref, outs_ref, ybuf, sem, *, n_prompt_tiles):
    t = pl.program_id(0)
    n_t = pl.num_programs(0)
    slot = t % 2

    def gather_start(pos_ref, dst, dst_sem):
        def body(r, carry):
            for k in range(TOP_K):
                j = k * TN_COMBINE + r
                _tile_copy(y_hbm, pos_ref[0, 0, j], dst, j, dst_sem).start(priority=k % 2)
            return carry
        lax.fori_loop(0, TN_COMBINE, body, 0, unroll=8)

    @pl.when(t == 0)
    def _first():
        gather_start(pos_cur_ref, ybuf.at[0], sem.at[0])

    @pl.when(t + 1 < n_t)
    def _prefetch():
        gather_start(pos_nxt_ref, ybuf.at[1 - slot], sem.at[1 - slot])

    h = _tiles_to_rows(h_ref, TN_COMBINE)
    hb = h.astype(BF16)
    g = _dot(hb, wsg_ref[...])
    u = _dot(hb, wsu_ref[...])
    shared = _dot(((g * _sigmoid(g)) * u).astype(BF16), wsd_ref[...])

    yslot = ybuf.at[slot]
    pltpu.make_async_copy(y_hbm.at[pl.ds(0, TOP_K * TN_COMBINE * SUBLANES)], yslot, sem.at[slot]).wait()
    w = w_ref[...]
    wk = [jnp.broadcast_to(w[:, k:k + 1], (TN_COMBINE, LANES)) for k in range(TOP_K)]
    chunks = []
    for s in range(SUBLANES):
        acc = wk[0] * yslot[pl.ds(s, TN_COMBINE, stride=SUBLANES), :]
        for k in range(1, TOP_K):
            acc = acc + wk[k] * yslot[pl.ds(k * TN_COMBINE * SUBLANES + s, TN_COMBINE, stride=SUBLANES), :]
        chunks.append(acc)
    routed = jnp.concatenate(chunks, axis=1)
    out = _layer_norm(ALPHA * h + (routed + shared), l2g_ref[...], l2b_ref[...])

    @pl.when(t < n_prompt_tiles)
    def _prompt_rows():
        outp_ref[...] = out

    @pl.when(t >= n_prompt_tiles)
    def _sample_rows():
        outs_ref[...] = out


def _combine(pos_tiles, y_sorted, h1_tiles, w_tok, wsg, wsu, wsd, l2g, l2b, n_prompt):
    n = h1_tiles.shape[0] // SUBLANES
    n_t = n // TN_COMBINE
    npt = n_prompt // TN_COMBINE
    n_idx = TOP_K * TN_COMBINE
    cspec = lambda shape: pl.BlockSpec(shape, lambda t: (0,) * len(shape))
    return pl.pallas_call(
        functools.partial(_combine_kernel, n_prompt_tiles=npt),
        grid=(n_t,),
        in_specs=[pl.BlockSpec((1, 1, n_idx), lambda t: (t, 0, 0), memory_space=pltpu.SMEM),
                  pl.BlockSpec((1, 1, n_idx), lambda t: (jnp.minimum(t + 1, n_t - 1), 0, 0),
                               memory_space=pltpu.SMEM),
                  pl.BlockSpec(memory_space=pl.ANY),
                  pl.BlockSpec((TN_COMBINE * SUBLANES, LANES), lambda t: (t, 0)),
                  pl.BlockSpec((TN_COMBINE, TOP_K), lambda t: (t, 0)),
                  cspec((D_MODEL, SHARED_HIDDEN)), cspec((D_MODEL, SHARED_HIDDEN)), cspec((SHARED_HIDDEN, D_MODEL)),
                  cspec((1, D_MODEL)), cspec((1, D_MODEL))],
        out_specs=[pl.BlockSpec((TN_COMBINE, D_MODEL), lambda t: (jnp.minimum(t, npt - 1), 0)),
                   pl.BlockSpec((TN_COMBINE, D_MODEL), lambda t: (jnp.maximum(t - npt, 0), 0))],
        out_shape=[jax.ShapeDtypeStruct((n_prompt, D_MODEL), F32),
                   jax.ShapeDtypeStruct((n - n_prompt, D_MODEL), F32)],
        scratch_shapes=[pltpu.VMEM((2, n_idx * SUBLANES, LANES), F32), pltpu.SemaphoreType.DMA((2,))],
        compiler_params=pltpu.CompilerParams(dimension_semantics=("arbitrary",), vmem_limit_bytes=VMEM_LIMIT),
        name="combine",
    )(pos_tiles, pos_tiles, y_sorted, h1_tiles, w_tok, wsg, wsu, wsd, l2g, l2b)


def _pack_mixer_weights(ln_emb_g, ln_emb_b, w_in, w_gate_a2, b_gate, gla_norm_g, w_gla_out, conv_w,
                        w_conv_out, w_o, ln1_g, ln1_b):
    q, k, v, g, a, cb, cc, ch, ga, gb = jnp.split(
        w_in, [512, 1024, 2048, 3072, 3088, 4112, 5136, 6160, 7184], axis=-1)
    w_main = jnp.concatenate([q, k, v, g, cb, cc, ch, ga, gb], axis=-1).astype(BF16)
    w_a = jnp.pad(a, ((0, 0), (0, RANK_PAD - GATE_RANK))).astype(BF16)
    w_a2 = jnp.pad(w_gate_a2, ((0, RANK_PAD - GATE_RANK), (0, 0))).astype(BF16)
    row = lambda x: x.reshape(1, -1).astype(F32)
    return [row(ln_emb_g), row(ln_emb_b), w_main, w_a, w_a2, row(b_gate), row(gla_norm_g),
            w_gla_out.astype(BF16), conv_w.astype(F32), w_conv_out.astype(BF16), w_o.astype(BF16),
            row(ln1_g), row(ln1_b)]


def kernel(x_prompt, x_sample, state_gla, state_conv, meta_tokens, ln_emb_g, ln_emb_b, w_in, w_gate_a2, b_gate, gla_norm_g, w_gla_out, conv_w, w_conv_out, w_o, ln1_g, ln1_b, w_router, router_bias, w_exp_gate, w_exp_up, w_exp_down, w_sh_gate, w_sh_up, w_sh_down, ln2_g, ln2_b):
    batch, seq, _ = x_prompt.shape
    dec_batch, dec_seq, _ = x_sample.shape
    depth = w_in.shape[0]
    assert batch == 1 and depth == 1 and seq % TM == 0 and TM % dec_seq == 0 and dec_seq % SUB == 0
    n_prompt, n_sample = batch * seq, dec_batch * dec_seq
    n_total = n_prompt + n_sample
    assert n_sample % TM == 0 and n_total % TN_ROUTER == 0 and n_total % TN_POSITIONS == 0
    assert n_prompt % TN_COMBINE == 0 and n_sample % TN_COMBINE == 0 and n_total % TN_DISPATCH == 0
    n_assign = n_total * TOP_K

    weights = _pack_mixer_weights(ln_emb_g, ln_emb_b, w_in[0], w_gate_a2[0], b_gate[0], gla_norm_g[0],
                                  w_gla_out[0], conv_w[0], w_conv_out[0], w_o[0], ln1_g[0], ln1_b[0])
    h1, sg_p, sc_p, sg_s, sc_s = _mixer(meta_tokens.astype(F32), x_prompt.reshape(n_prompt, D_MODEL),
                                        x_sample.reshape(n_sample, D_MODEL), state_gla[0], state_conv[0], weights)

    wr_t = w_router[0].T
    wr_hi = wr_t.astype(BF16)
    wr_split = jnp.stack([wr_hi, (wr_t - wr_hi.astype(F32)).astype(BF16)])
    idx_t, w_t, rank_t, counts = _router(h1, wr_split, router_bias[0].reshape(N_EXPERTS, 1).astype(F32))

    seg = _segments(counts.reshape(N_EXPERTS).astype(I32), n_assign)
    table = lambda v: v.astype(F32).reshape(N_EXPERTS, 1)
    posx_t, posy_t = _positions(idx_t, rank_t, table(seg["x_start"]), table(seg["y_start"]))
    tiles = lambda p, tn: p.reshape(TOP_K, n_total // tn, tn).transpose(1, 0, 2).reshape(n_total // tn, 1, -1)

    x_sorted = _dispatch(tiles(posx_t, TN_DISPATCH), h1)
    y_sorted = _experts(seg, x_sorted, w_exp_gate[0], w_exp_up[0], w_exp_down[0], n_assign)
    y_prompt, y_sample = _combine(tiles(posy_t, TN_COMBINE), y_sorted, h1, w_t.T,
                                  w_sh_gate[0].astype(BF16), w_sh_up[0].astype(BF16), w_sh_down[0].astype(BF16),
                                  ln2_g[0].reshape(1, -1), ln2_b[0].reshape(1, -1), n_prompt)
    y_prompt = y_prompt.reshape(batch, seq, D_MODEL)
    y_sample = y_sample.reshape(dec_batch, dec_seq, D_MODEL)
    return (y_prompt, y_sample,
            sg_p.reshape(depth, batch, HEADS, DK, DV), sc_p.reshape(depth, batch, 2, D_MODEL),
            sg_s.reshape(depth, dec_batch, HEADS, DK, DV), sc_s.reshape(depth, dec_batch, 2, D_MODEL))
```

```python
import functools

import jax
import jax.numpy as jnp
from jax import lax
from jax.experimental import pallas as pl
from jax.experimental.pallas import tpu as pltpu

F32 = jnp.float32
BF16 = jnp.bfloat16
I32 = jnp.int32

D_MODEL = 1024
N_META = 16
HEADS = 4
DK = 128
DV = 256
GLA_DK = HEADS * DK
GLA_DV = HEADS * DV
GATE_RANK = 16
GATE_TAU = 16.0
N_EXPERTS = 256
TOP_K = 8
N_GROUPS = 8
GROUP_SIZE = N_EXPERTS // N_GROUPS
TOPK_GROUPS = 4
EXPERT_HIDDEN = 256
SHARED_HIDDEN = 256
ROUTED_SCALE = 2.5
LN_EPS = 1e-5
RMS_EPS = 1e-6
ALPHA = 2.0 ** 0.25
QSCALE = DK ** -0.5

Q0, K0, V0, G0, CB0, CC0, CH0, GA0, GB0 = 0, 512, 1024, 2048, 3072, 4096, 5120, 6144, 7168
W_MAIN_COLS = 8192
RANK_PAD = 128

SUB = 16
PROMPT_CHUNK = 64
TM = 256
TN_ROUTER = 512
TN_POSITIONS = 1536
TN_DISPATCH = 1536
CH = 64
GROUP = 8
X_AHEAD = 12
assert GROUP & (GROUP - 1) == 0
NX = X_AHEAD + GROUP
NY = 2 * GROUP
TN_COMBINE = 256
VMEM_LIMIT = 60 * 1024 * 1024
SUBLANES = 8
LANES = 128
assert D_MODEL == SUBLANES * LANES

_DN_TB = (((1,), (1,)), ((), ()))
_DN_TA = (((0,), (0,)), ((), ()))


def _dot(a, b):
    return jnp.dot(a, b, preferred_element_type=F32)


def _tiles_to_rows(ref, n, base=0):
    return jnp.concatenate(
        [ref[pl.ds(SUBLANES * base + s, n, stride=SUBLANES), :] for s in range(SUBLANES)], axis=1)


def _rows_to_tiles(ref, val):
    n = val.shape[0]
    for s in range(SUBLANES):
        ref[pl.ds(s, n, stride=SUBLANES), :] = val[:, s * LANES:(s + 1) * LANES]


def _layer_norm(x, g, b):
    mu = jnp.mean(x, axis=-1, keepdims=True)
    xc = x - mu
    var = jnp.mean(xc * xc, axis=-1, keepdims=True)
    return xc * lax.rsqrt(var + LN_EPS) * g + b


def _sigmoid(x):
    return 1.0 / (1.0 + jnp.exp(-x))


def _log_sigmoid(x):
    return jnp.minimum(x, 0.0) - jnp.log1p(jnp.exp(-jnp.abs(x)))


def _chunk_cumsum(g, chunk):
    t = g.shape[0]
    shift = chunk.bit_length() - 1
    r = lax.broadcasted_iota(I32, (t, t), 0)
    c = lax.broadcasted_iota(I32, (t, t), 1)
    tri = jnp.where(((r >> shift) == (c >> shift)) & (c <= r), 1.0, 0.0).astype(BF16)
    g1 = g.astype(BF16)
    r1 = g - g1.astype(F32)
    g2 = r1.astype(BF16)
    g3 = (r1 - g2.astype(F32)).astype(BF16)
    return _dot(tri, g1) + _dot(tri, g2) + _dot(tri, g3)


def _gate_log_decay(hb, wa_ref, wa2_ref, bg_ref):
    a = _dot(hb, wa_ref[...])
    return _log_sigmoid(_dot(a.astype(BF16), wa2_ref[...]) + bg_ref[...]) * (1.0 / GATE_TAU)


def _gla_state_update(k, v, b, st):
    n = k.shape[0]
    b_last = b[n - 1:n, :]
    khat = (k * jnp.exp(b_last - b)).astype(BF16)
    return st * jnp.exp(b_last) + lax.dot_general(v.astype(BF16), khat, _DN_TA, preferred_element_type=F32)


def _gla_chunk(q, k, v, b, st):
    n = q.shape[0]
    vb = v.astype(BF16)
    q0 = (q * (jnp.exp(b) * QSCALE)).astype(BF16)
    o = lax.dot_general(q0, st.astype(BF16), _DN_TB, preferred_element_type=F32)
    col = lax.broadcasted_iota(I32, (SUB, n), 1)
    row = lax.broadcasted_iota(I32, (SUB, n), 0)
    blocks = []
    for i in range(n // SUB):
        lo, hi = SUB * i, SUB * (i + 1)
        bref = b[lo:lo + 1, :]
        qi = (q[lo:hi] * (jnp.exp(b[lo:hi] - bref) * QSCALE)).astype(BF16)
        kk = (k[:hi] * jnp.exp(bref - b[:hi])).astype(BF16)
        if hi < n:
            kk = jnp.concatenate([kk, jnp.zeros((n - hi, DK), BF16)], axis=0)
        a = lax.dot_general(qi, kk, _DN_TB, preferred_element_type=F32)
        blocks.append(jnp.where(col <= row + lo, a, 0.0))
    scores = blocks[0] if len(blocks) == 1 else jnp.concatenate(blocks, axis=0)
    o = o + _dot(scores.astype(BF16), vb)
    return o, _gla_state_update(k, v, b, st)


def _mixer_back(h, hb, o_ref, conv, wm_ref, ng_ref, wgo_ref, wco_ref, wo_ref, l1g_ref, l1b_ref):
    g = _dot(hb, wm_ref[:, G0:G0 + GLA_DV])
    parts = []
    for hd in range(HEADS):
        oh = o_ref[:, hd * DV:(hd + 1) * DV]
        ms = jnp.mean(oh * oh, axis=-1, keepdims=True)
        parts.append(oh * lax.rsqrt(ms + RMS_EPS) * ng_ref[...])
    on = jnp.concatenate(parts, axis=1) * (g * _sigmoid(g))
    branch_a = _dot(on.astype(BF16), wgo_ref[...])
    mix = _sigmoid(_dot(hb, wm_ref[:, GA0:GA0 + D_MODEL])) * branch_a
    yc = _dot(hb, wm_ref[:, CB0:CB0 + D_MODEL]) * conv
    branch_b = _dot(yc.astype(BF16), wco_ref[...])
    mix = mix + _sigmoid(_dot(hb, wm_ref[:, GB0:GB0 + D_MODEL])) * branch_b
    mixed = _dot(mix.astype(BF16), wo_ref[...])
    return _layer_norm(ALPHA * h + mixed, l1g_ref[...], l1b_ref[...])


def _conv_input(hb, wm_ref):
    return _dot(hb, wm_ref[:, CC0:CC0 + D_MODEL]) * _dot(hb, wm_ref[:, CH0:CH0 + D_MODEL])


def _gla_rows(qkv_ref, b_ref, rows, hd, st):
    kc = slice(hd * DK, (hd + 1) * DK)
    return _gla_chunk(qkv_ref[rows, Q0 + hd * DK:Q0 + (hd + 1) * DK],
                      qkv_ref[rows, K0 + hd * DK:K0 + (hd + 1) * DK],
                      qkv_ref[rows, V0 + hd * DV:V0 + (hd + 1) * DV],
                      b_ref[rows, kc], st)


def _mixer_kernel(meta_ref, xp_ref, xs_ref, sgin_ref, scin_ref,
                  lng_ref, lnb_ref, wm_ref, wa_ref, wa2_ref, bg_ref,
                  ng_ref, wgo_ref, cw_ref, wco_ref, wo_ref, l1g_ref, l1b_ref,
                  h1_ref, sgp_ref, scp_ref, sgs_ref, scs_ref,
                  st_ref, zbuf_ref, qkv_ref, b_ref, o_ref, *, n_prompt_tiles, seq, streams):
    i = pl.program_id(0)
    back = functools.partial(_mixer_back, wm_ref=wm_ref, ng_ref=ng_ref, wgo_ref=wgo_ref, wco_ref=wco_ref,
                             wo_ref=wo_ref, l1g_ref=l1g_ref, l1b_ref=l1b_ref)

    def front(x):
        h = _layer_norm(x, lng_ref[...], lnb_ref[...])
        hb = h.astype(BF16)
        return h, hb, _gate_log_decay(hb, wa_ref, wa2_ref, bg_ref)

    @pl.when(i == 0)
    def _meta():
        _, hb, gk = front(meta_ref[...])
        k = _dot(hb, wm_ref[:, K0:K0 + GLA_DK])
        v = _dot(hb, wm_ref[:, V0:V0 + GLA_DV])
        b = _chunk_cumsum(gk, N_META)
        zero_state = jnp.zeros((DV, DK), F32)
        for hd in range(HEADS):
            kc = slice(hd * DK, (hd + 1) * DK)
            st_ref[hd] = _gla_state_update(k[:, kc], v[:, hd * DV:(hd + 1) * DV], b[:, kc], zero_state)
        zbuf_ref[0:8, :] = _conv_input(hb, wm_ref)[N_META - 8:N_META, :]

    @pl.when((i >= 1) & (i <= n_prompt_tiles))
    def _prompt_tile():
        h, hb, gk = front(xp_ref[...])
        qkv_ref[...] = _dot(hb, wm_ref[:, Q0:G0])
        b_ref[...] = _chunk_cumsum(gk, PROMPT_CHUNK)
        for c in range(TM // PROMPT_CHUNK):
            rows = slice(c * PROMPT_CHUNK, (c + 1) * PROMPT_CHUNK)
            for hd in range(HEADS):
                o, st_new = _gla_rows(qkv_ref, b_ref, rows, hd, st_ref[hd])
                st_ref[hd] = st_new
                o_ref[rows, hd * DV:(hd + 1) * DV] = o
        z = _conv_input(hb, wm_ref)
        zbuf_ref[8:8 + TM, :] = z
        cw = cw_ref[...]
        conv = cw[0:1, :] * zbuf_ref[6:6 + TM, :] + cw[1:2, :] * zbuf_ref[7:7 + TM, :] + cw[2:3, :] * z
        zbuf_ref[0:8, :] = z[TM - 8:TM, :]
        _rows_to_tiles(h1_ref, back(h, hb, o_ref, conv))

        @pl.when(i == n_prompt_tiles)
        def _prompt_states():
            for hd in range(HEADS):
                sgp_ref[hd] = st_ref[hd].T
            scp_ref[...] = z[TM - 2:TM, :]

    @pl.when(i > n_prompt_tiles)
    def _sample_tile():
        h, hb, gk = front(xs_ref[...])
        qkv_ref[...] = _dot(hb, wm_ref[:, Q0:G0])
        b_ref[...] = _chunk_cumsum(gk, seq)
        z = _conv_input(hb, wm_ref)
        pitch = seq + 8
        cw = cw_ref[...]
        convs = []
        for s in range(streams):
            rows = slice(s * seq, (s + 1) * seq)
            for hd in range(HEADS):
                o, st_new = _gla_rows(qkv_ref, b_ref, rows, hd, sgin_ref[s, hd].T)
                sgs_ref[s, hd] = st_new.T
                o_ref[rows, hd * DV:(hd + 1) * DV] = o
            zs = z[rows, :]
            base = s * pitch
            zbuf_ref[base + 6:base + 8, :] = scin_ref[s]
            zbuf_ref[base + 8:base + 8 + seq, :] = zs
            convs.append(cw[0:1, :] * zbuf_ref[base + 6:base + 6 + seq, :]
                         + cw[1:2, :] * zbuf_ref[base + 7:base + 7 + seq, :] + cw[2:3, :] * zs)
            scs_ref[s] = zs[seq - 2:seq, :]
        _rows_to_tiles(h1_ref, back(h, hb, o_ref, jnp.concatenate(convs, axis=0)))


def _const_spec(shape):
    nd = len(shape)
    return pl.BlockSpec(shape, lambda i, _nd=nd: (0,) * _nd, pipeline_mode=pl.Buffered(1))


def _mixer_weight_specs():
    return [
        _const_spec((1, D_MODEL)), _const_spec((1, D_MODEL)),
        _const_spec((D_MODEL, W_MAIN_COLS)),
        _const_spec((D_MODEL, RANK_PAD)), _const_spec((RANK_PAD, GLA_DK)), _const_spec((1, GLA_DK)),
        _const_spec((1, DV)), _const_spec((GLA_DV, D_MODEL)),
        _const_spec((3, D_MODEL)), _const_spec((D_MODEL, D_MODEL)),
        _const_spec((D_MODEL, D_MODEL)),
        _const_spec((1, D_MODEL)), _const_spec((1, D_MODEL)),
    ]


def _mixer(meta, x_prompt, x_sample, state_gla, state_conv, weights):
    n_streams = state_conv.shape[0]
    seq = x_sample.shape[0] // n_streams
    streams = TM // seq
    npt = x_prompt.shape[0] // TM
    nst = n_streams // streams
    ptile = lambda i: (jnp.clip(i - 1, 0, npt - 1), 0)
    stile = lambda i: jnp.clip(i - 1 - npt, 0, nst - 1)
    kern = functools.partial(_mixer_kernel, n_prompt_tiles=npt, seq=seq, streams=streams)
    return pl.pallas_call(
        kern,
        grid=(1 + npt + nst,),
        in_specs=[_const_spec((N_META, D_MODEL)),
                  pl.BlockSpec((TM, D_MODEL), ptile),
                  pl.BlockSpec((TM, D_MODEL), lambda i: (stile(i), 0), pipeline_mode=pl.Buffered(1)),
                  pl.BlockSpec((streams, HEADS, DK, DV), lambda i: (stile(i), 0, 0, 0),
                               pipeline_mode=pl.Buffered(1)),
                  pl.BlockSpec((streams, 2, D_MODEL), lambda i: (stile(i), 0, 0))] + _mixer_weight_specs(),
        out_specs=[pl.BlockSpec((TM * SUBLANES, LANES), lambda i: (jnp.maximum(i - 1, 0), 0)),
                   pl.BlockSpec((HEADS, DK, DV), lambda i: (0, 0, 0)),
                   pl.BlockSpec((2, D_MODEL), lambda i: (0, 0)),
                   pl.BlockSpec((streams, HEADS, DK, DV), lambda i: (stile(i), 0, 0, 0)),
                   pl.BlockSpec((streams, 2, D_MODEL), lambda i: (stile(i), 0, 0))],
        out_shape=[jax.ShapeDtypeStruct(((x_prompt.shape[0] + x_sample.shape[0]) * SUBLANES, LANES), F32),
                   jax.ShapeDtypeStruct((HEADS, DK, DV), F32),
                   jax.ShapeDtypeStruct((2, D_MODEL), F32),
                   jax.ShapeDtypeStruct(state_gla.shape, F32),
                   jax.ShapeDtypeStruct(state_conv.shape, F32)],
        scratch_shapes=[pltpu.VMEM((HEADS, DV, DK), F32),
                        pltpu.VMEM((max(TM + 8, streams * (seq + 8)), D_MODEL), F32),
                        pltpu.VMEM((TM, G0), F32),
                        pltpu.VMEM((TM, GLA_DK), F32),
                        pltpu.VMEM((TM, GLA_DV), F32)],
        compiler_params=pltpu.CompilerParams(dimension_semantics=("arbitrary",), vmem_limit_bytes=VMEM_LIMIT),
        name="mixer",
    )(meta, x_prompt, x_sample, state_gla, state_conv, *weights)


def _first_index_of_max(vals, ids, sentinel):
    m = jnp.max(vals, axis=0, keepdims=True)
    first = jnp.min(jnp.where(vals == m, ids, sentinel), axis=0, keepdims=True)
    return m, first


def _router_kernel(h_ref, wr_ref, bias_ref, idx_ref, w_ref, rank_ref, counts_ref):
    @pl.when(pl.program_id(0) == 0)
    def _init():
        counts_ref[...] = jnp.zeros_like(counts_ref)

    h = _tiles_to_rows(h_ref, TN_ROUTER)
    hh = h.astype(BF16)
    hl = (h - hh.astype(F32)).astype(BF16)
    wh, wl = wr_ref[0], wr_ref[1]
    dg = functools.partial(lax.dot_general, dimension_numbers=_DN_TB, preferred_element_type=F32)
    logits = dg(wh, hh) + dg(wl, hh) + dg(wh, hl)
    scores = _sigmoid(logits)
    biased = scores + bias_ref[...]
    n_tok = biased.shape[1]
    neg_inf = jnp.float32(-jnp.inf)

    eid = lax.broadcasted_iota(I32, (N_EXPERTS, n_tok), 0).astype(F32)
    lid = lax.broadcasted_iota(I32, (GROUP_SIZE, n_tok), 0).astype(F32)
    group_scores = []
    for g in range(N_GROUPS):
        blk = biased[g * GROUP_SIZE:(g + 1) * GROUP_SIZE]
        m1, first = _first_index_of_max(blk, lid, float(GROUP_SIZE))
        m2 = jnp.max(jnp.where(lid == first, neg_inf, blk), axis=0, keepdims=True)
        group_scores.append(m1 + m2)
    gsc = jnp.concatenate(group_scores, axis=0)

    gid = lax.broadcasted_iota(I32, (N_GROUPS, n_tok), 0).astype(F32)
    keep = jnp.zeros((N_GROUPS, n_tok), F32)
    cur = gsc
    for _ in range(TOPK_GROUPS):
        _, first = _first_index_of_max(cur, gid, float(N_GROUPS))
        sel = gid == first
        keep = jnp.where(sel, 1.0, keep)
        cur = jnp.where(sel, neg_inf, cur)

    masked = jnp.concatenate(
        [jnp.where(keep[g:g + 1] > 0.5, biased[g * GROUP_SIZE:(g + 1) * GROUP_SIZE], neg_inf)
         for g in range(N_GROUPS)], axis=0)
    idxs, wts, sels = [], [], []
    cur = masked
    for _ in range(TOP_K):
        _, first = _first_index_of_max(cur, eid, float(N_EXPERTS))
        sel = eid == first
        idxs.append(first)
        sels.append(sel)
        wts.append(jnp.sum(jnp.where(sel, scores, 0.0), axis=0, keepdims=True))
        cur = jnp.where(sel, neg_inf, cur)
    w = jnp.concatenate(wts, axis=0)
    idx_ref[...] = jnp.concatenate(idxs, axis=0).astype(I32)
    w_ref[...] = w / jnp.sum(w, axis=0, keepdims=True) * ROUTED_SCALE

    chosen = jnp.where(sels[0], 1.0, 0.0)
    for sel in sels[1:]:
        chosen = chosen + jnp.where(sel, 1.0, 0.0)
    t_src = lax.broadcasted_iota(I32, (n_tok, n_tok), 0)
    t_dst = lax.broadcasted_iota(I32, (n_tok, n_tok), 1)
    earlier = jnp.where(t_src < t_dst, 1.0, 0.0).astype(BF16)
    arrivals = counts_ref[...] + _dot(chosen.astype(BF16), earlier)
    rank_ref[...] = jnp.concatenate(
        [jnp.sum(jnp.where(sel, arrivals, 0.0), axis=0, keepdims=True) for sel in sels], axis=0).astype(I32)
    counts_ref[...] += jnp.sum(chosen, axis=1, keepdims=True)


def _router(h1_tiles, wr_split, bias_col):
    n = h1_tiles.shape[0] // SUBLANES
    tok = pl.BlockSpec((TOP_K, TN_ROUTER), lambda i: (0, i))
    return pl.pallas_call(
        _router_kernel,
        grid=(n // TN_ROUTER,),
        in_specs=[pl.BlockSpec((TN_ROUTER * SUBLANES, LANES), lambda i: (i, 0)),
                  pl.BlockSpec((2, N_EXPERTS, D_MODEL), lambda i: (0, 0, 0)),
                  pl.BlockSpec((N_EXPERTS, 1), lambda i: (0, 0))],
        out_specs=[tok, tok, tok, pl.BlockSpec((N_EXPERTS, 1), lambda i: (0, 0))],
        out_shape=[jax.ShapeDtypeStruct((TOP_K, n), I32), jax.ShapeDtypeStruct((TOP_K, n), F32),
                   jax.ShapeDtypeStruct((TOP_K, n), I32), jax.ShapeDtypeStruct((N_EXPERTS, 1), F32)],
        compiler_params=pltpu.CompilerParams(dimension_semantics=("arbitrary",), vmem_limit_bytes=VMEM_LIMIT),
        name="router",
    )(h1_tiles, wr_split, bias_col)


def _positions_kernel(idx_ref, rank_ref, segx_ref, segy_ref, posx_ref, posy_ref):
    n_tok = idx_ref.shape[1]
    eid = lax.broadcasted_iota(I32, (N_EXPERTS, n_tok), 0)
    segx, segy = segx_ref[...], segy_ref[...]
    rows_x, rows_y = [], []
    for k in range(TOP_K):
        onehot = eid == idx_ref[k:k + 1, :]
        rows_x.append(jnp.sum(jnp.where(onehot, segx, 0.0), axis=0, keepdims=True))
        rows_y.append(jnp.sum(jnp.where(onehot, segy, 0.0), axis=0, keepdims=True))
    posx_ref[...] = jnp.concatenate(rows_x, axis=0).astype(I32) + rank_ref[...]
    posy_ref[...] = jnp.concatenate(rows_y, axis=0).astype(I32) + rank_ref[...]


def _positions(idx_t, rank_t, segx_col, segy_col):
    n = idx_t.shape[1]
    tok = pl.BlockSpec((TOP_K, TN_POSITIONS), lambda i: (0, i))
    table = pl.BlockSpec((N_EXPERTS, 1), lambda i: (0, 0))
    return pl.pallas_call(
        _positions_kernel,
        grid=(n // TN_POSITIONS,),
        in_specs=[tok, tok, table, table],
        out_specs=[tok, tok],
        out_shape=[jax.ShapeDtypeStruct((TOP_K, n), I32), jax.ShapeDtypeStruct((TOP_K, n), I32)],
        compiler_params=pltpu.CompilerParams(dimension_semantics=("arbitrary",)),
        name="positions",
    )(idx_t, rank_t, segx_col, segy_col)


def _tile_copy(src, src_tile, dst, dst_tile, sem):
    return pltpu.make_async_copy(src.at[pl.ds(pl.multiple_of(src_tile * SUBLANES, SUBLANES), SUBLANES)],
                                 dst.at[pl.ds(pl.multiple_of(dst_tile * SUBLANES, SUBLANES), SUBLANES)], sem)


def _chunk_rows(first_tile):
    return pl.ds(pl.multiple_of(first_tile * SUBLANES, SUBLANES), CH * SUBLANES)


def _dispatch_kernel(pos_ref, h_ref, x_hbm, zeros, sem, zsem, *, n_assign):
    @pl.when(pl.program_id(0) == 0)
    def _zero_slack():
        zeros[...] = jnp.zeros_like(zeros)
        fill = pltpu.make_async_copy(zeros, x_hbm.at[_chunk_rows(n_assign)], zsem)
        fill.start()
        fill.wait()

    def body(t, carry):
        for k in range(TOP_K):
            _tile_copy(h_ref, t, x_hbm, pos_ref[0, 0, k * TN_DISPATCH + t], sem.at[k]).start(priority=k % 2)
        return carry
    lax.fori_loop(0, TN_DISPATCH, body, 0, unroll=8)
    for k in range(TOP_K):
        pltpu.make_async_copy(h_ref, x_hbm.at[pl.ds(0, TN_DISPATCH * SUBLANES)], sem.at[k]).wait()


def _dispatch(pos_tiles, h1_tiles):
    n_t = pos_tiles.shape[0]
    n_assign = n_t * TN_DISPATCH * TOP_K
    return pl.pallas_call(
        functools.partial(_dispatch_kernel, n_assign=n_assign),
        grid=(n_t,),
        in_specs=[pl.BlockSpec((1, 1, TOP_K * TN_DISPATCH), lambda t: (t, 0, 0), memory_space=pltpu.SMEM),
                  pl.BlockSpec((TN_DISPATCH * SUBLANES, LANES), lambda t: (t, 0))],
        out_specs=pl.BlockSpec(memory_space=pl.ANY),
        out_shape=jax.ShapeDtypeStruct(((n_assign + CH) * SUBLANES, LANES), F32),
        scratch_shapes=[pltpu.VMEM((CH * SUBLANES, LANES), F32), pltpu.SemaphoreType.DMA((TOP_K,)),
                        pltpu.SemaphoreType.DMA(())],
        compiler_params=pltpu.CompilerParams(dimension_semantics=("arbitrary",), has_side_effects=True),
        name="dispatch",
    )(pos_tiles, h1_tiles)


def _expert_kernel(gfirst_ref, cnt_ref, seq_ref, nxt_ref, gtot_ref, src_ref, x_hbm, wg_hbm, wu_hbm, wd_hbm, y_hbm,
                   xbuf, ybuf, zbuf, wg32, wu32, wd32, xsem, ysem, zsem, wsem, *, g_max):
    e = pl.program_id(0)
    cnt = cnt_ref[e]
    g_total = gtot_ref[0]

    def x_copy(g, slot):
        return pltpu.make_async_copy(x_hbm.at[_chunk_rows(src_ref[g])], xbuf.at[slot], xsem.at[slot])

    def y_copy(g, slot):
        return pltpu.make_async_copy(ybuf.at[slot], y_hbm.at[_chunk_rows(g * CH)], ysem.at[slot])

    def tail_fill(g):
        return pltpu.make_async_copy(zbuf, y_hbm.at[_chunk_rows(g * CH)], zsem)

    def for_tail_chunks(fn):
        def body(g, carry):
            fn(g)
            return carry
        lax.fori_loop(g_total, g_max, body, 0)

    def weight_copies(ex, slot):
        return (pltpu.make_async_copy(wg_hbm.at[ex], wg32.at[slot], wsem.at[slot]),
                pltpu.make_async_copy(wu_hbm.at[ex], wu32.at[slot], wsem.at[slot]),
                pltpu.make_async_copy(wd_hbm.at[ex], wd32.at[slot], wsem.at[slot]))

    @pl.when(e == 0)
    def _prime():
        for g in range(X_AHEAD):
            @pl.when(g < g_total)
            def _():
                x_copy(g, g).start()
        zbuf[...] = jnp.zeros_like(zbuf)
        for_tail_chunks(lambda g: tail_fill(g).start())

    @pl.when(cnt > 0)
    def _expert():
        seq, nxt = seq_ref[e], nxt_ref[e]
        wslot = lax.rem(seq, 2)

        @pl.when(seq == 0)
        def _fetch_own():
            for c in weight_copies(e, wslot):
                c.start(priority=1)

        @pl.when(nxt >= 0)
        def _prefetch_weights():
            for c in weight_copies(nxt, 1 - wslot):
                c.start(priority=1)

        for c in weight_copies(e, wslot):
            c.wait()

        def process(g, width):
            for i in range(width):
                x_copy(0, lax.rem(g + i, NX)).wait()
            for i in range(width):
                ahead = g + i + X_AHEAD

                @pl.when(ahead < g_total)
                def _fetch_ahead():
                    x_copy(ahead, lax.rem(ahead, NX)).start()

            xb = jnp.concatenate([_tiles_to_rows(xbuf.at[lax.rem(g + i, NX)], CH) for i in range(width)],
                                 axis=0).astype(BF16)
            gate = _dot(xb, wg32[wslot].astype(BF16))
            up = _dot(xb, wu32[wslot].astype(BF16))
            res = _dot(((gate * _sigmoid(gate)) * up).astype(BF16), wd32[wslot].astype(BF16))
            for i in range(width):
                ys = lax.rem(g + i, NY)

                @pl.when(g + i >= NY)
                def _slot_written_back():
                    y_copy(0, ys).wait()

                _rows_to_tiles(ybuf.at[ys], res[i * CH:(i + 1) * CH])
                y_copy(g + i, ys).start()

        g_first = gfirst_ref[e]
        n_chunks = (cnt + CH - 1) // CH
        n_wide = n_chunks // GROUP
        rest = n_chunks - n_wide * GROUP

        def wide_group(j, carry):
            process(g_first + j * GROUP, GROUP)
            return carry

        lax.fori_loop(0, n_wide, wide_group, 0)
        g_rest = g_first + n_wide * GROUP
        width = GROUP // 2
        while width >= 1:
            @pl.when((rest & width) != 0)
            def _narrow_group(width=width):
                process(g_rest + (rest & ~(2 * width - 1)), width)
            width //= 2

    @pl.when(e == pl.num_programs(0) - 1)
    def _finish():
        for back in range(1, NY + 1):
            @pl.when(g_total - back >= 0)
            def _():
                y_copy(0, lax.rem(g_total - back, NY)).wait()

        for_tail_chunks(lambda g: tail_fill(0).wait())


def _max_chunks(n_assign):
    return n_assign // CH + N_EXPERTS


def _segments(counts, n_assign):
    n_chunks = (counts + CH - 1) // CH
    g_end = jnp.cumsum(n_chunks).astype(I32)
    g_first = g_end - n_chunks
    x_start = (jnp.cumsum(counts) - counts).astype(I32)
    chunk = jnp.arange(_max_chunks(n_assign), dtype=I32)
    eid = jnp.arange(N_EXPERTS, dtype=I32)
    owner = jnp.minimum(jnp.sum((g_end[None, :] <= chunk[:, None]).astype(I32), axis=1), N_EXPERTS - 1)
    onehot = owner[:, None] == eid[None, :]
    pick = lambda v: jnp.sum(jnp.where(onehot, v[None, :], 0), axis=1)
    src = jnp.minimum(pick(x_start) + (chunk - pick(g_first)) * CH, n_assign).astype(I32)
    has_rows = counts > 0
    seq = jnp.where(has_rows, jnp.cumsum(has_rows.astype(I32)) - 1, -1).astype(I32)
    nxt = jnp.min(jnp.where((eid[None, :] > eid[:, None]) & has_rows[None, :], eid[None, :], N_EXPERTS), axis=1)
    nxt = jnp.where(nxt == N_EXPERTS, -1, nxt).astype(I32)
    return dict(x_start=x_start, y_start=g_first * CH, g_first=g_first, g_total=g_end[-1:], src=src,
                seq=seq, nxt=nxt, counts=counts)


def _experts(seg, x_sorted, w_gate, w_up, w_down, n_assign):
    g_max = _max_chunks(n_assign)
    hbm = pl.BlockSpec(memory_space=pl.ANY)
    grid_spec = pltpu.PrefetchScalarGridSpec(
        num_scalar_prefetch=6,
        grid=(N_EXPERTS,),
        in_specs=[hbm, hbm, hbm, hbm],
        out_specs=hbm,
        scratch_shapes=[pltpu.VMEM((NX, CH * SUBLANES, LANES), F32),
                        pltpu.VMEM((NY, CH * SUBLANES, LANES), F32),
                        pltpu.VMEM((CH * SUBLANES, LANES), F32),
                        pltpu.VMEM((2, D_MODEL, EXPERT_HIDDEN), F32),
                        pltpu.VMEM((2, D_MODEL, EXPERT_HIDDEN), F32),
                        pltpu.VMEM((2, EXPERT_HIDDEN, D_MODEL), F32),
                        pltpu.SemaphoreType.DMA((NX,)), pltpu.SemaphoreType.DMA((NY,)),
                        pltpu.SemaphoreType.DMA(()), pltpu.SemaphoreType.DMA((2,))],
    )
    return pl.pallas_call(
        functools.partial(_expert_kernel, g_max=g_max),
        grid_spec=grid_spec,
        out_shape=jax.ShapeDtypeStruct((g_max * CH * SUBLANES, LANES), F32),
        compiler_params=pltpu.CompilerParams(dimension_semantics=("arbitrary",), vmem_limit_bytes=VMEM_LIMIT,
                                             has_side_effects=True),
        name="experts",
    )(seg["g_first"], seg["counts"], seg["seq"], seg["nxt"], seg["g_total"], seg["src"],
      x_sorted, w_gate, w_up, w_down)


def _combine_kernel(pos_cur_ref, pos_nxt_ref, y_hbm, h_ref, w_ref, wsg_ref, wsu_ref, wsd_ref, l2g_ref, l2b_ref,
                    outp_ref, outs_ref, ybuf, sem, *, n_prompt_tiles):
    t = pl.program_id(0)
    n_t = pl.num_programs(0)
    slot = t % 2

    def gather_start(pos_ref, dst, dst_sem):
        def body(r, carry):
            for k in range(TOP_K):
                j = k * TN_COMBINE + r
                _tile_copy(y_hbm, pos_ref[0, 0, j], dst, j, dst_sem).start(priority=k % 2)
            return carry
        lax.fori_loop(0, TN_COMBINE, body, 0, unroll=8)

    @pl.when(t == 0)
    def _first():
        gather_start(pos_cur_ref, ybuf.at[0], sem.at[0])

    @pl.when(t + 1 < n_t)
    def _prefetch():
        gather_start(pos_nxt_ref, ybuf.at[1 - slot], sem.at[1 - slot])

    h = _tiles_to_rows(h_ref, TN_COMBINE)
    hb = h.astype(BF16)
    g = _dot(hb, wsg_ref[...])
    u = _dot(hb, wsu_ref[...])
    shared = _dot(((g * _sigmoid(g)) * u).astype(BF16), wsd_ref[...])

    yslot = ybuf.at[slot]
    pltpu.make_async_copy(y_hbm.at[pl.ds(0, TOP_K * TN_COMBINE * SUBLANES)], yslot, sem.at[slot]).wait()
    w = w_ref[...]
    wk = [jnp.broadcast_to(w[:, k:k + 1], (TN_COMBINE, LANES)) for k in range(TOP_K)]
    chunks = []
    for s in range(SUBLANES):
        acc = wk[0] * yslot[pl.ds(s, TN_COMBINE, stride=SUBLANES), :]
        for k in range(1, TOP_K):
            acc = acc + wk[k] * yslot[pl.ds(k * TN_COMBINE * SUBLANES + s, TN_COMBINE, stride=SUBLANES), :]
        chunks.append(acc)
    routed = jnp.concatenate(chunks, axis=1)
    out = _layer_norm(ALPHA * h + (routed + shared), l2g_ref[...], l2b_ref[...])

    @pl.when(t < n_prompt_tiles)
    def _prompt_rows():
        outp_ref[...] = out

    @pl.when(t >= n_prompt_tiles)
    def _sample_rows():
        outs_ref[...] = out


def _combine(pos_tiles, y_sorted, h1_tiles, w_tok, wsg, wsu, wsd, l2g, l2b, n_prompt):
    n = h1_tiles.shape[0] // SUBLANES
    n_t = n // TN_COMBINE
    npt = n_prompt // TN_COMBINE
    n_idx = TOP_K * TN_COMBINE
    cspec = lambda shape: pl.BlockSpec(shape, lambda t: (0,) * len(shape))
    return pl.pallas_call(
        functools.partial(_combine_kernel, n_prompt_tiles=npt),
        grid=(n_t,),
        in_specs=[pl.BlockSpec((1, 1, n_idx), lambda t: (t, 0, 0), memory_space=pltpu.SMEM),
                  pl.BlockSpec((1, 1, n_idx), lambda t: (jnp.minimum(t + 1, n_t - 1), 0, 0),
                               memory_space=pltpu.SMEM),
                  pl.BlockSpec(memory_space=pl.ANY),
                  pl.BlockSpec((TN_COMBINE * SUBLANES, LANES), lambda t: (t, 0)),
                  pl.BlockSpec((TN_COMBINE, TOP_K), lambda t: (t, 0)),
                  cspec((D_MODEL, SHARED_HIDDEN)), cspec((D_MODEL, SHARED_HIDDEN)), cspec((SHARED_HIDDEN, D_MODEL)),
                  cspec((1, D_MODEL)), cspec((1, D_MODEL))],
        out_specs=[pl.BlockSpec((TN_COMBINE, D_MODEL), lambda t: (jnp.minimum(t, npt - 1), 0)),
                   pl.BlockSpec((TN_COMBINE, D_MODEL), lambda t: (jnp.maximum(t - npt, 0), 0))],
        out_shape=[jax.ShapeDtypeStruct((n_prompt, D_MODEL), F32),
                   jax.ShapeDtypeStruct((n - n_prompt, D_MODEL), F32)],
        scratch_shapes=[pltpu.VMEM((2, n_idx * SUBLANES, LANES), F32), pltpu.SemaphoreType.DMA((2,))],
        compiler_params=pltpu.CompilerParams(dimension_semantics=("arbitrary",), vmem_limit_bytes=VMEM_LIMIT),
        name="combine",
    )(pos_tiles, pos_tiles, y_sorted, h1_tiles, w_tok, wsg, wsu, wsd, l2g, l2b)


def _pack_mixer_weights(ln_emb_g, ln_emb_b, w_in, w_gate_a2, b_gate, gla_norm_g, w_gla_out, conv_w,
                        w_conv_out, w_o, ln1_g, ln1_b):
    q, k, v, g, a, cb, cc, ch, ga, gb = jnp.split(
        w_in, [512, 1024, 2048, 3072, 3088, 4112, 5136, 6160, 7184], axis=-1)
    w_main = jnp.concatenate([q, k, v, g, cb, cc, ch, ga, gb], axis=-1).astype(BF16)
    w_a = jnp.pad(a, ((0, 0), (0, RANK_PAD - GATE_RANK))).astype(BF16)
    w_a2 = jnp.pad(w_gate_a2, ((0, RANK_PAD - GATE_RANK), (0, 0))).astype(BF16)
    row = lambda x: x.reshape(1, -1).astype(F32)
    return [row(ln_emb_g), row(ln_emb_b), w_main, w_a, w_a2, row(b_gate), row(gla_norm_g),
            w_gla_out.astype(BF16), conv_w.astype(F32), w_conv_out.astype(BF16), w_o.astype(BF16),
            row(ln1_g), row(ln1_b)]


def kernel(x_prompt, x_sample, state_gla, state_conv, meta_tokens, ln_emb_g, ln_emb_b, w_in, w_gate_a2, b_gate, gla_norm_g, w_gla_out, conv_w, w_conv_out, w_o, ln1_g, ln1_b, w_router, router_bias, w_exp_gate, w_exp_up, w_exp_down, w_sh_gate, w_sh_up, w_sh_down, ln2_g, ln2_b):
    batch, seq, _ = x_prompt.shape
    dec_batch, dec_seq, _ = x_sample.shape
    depth = w_in.shape[0]
    assert batch == 1 and depth == 1 and seq % TM == 0 and TM % dec_seq == 0 and dec_seq % SUB == 0
    n_prompt, n_sample = batch * seq, dec_batch * dec_seq
    n_total = n_prompt + n_sample
    assert n_sample % TM == 0 and n_total % TN_ROUTER == 0 and n_total % TN_POSITIONS == 0
    assert n_prompt % TN_COMBINE == 0 and n_sample % TN_COMBINE == 0 and n_total % TN_DISPATCH == 0
    n_assign = n_total * TOP_K

    weights = _pack_mixer_weights(ln_emb_g, ln_emb_b, w_in[0], w_gate_a2[0], b_gate[0], gla_norm_g[0],
                                  w_gla_out[0], conv_w[0], w_conv_out[0], w_o[0], ln1_g[0], ln1_b[0])
    h1, sg_p, sc_p, sg_s, sc_s = _mixer(meta_tokens.astype(F32), x_prompt.reshape(n_prompt, D_MODEL),
                                        x_sample.reshape(n_sample, D_MODEL), state_gla[0], state_conv[0], weights)

    wr_t = w_router[0].T
    wr_hi = wr_t.astype(BF16)
    wr_split = jnp.stack([wr_hi, (wr_t - wr_hi.astype(F32)).astype(BF16)])
    idx_t, w_t, rank_t, counts = _router(h1, wr_split, router_bias[0].reshape(N_EXPERTS, 1).astype(F32))

    seg = _segments(counts.reshape(N_EXPERTS).astype(I32), n_assign)
    table = lambda v: v.astype(F32).reshape(N_EXPERTS, 1)
    posx_t, posy_t = _positions(idx_t, rank_t, table(seg["x_start"]), table(seg["y_start"]))
    tiles = lambda p, tn: p.reshape(TOP_K, n_total // tn, tn).transpose(1, 0, 2).reshape(n_total // tn, 1, -1)

    x_sorted = _dispatch(tiles(posx_t, TN_DISPATCH), h1)
    y_sorted = _experts(seg, x_sorted, w_exp_gate[0], w_exp_up[0], w_exp_down[0], n_assign)
    y_prompt, y_sample = _combine(tiles(posy_t, TN_COMBINE), y_sorted, h1, w_t.T,
                                  w_sh_gate[0].astype(BF16), w_sh_up[0].astype(BF16), w_sh_down[0].astype(BF16),
                                  ln2_g[0].reshape(1, -1), ln2_b[0].reshape(1, -1), n_prompt)
    y_prompt = y_prompt.reshape(batch, seq, D_MODEL)
    y_sample = y_sample.reshape(dec_batch, dec_seq, D_MODEL)
    return (y_prompt, y_sample,
            sg_p.reshape(depth, batch, HEADS, DK, DV), sc_p.reshape(depth, batch, 2, D_MODEL),
            sg_s.reshape(depth, dec_batch, HEADS, DK, DV), sc_s.reshape(depth, dec_batch, 2, D_MODEL))
```
